```python
import jax
import jax.numpy as jnp
from jax import lax
import numpy as np

D_MODEL = 2048
BATCH = 2
SEQ = 4096
DEPTH = 4
DEC_BATCH = 8
DEC_SEQ = 1
PAST_LEN = 16384
PAGE_SIZE = 128

N_EVEN = (DEPTH + 1) // 2
N_ODD = DEPTH // 2
A_HEAD_DIM = 128
A_HEADS = D_MODEL // 2 // A_HEAD_DIM
A_W = A_HEADS * A_HEAD_DIM
MOBA_BLOCK = 256
MOBA_TOPK = 3
MOBA_Q_CHUNK = 32
ROPE_THETA = 10000.0
B_HEADS = 4
B_V_DIM = D_MODEL // 2 // B_HEADS
B_QK_DIM = B_V_DIM // 2
B_QK_W = B_HEADS * B_QK_DIM
B_V_W = B_HEADS * B_V_DIM
MLSTM_CHUNK = 64
AB_IN = 3 * A_W + 2 * B_QK_W + 2 * B_V_W + 2 * B_HEADS
AB_OUT = A_W + B_V_W
RG_WIDTH = D_MODEL
RG_BLOCKS = 16
RG_BLOCK_DIM = RG_WIDTH // RG_BLOCKS
CONV_WIDTH = 4
RG_C = 8.0
D_FF = 256 * ((8 * D_MODEL // 3 + 255) // 256)
FFN_RES = 0.5
ALPHA = (2.0 * DEPTH) ** 0.25
BETA = (8.0 * DEPTH) ** -0.25
LN_EPS = 1e-5

kernel_name = 'moba_mlstm_rglru_macaron_deepnorm_step'

F32 = jnp.float32


def layer_norm(x, g, b):
    xf = x.astype(F32)
    mu = jnp.mean(xf, axis=-1, keepdims=True)
    var = jnp.mean(jnp.square(xf - mu), axis=-1, keepdims=True)
    return ((xf - mu) * lax.rsqrt(var + LN_EPS) * g + b).astype(x.dtype)


def modulate(x, mod_s):
    return x * (1.0 + mod_s[:, 1][:, None, :]) + mod_s[:, 0][:, None, :]


def post_norm(x, out, mod_s, res_w, g, b):
    return layer_norm(ALPHA * x + res_w * (1.0 + mod_s[:, 2][:, None, :]) * out, g, b)


def swiglu(u, w_in, w_out):
    gv = u @ w_in
    g, v = jnp.split(gv, 2, axis=-1)
    return (jax.nn.silu(g) * v) @ w_out


def rope(x, pos):
    half = x.shape[-1] // 2
    inv = ROPE_THETA ** (-jnp.arange(half, dtype=F32) / half)
    ang = pos.astype(F32)[:, None] * inv[None, :]
    cos, sin = jnp.cos(ang), jnp.sin(ang)
    xf = x.astype(F32)
    x1, x2 = xf[..., :half], xf[..., half:]
    return jnp.concatenate([x1 * cos - x2 * sin, x2 * cos + x1 * sin], axis=-1).astype(x.dtype)


def moba_attention(q, k_all, v_all, q_pos0):
    bsz, nh, lq, hd = q.shape
    t_len = k_all.shape[2]
    n_blk = -(-t_len // MOBA_BLOCK)
    pad = n_blk * MOBA_BLOCK - t_len
    kp = jnp.pad(k_all, ((0, 0), (0, 0), (0, pad), (0, 0)))
    vp = jnp.pad(v_all, ((0, 0), (0, 0), (0, pad), (0, 0)))
    kb = kp.reshape(bsz, nh, n_blk, MOBA_BLOCK, hd)
    vb = vp.reshape(bsz, nh, n_blk, MOBA_BLOCK, hd)
    k_mean = jnp.mean(kb.astype(F32), axis=3)
    n_sel = min(MOBA_TOPK, n_blk)
    lc = MOBA_Q_CHUNK if lq % MOBA_Q_CHUNK == 0 else lq
    n_chunks = lq // lc
    pos = q_pos0 + jnp.arange(lq, dtype=jnp.int32)
    q_chunks = jnp.moveaxis(q.reshape(bsz, nh, n_chunks, lc, hd), 2, 0)
    p_chunks = pos.reshape(n_chunks, lc)
    bi = jnp.arange(bsz)[:, None, None, None]
    hi = jnp.arange(nh)[None, :, None, None]
    blk_ids = jnp.arange(n_blk, dtype=jnp.int32)
    scale = hd ** -0.5

    def one_chunk(args):
        qq, pp = args
        jq = pp // MOBA_BLOCK
        gate = jnp.einsum('bhqd,bhnd->bhqn', qq.astype(F32), k_mean)
        gate = jnp.where(blk_ids[None, :] < jq[:, None], gate, -jnp.inf)
        _, top = lax.top_k(gate, n_sel)
        past_ok = jnp.arange(n_sel)[None, :] < jq[:, None]
        own = jq[:, None]
        idx = jnp.concatenate([jnp.where(past_ok, top, own),
                               jnp.broadcast_to(own, (bsz, nh, lc, 1))], axis=-1)
        slot_ok = jnp.concatenate([past_ok, jnp.ones((lc, 1), dtype=bool)], axis=-1)
        kg = kb[bi, hi, idx]
        vg = vb[bi, hi, idx]
        kpos = idx[..., None] * MOBA_BLOCK + jnp.arange(MOBA_BLOCK, dtype=jnp.int32)
        mask = slot_ok[:, :, None] & (kpos <= pp[:, None, None])
        s = jnp.einsum('bhqd,bhqsjd->bhqsj', qq, kg).astype(F32) * scale
        s = jnp.where(mask, s, -jnp.inf).reshape(bsz, nh, lc, -1)
        p = jax.nn.softmax(s, axis=-1).astype(vg.dtype)
        return jnp.einsum('bhqn,bhqnd->bhqd', p, vg.reshape(bsz, nh, lc, -1, hd))

    out = lax.map(one_chunk, (q_chunks, p_chunks))
    return jnp.moveaxis(out, 0, 2).reshape(bsz, nh, lq, hd)


def mlstm_chunkwise(q, k, v, i_pre, log_f, c0, n0, m0):
    bsz, nh, l, _ = q.shape
    dv = v.shape[-1]
    lc = MLSTM_CHUNK if l % MLSTM_CHUNK == 0 else l
    nc = l // lc

    def split(a):
        return jnp.moveaxis(a.reshape(bsz, nh, nc, lc, *a.shape[3:]), 2, 0)

    causal = jnp.tril(jnp.ones((lc, lc), dtype=bool))

    def step(carry, xs):
        c, n, m = carry
        qq, kk, vv, ii, ff = xs
        b = jnp.cumsum(ff, axis=-1)
        d = jnp.where(causal, b[..., :, None] - b[..., None, :] + ii[..., None, :], -jnp.inf)
        inter = b + m[..., None]
        m_t = jnp.maximum(jnp.max(d, axis=-1), inter)
        w = jnp.exp(d - m_t[..., None])
        s = jnp.einsum('bhtd,bhsd->bhts', qq, kk) * w
        wi = jnp.exp(inter - m_t)
        num = wi[..., None] * jnp.einsum('bhtd,bhde->bhte', qq, c) + jnp.einsum('bhts,bhse->bhte', s, vv)
        den = wi * jnp.einsum('bhtd,bhd->bht', qq, n) + jnp.sum(s, axis=-1)
        h = num / jnp.maximum(jnp.abs(den), jnp.exp(-m_t))[..., None]
        b_last = b[..., -1]
        g = b_last[..., None] - b + ii
        m_new = jnp.maximum(b_last + m, jnp.max(g, axis=-1))
        wc = jnp.exp(b_last + m - m_new)
        wg = jnp.exp(g - m_new[..., None])
        c_new = wc[..., None, None] * c + jnp.einsum('bhs,bhsd,bhse->bhde', wg, kk, vv)
        n_new = wc[..., None] * n + jnp.einsum('bhs,bhsd->bhd', wg, kk)
        return (c_new, n_new, m_new), h

    (c, n, m), h = lax.scan(step, (c0, n0, m0), (split(q), split(k), split(v), split(i_pre), split(log_f)))
    h = jnp.moveaxis(h, 0, 2).reshape(bsz, nh, l, dv)
    return h, c, n, m


def mixer_ab(u, pos0, k_past, v_past, c0, n0, m0, w_in, b_i, b_f, norm_g, w_out):
    bsz, l, _ = u.shape
    z = u @ w_in
    offs = [int(o) for o in np.cumsum([A_W, A_W, A_W, B_QK_W, B_QK_W, B_V_W, B_V_W, B_HEADS])]
    qa, ka, va, qb, kb, vb, ob, ib, fb = jnp.split(z, offs, axis=-1)

    def heads(t, h):
        return t.reshape(bsz, l, h, -1).transpose(0, 2, 1, 3)

    pos = pos0 + jnp.arange(l, dtype=jnp.int32)
    qa = rope(heads(qa, A_HEADS), pos)
    ka = rope(heads(ka, A_HEADS), pos)
    va = heads(va, A_HEADS)
    if k_past is None:
        k_all, v_all = ka, va
    else:
        k_all = jnp.concatenate([k_past, ka.astype(k_past.dtype)], axis=2)
        v_all = jnp.concatenate([v_past, va.astype(v_past.dtype)], axis=2)
    oa = moba_attention(qa, k_all, v_all, pos0)

    qm = heads(qb, B_HEADS).astype(F32)
    km = heads(kb, B_HEADS).astype(F32) * (B_QK_DIM ** -0.5)
    vm = heads(vb, B_HEADS).astype(F32)
    ig = (ib + b_i).astype(F32).transpose(0, 2, 1)
    lf = jax.nn.log_sigmoid((fb + b_f).astype(F32)).transpose(0, 2, 1)
    hm, c1, n1, m1 = mlstm_chunkwise(qm, km, vm, ig, lf, c0.astype(F32), n0.astype(F32), m0.astype(F32))
    hm = hm * lax.rsqrt(jnp.mean(hm * hm, axis=-1, keepdims=True) + LN_EPS)
    hm = hm.transpose(0, 2, 1, 3).reshape(bsz, l, B_V_W) * norm_g * jax.nn.sigmoid(ob.astype(F32))
    mix = jnp.concatenate([oa.transpose(0, 2, 1, 3).reshape(bsz, l, A_W).astype(u.dtype), hm.astype(u.dtype)], axis=-1)
    return mix @ w_out, ka.transpose(0, 2, 1, 3), va.transpose(0, 2, 1, 3), c1, n1, m1


def rglru_mixer(u, conv_buf, h0, w_in, cw, cb, w_a, b_a, w_x, b_x, lam, w_out):
    bsz, l, _ = u.shape
    z = u @ w_in
    gate_br, xr = jnp.split(z, 2, axis=-1)
    xc = jnp.concatenate([conv_buf.astype(xr.dtype), xr], axis=1)
    xconv = cb + xc[:, 0:l] * cw[0]
    for j in range(1, CONV_WIDTH):
        xconv = xconv + xc[:, j:j + l] * cw[j]
    new_buf = xc[:, -(CONV_WIDTH - 1):]
    xb = xconv.reshape(bsz, l, RG_BLOCKS, RG_BLOCK_DIM)
    r = jax.nn.sigmoid((jnp.einsum('blnd,nde->blne', xb, w_a).reshape(bsz, l, RG_WIDTH) + b_a).astype(F32))
    i = jax.nn.sigmoid((jnp.einsum('blnd,nde->blne', xb, w_x).reshape(bsz, l, RG_WIDTH) + b_x).astype(F32))
    log_a = -RG_C * r * jax.nn.softplus(-lam.astype(F32))
    a = jnp.exp(log_a)
    mult = jnp.sqrt(-jnp.expm1(2.0 * log_a))
    bterm = mult * i * xconv.astype(F32)
    bterm = bterm.at[:, 0].add(a[:, 0] * h0.astype(F32))

    def comb(e1, e2):
        a1, b1 = e1
        a2, b2 = e2
        return a1 * a2, a2 * b1 + b2

    _, h = lax.associative_scan(comb, (a, bterm), axis=1)
    y = (jax.nn.gelu(gate_br) * h.astype(u.dtype)) @ w_out
    return y, h[:, -1], new_buf


def gather_pages(pages, table):
    rows = pages[table].reshape(table.shape[0], -1, A_HEADS, A_HEAD_DIM)
    return rows.transpose(0, 2, 1, 3)


def setup_inputs(seed: int = 0) -> dict:
    key = jax.random.key(seed)
    ks = jax.random.split(key, 40)

    def nrm(k, shape, s):
        return s * jax.random.normal(k, shape, F32)

    n_pages = PAST_LEN // PAGE_SIZE
    used = DEC_BATCH * n_pages
    n_phys = used + max(1, used // 4)
    page_table = jax.random.permutation(ks[0], n_phys)[:used].reshape(DEC_BATCH, n_pages).astype(jnp.int32)
    u_lam = jax.random.uniform(ks[1], (N_ODD, RG_WIDTH), F32, 0.9, 0.999) ** (1.0 / RG_C)
    lru_lambda = jnp.log(u_lam) - jnp.log1p(-u_lam)
    b_fgate = jnp.linspace(3.0, 6.0, B_HEADS, dtype=F32)[None, :] + nrm(ks[2], (N_EVEN, B_HEADS), 0.1)
    return {
        'x_prompt': nrm(ks[3], (BATCH, SEQ, D_MODEL), 1.0),
        'x_sample': nrm(ks[4], (DEC_BATCH, DEC_SEQ, D_MODEL), 1.0),
        'cache_k': nrm(ks[5], (N_EVEN, n_phys, PAGE_SIZE, A_HEADS, A_HEAD_DIM), 1.0),
        'cache_v': nrm(ks[6], (N_EVEN, n_phys, PAGE_SIZE, A_HEADS, A_HEAD_DIM), 1.0),
        'state_mlstm_c': nrm(ks[7], (N_EVEN, DEC_BATCH, B_HEADS, B_QK_DIM, B_V_DIM), 0.5),
        'state_mlstm_n': nrm(ks[8], (N_EVEN, DEC_BATCH, B_HEADS, B_QK_DIM), 0.5),
        'state_mlstm_m': nrm(ks[9], (N_EVEN, DEC_BATCH, B_HEADS), 0.5),
        'state_rglru_h': nrm(ks[10], (N_ODD, DEC_BATCH, RG_WIDTH), 0.5),
        'state_conv': nrm(ks[11], (N_ODD, DEC_BATCH, CONV_WIDTH - 1, RG_WIDTH), 1.0),
        'page_table': page_table,
        'c_prompt': nrm(ks[12], (BATCH, D_MODEL), 1.0),
        'c_sample': nrm(ks[13], (DEC_BATCH, D_MODEL), 1.0),
        'w_ada': nrm(ks[14], (DEPTH, D_MODEL, 9 * D_MODEL), 0.5 * D_MODEL ** -0.5),
        'b_ada': nrm(ks[15], (DEPTH, 9 * D_MODEL), 0.01),
        'ln_g': 1.0 + nrm(ks[16], (DEPTH, 3, D_MODEL), 0.02),
        'ln_b': nrm(ks[17], (DEPTH, 3, D_MODEL), 0.02),
        'w_ffn1_in': nrm(ks[18], (DEPTH, D_MODEL, 2 * D_FF), D_MODEL ** -0.5),
        'w_ffn1_out': nrm(ks[19], (DEPTH, D_FF, D_MODEL), BETA * D_FF ** -0.5),
        'w_ffn2_in': nrm(ks[20], (DEPTH, D_MODEL, 2 * D_FF), D_MODEL ** -0.5),
        'w_ffn2_out': nrm(ks[21], (DEPTH, D_FF, D_MODEL), BETA * D_FF ** -0.5),
        'w_in_ab': nrm(ks[22], (N_EVEN, D_MODEL, AB_IN), D_MODEL ** -0.5),
        'b_igate': nrm(ks[23], (N_EVEN, B_HEADS), 0.1),
        'b_fgate': b_fgate,
        'mlstm_norm_g': 1.0 + nrm(ks[24], (N_EVEN, B_V_W), 0.02),
        'w_out_ab': nrm(ks[25], (N_EVEN, AB_OUT, D_MODEL), BETA * AB_OUT ** -0.5),
        'w_in_rg': nrm(ks[26], (N_ODD, D_MODEL, 2 * RG_WIDTH), D_MODEL ** -0.5),
        'conv_w': nrm(ks[27], (N_ODD, CONV_WIDTH, RG_WIDTH), CONV_WIDTH ** -0.5),
        'conv_b': nrm(ks[28], (N_ODD, RG_WIDTH), 0.01),
        'w_rg_a': nrm(ks[29], (N_ODD, RG_BLOCKS, RG_BLOCK_DIM, RG_BLOCK_DIM), RG_BLOCK_DIM ** -0.5),
        'b_rg_a': nrm(ks[30], (N_ODD, RG_WIDTH), 0.01),
        'w_rg_x': nrm(ks[31], (N_ODD, RG_BLOCKS, RG_BLOCK_DIM, RG_BLOCK_DIM), RG_BLOCK_DIM ** -0.5),
        'b_rg_x': nrm(ks[32], (N_ODD, RG_WIDTH), 0.01),
        'lru_lambda': lru_lambda,
        'w_out_rg': nrm(ks[33], (N_ODD, RG_WIDTH, D_MODEL), BETA * RG_WIDTH ** -0.5),
    }


def reference(x_prompt, x_sample, cache_k, cache_v, state_mlstm_c, state_mlstm_n, state_mlstm_m,
              state_rglru_h, state_conv, page_table, c_prompt, c_sample, w_ada, b_ada, ln_g, ln_b,
              w_ffn1_in, w_ffn1_out, w_ffn2_in, w_ffn2_out, w_in_ab, b_igate, b_fgate, mlstm_norm_g,
              w_out_ab, w_in_rg, conv_w, conv_b, w_rg_a, b_rg_a, w_rg_x, b_rg_x, lru_lambda, w_out_rg):

    def run_group(x, c, pos0, is_sample):
        bsz = x.shape[0]
        sil_c = jax.nn.silu(c)
        k_rows, v_rows, c_st, n_st, m_st, h_st, buf_st = [], [], [], [], [], [], []
        for l in range(DEPTH):
            mods = (sil_c @ w_ada[l] + b_ada[l]).reshape(bsz, 3, 3, D_MODEL)
            x = post_norm(x, swiglu(modulate(x, mods[:, 0]), w_ffn1_in[l], w_ffn1_out[l]),
                          mods[:, 0], FFN_RES, ln_g[l, 0], ln_b[l, 0])
            u = modulate(x, mods[:, 1])
            if l % 2 == 0:
                e = l // 2
                if is_sample:
                    k_past = gather_pages(cache_k[e], page_table)
                    v_past = gather_pages(cache_v[e], page_table)
                    c0, n0, m0 = state_mlstm_c[e], state_mlstm_n[e], state_mlstm_m[e]
                else:
                    k_past = v_past = None
                    c0 = jnp.zeros((bsz, B_HEADS, B_QK_DIM, B_V_DIM), F32)
                    n0 = jnp.zeros((bsz, B_HEADS, B_QK_DIM), F32)
                    m0 = jnp.zeros((bsz, B_HEADS), F32)
                out, k_new, v_new, c1, n1, m1 = mixer_ab(u, pos0, k_past, v_past, c0, n0, m0, w_in_ab[e],
                                                         b_igate[e], b_fgate[e], mlstm_norm_g[e], w_out_ab[e])
                k_rows.append(k_new)
                v_rows.append(v_new)
                c_st.append(c1)
                n_st.append(n1)
                m_st.append(m1)
            else:
                o = l // 2
                if is_sample:
                    h0, buf0 = state_rglru_h[o], state_conv[o]
                else:
                    h0 = jnp.zeros((bsz, RG_WIDTH), F32)
                    buf0 = jnp.zeros((bsz, CONV_WIDTH - 1, RG_WIDTH), x.dtype)
                out, h1, buf1 = rglru_mixer(u, buf0, h0, w_in_rg[o], conv_w[o], conv_b[o], w_rg_a[o], b_rg_a[o],
                                            w_rg_x[o], b_rg_x[o], lru_lambda[o], w_out_rg[o])
                h_st.append(h1)
                buf_st.append(buf1)
            x = post_norm(x, out, mods[:, 1], 1.0, ln_g[l, 1], ln_b[l, 1])
            x = post_norm(x, swiglu(modulate(x, mods[:, 2]), w_ffn2_in[l], w_ffn2_out[l]),
                          mods[:, 2], FFN_RES, ln_g[l, 2], ln_b[l, 2])
        return (x, jnp.stack(k_rows), jnp.stack(v_rows), jnp.stack(c_st), jnp.stack(n_st), jnp.stack(m_st),
                jnp.stack(h_st), jnp.stack(buf_st))

    y_p, k_p, v_p, c_p, n_p, m_p, h_p, b_p = run_group(x_prompt, c_prompt, 0, False)
    y_s, k_s, v_s, c_s, n_s, m_s, h_s, b_s = run_group(x_sample, c_sample, PAST_LEN, True)
    return (y_p, y_s, k_p, v_p, k_s, v_s, c_p, n_p, m_p, c_s, n_s, m_s, h_p, b_p, h_s, b_s)
```

```python
import functools

import numpy as np
import jax
import jax.numpy as jnp
from jax import lax
from jax.experimental import pallas as pl
from jax.experimental.pallas import tpu as pltpu

F32 = jnp.float32
BF16 = jnp.bfloat16

DEPTH = 4
MOBA_BLOCK = 256
MOBA_TOPK = 3
ROPE_THETA = 10000.0
A_HEAD_DIM = 128
B_HEADS = 4
MLSTM_CHUNK = 64
RG_BLOCKS = 16
CONV_WIDTH = 4
RG_C = 8.0
FFN_RES = 0.5
ALPHA = (2.0 * DEPTH) ** 0.25
LN_EPS = 1e-5

LANES = 128
SAMPLE_ROWS = 16
NEG_BIG = -1e30
MIB = 1024 * 1024


def _cparams(n_axes, vmem_mib):
    return pltpu.CompilerParams(
        dimension_semantics=("arbitrary",) * n_axes,
        vmem_limit_bytes=int(vmem_mib * MIB),
    )


def _sigmoid(x):
    return 1.0 / (1.0 + jnp.exp(-x))


def _softplus(x):
    return jnp.maximum(x, 0.0) + jnp.log1p(jnp.exp(-jnp.abs(x)))


def _gelu_tanh(x):
    return 0.5 * x * (1.0 + jnp.tanh(0.7978845608028654 * (x + 0.044715 * x * x * x)))


def _mods_kernel(c_ref, w_ref, b_ref, o_ref):
    c = c_ref[...]
    s = (c * _sigmoid(c)).astype(BF16)
    o_ref[...] = jnp.dot(s, w_ref[...].astype(BF16), preferred_element_type=F32) + b_ref[...]


def _mods(c_all, w_ada, b_ada):
    depth, k, n = w_ada.shape
    rows = c_all.shape[0]
    tn = 1024 if n % 1024 == 0 else n
    return pl.pallas_call(
        _mods_kernel,
        grid=(depth, n // tn),
        in_specs=[
            pl.BlockSpec((rows, k), lambda l, j: (0, 0)),
            pl.BlockSpec((None, k, tn), lambda l, j: (l, 0, j)),
            pl.BlockSpec((None, 1, tn), lambda l, j: (l, 0, j)),
        ],
        out_specs=pl.BlockSpec((None, rows, tn), lambda l, j: (l, 0, j)),
        out_shape=jax.ShapeDtypeStruct((depth, rows, n), F32),
        compiler_params=_cparams(2, 48),
        name="adaln_mods",
    )(c_all, w_ada, b_ada.reshape(depth, 1, n))


def _modulate_kernel(x_ref, sc_ref, sh_ref, u_ref):
    u_ref[...] = (x_ref[...] * (1.0 + sc_ref[...]) + sh_ref[...]).astype(u_ref.dtype)


def _modulate(x, sc_spec, sh_spec, mods, tm):
    m, d = x.shape
    return pl.pallas_call(
        _modulate_kernel,
        grid=(m // tm,),
        in_specs=[pl.BlockSpec((tm, d), lambda i: (i, 0)), sc_spec(1), sh_spec(1)],
        out_specs=pl.BlockSpec((tm, d), lambda i: (i, 0)),
        out_shape=jax.ShapeDtypeStruct((m, d), BF16),
        compiler_params=_cparams(1, 32),
        name="modulate0",
    )(x, mods, mods)


def _proj_kernel(x_ref, w_ref, o_ref, wb_ref):
    @pl.when(pl.program_id(1) == 0)
    def _():
        wb_ref[...] = w_ref[...].astype(BF16)

    o_ref[...] = jnp.dot(x_ref[...], wb_ref[...], preferred_element_type=F32).astype(o_ref.dtype)


def _proj_rope_kernel(x_ref, w_ref, cos_ref, sin_ref, o_ref, wb_ref):
    @pl.when(pl.program_id(1) == 0)
    def _():
        wb_ref[...] = w_ref[...].astype(BF16)

    z = jnp.dot(x_ref[...], wb_ref[...], preferred_element_type=F32)
    cos = cos_ref[...]
    sin = sin_ref[...]
    for j in range(z.shape[1] // A_HEAD_DIM):
        zj = z[:, j * A_HEAD_DIM:(j + 1) * A_HEAD_DIM]
        o_ref[:, j * A_HEAD_DIM:(j + 1) * A_HEAD_DIM] = (
            zj * cos + pltpu.roll(zj, A_HEAD_DIM // 2, 1) * sin).astype(o_ref.dtype)


def _swiglu_in_kernel(x_ref, wg_ref, wv_ref, o_ref, wgb_ref, wvb_ref):
    @pl.when(pl.program_id(1) == 0)
    def _():
        wgb_ref[...] = wg_ref[...].astype(BF16)
        wvb_ref[...] = wv_ref[...].astype(BF16)

    x = x_ref[...]
    g = jnp.dot(x, wgb_ref[...], preferred_element_type=F32)
    v = jnp.dot(x, wvb_ref[...], preferred_element_type=F32)
    o_ref[...] = (g * _sigmoid(g) * v).astype(o_ref.dtype)


def _proj(x, w, layer, col0, ncols, tm, tn, out_dtype, rope=None, name="proj"):
    m, k = x.shape
    c0 = col0 // tn
    in_specs = [
        pl.BlockSpec((tm, k), lambda j, i: (i, 0)),
        pl.BlockSpec((None, k, tn), lambda j, i: (layer, 0, c0 + j)),
    ]
    args = [x, w]
    body = _proj_kernel
    if rope is not None:
        cos, sin = rope
        period = cos.shape[0] // tm
        in_specs += [pl.BlockSpec((tm, A_HEAD_DIM), lambda j, i: (i % period, 0))] * 2
        args += [cos, sin]
        body = _proj_rope_kernel
    return pl.pallas_call(
        body,
        grid=(ncols // tn, m // tm),
        in_specs=in_specs,
        out_specs=pl.BlockSpec((tm, tn), lambda j, i: (i, j)),
        out_shape=jax.ShapeDtypeStruct((m, ncols), out_dtype),
        scratch_shapes=[pltpu.VMEM((k, tn), BF16)],
        compiler_params=_cparams(2, 48),
        name=name,
    )(*args)


def _swiglu_in(x, w, layer, tm, tn):
    m, k = x.shape
    dff = w.shape[2] // 2
    nv = dff // tn
    return pl.pallas_call(
        _swiglu_in_kernel,
        grid=(dff // tn, m // tm),
        in_specs=[
            pl.BlockSpec((tm, k), lambda j, i: (i, 0)),
            pl.BlockSpec((None, k, tn), lambda j, i: (layer, 0, j)),
            pl.BlockSpec((None, k, tn), lambda j, i: (layer, 0, nv + j)),
        ],
        out_specs=pl.BlockSpec((tm, tn), lambda j, i: (i, j)),
        out_shape=jax.ShapeDtypeStruct((m, dff), BF16),
        scratch_shapes=[pltpu.VMEM((k, tn), BF16), pltpu.VMEM((k, tn), BF16)],
        compiler_params=_cparams(2, 56),
        name="swiglu_in",
    )(x, w, w)


def _out_ln_kernel(a_ref, w_ref, x_ref, gate_ref, lng_ref, lnb_ref, sc_ref, sh_ref,
                   xo_ref, uo_ref, acc_ref, *, res_w):
    kk = pl.program_id(1)

    @pl.when(kk == 0)
    def _():
        acc_ref[...] = jnp.zeros_like(acc_ref)

    acc_ref[...] += jnp.dot(a_ref[...], w_ref[...].astype(BF16), preferred_element_type=F32)

    @pl.when(kk == pl.num_programs(1) - 1)
    def _():
        y = ALPHA * x_ref[...] + (res_w * (1.0 + gate_ref[...])) * acc_ref[...]
        mu = jnp.mean(y, axis=-1, keepdims=True)
        yc = y - mu
        var = jnp.mean(yc * yc, axis=-1, keepdims=True)
        xn = yc * lax.rsqrt(var + LN_EPS) * lng_ref[...] + lnb_ref[...]
        xo_ref[...] = xn
        uo_ref[...] = (xn * (1.0 + sc_ref[...]) + sh_ref[...]).astype(uo_ref.dtype)


def _out_ln(a, w, layer, x, mods, gate_spec, sc_spec, sh_spec, ln_g, ln_b, ln_row, res_w, tm, tk):
    m, k = a.shape
    d = w.shape[2]
    ln_spec = pl.BlockSpec((None, 1, d), lambda i, j: (ln_row, 0, 0))
    return pl.pallas_call(
        functools.partial(_out_ln_kernel, res_w=res_w),
        grid=(m // tm, k // tk),
        in_specs=[
            pl.BlockSpec((tm, tk), lambda i, j: (i, j)),
            pl.BlockSpec((None, tk, d), lambda i, j: (layer, j, 0)),
            pl.BlockSpec((tm, d), lambda i, j: (i, 0)),
            gate_spec(2), ln_spec, ln_spec, sc_spec(2), sh_spec(2),
        ],
        out_specs=[pl.BlockSpec((tm, d), lambda i, j: (i, 0)),
                   pl.BlockSpec((tm, d), lambda i, j: (i, 0))],
        out_shape=[jax.ShapeDtypeStruct((m, d), F32), jax.ShapeDtypeStruct((m, d), BF16)],
        scratch_shapes=[pltpu.VMEM((tm, d), F32)],
        compiler_params=_cparams(2, 56),
        name="out_postnorm",
    )(a, w, x, mods, ln_g, ln_b, mods, mods)


def _moba_prompt_kernel(q_ref, k_ref, v_ref, o_ref, kb_ref, vb_ref, kmean_ref, sel_ref,
                        m_ref, l_ref, acc_ref, *, nblk, scale):
    qi = pl.program_id(2)
    blk = MOBA_BLOCK
    hd = q_ref.shape[1]

    @pl.when(qi == 0)
    def _():
        kf = k_ref[...]
        kb_ref[...] = kf.astype(BF16)
        vb_ref[...] = v_ref[...].astype(BF16)
        kmean_ref[...] = jnp.zeros_like(kmean_ref)
        kmean_ref[0:nblk, :] = jnp.mean(kf.reshape(nblk, blk, hd), axis=1)

    q = q_ref[...]
    gate = lax.dot_general(q, kmean_ref[...], (((1,), (1,)), ((), ())),
                           precision=lax.Precision.HIGHEST, preferred_element_type=F32)
    lane = lax.broadcasted_iota(jnp.int32, gate.shape, 1)
    lanef = lane.astype(F32)
    g = jnp.where(lane < qi, gate, -jnp.inf)
    selm = jnp.zeros(gate.shape, F32)
    for _ in range(MOBA_TOPK):
        mx = jnp.max(g, axis=1, keepdims=True)
        cand = jnp.where(g == mx, lanef, 1e9)
        cand = jnp.where(mx > -jnp.inf, cand, 1e9)
        pick = lanef == jnp.min(cand, axis=1, keepdims=True)
        selm = jnp.where(pick, 1.0, selm)
        g = jnp.where(pick, -jnp.inf, g)
    sel_ref[...] = selm

    qb = q.astype(BF16)
    nt = (((1,), (1,)), ((), ()))
    start = pl.multiple_of(qi * blk, blk)
    s = lax.dot_general(qb, kb_ref[pl.ds(start, blk), :], nt, preferred_element_type=F32) * scale
    row = lax.broadcasted_iota(jnp.int32, s.shape, 0)
    col = lax.broadcasted_iota(jnp.int32, s.shape, 1)
    s = jnp.where(col <= row, s, NEG_BIG)
    m0 = jnp.max(s, axis=1, keepdims=True)
    p = jnp.exp(s - m0)
    m_ref[...] = m0
    l_ref[...] = jnp.sum(p, axis=1, keepdims=True)
    acc_ref[...] = jnp.dot(p.astype(BF16), vb_ref[pl.ds(start, blk), :], preferred_element_type=F32)

    def past_block(n, carry):
        st = pl.multiple_of(n * blk, blk)
        sn = lax.dot_general(qb, kb_ref[pl.ds(st, blk), :], nt, preferred_element_type=F32) * scale
        chosen = jnp.sum(jnp.where(lane == n, sel_ref[...], 0.0), axis=1, keepdims=True)
        sn = jnp.where(chosen > 0.5, sn, NEG_BIG)
        m_prev = m_ref[...]
        m_new = jnp.maximum(m_prev, jnp.max(sn, axis=1, keepdims=True))
        a = jnp.exp(m_prev - m_new)
        pn = jnp.exp(sn - m_new)
        l_ref[...] = a * l_ref[...] + jnp.sum(pn, axis=1, keepdims=True)
        acc_ref[...] = a * acc_ref[...] + jnp.dot(pn.astype(BF16), vb_ref[pl.ds(st, blk), :],
                                                  preferred_element_type=F32)
        m_ref[...] = m_new
        return carry

    lax.fori_loop(0, qi, past_block, 0)
    o_ref[...] = (acc_ref[...] / l_ref[...]).astype(o_ref.dtype)


def _moba_prompt(q, k, v, bsz):
    rows, width = q.shape
    seq = rows // bsz
    nh = width // A_HEAD_DIM
    nblk = seq // MOBA_BLOCK
    blk = MOBA_BLOCK
    hd = A_HEAD_DIM
    return pl.pallas_call(
        functools.partial(_moba_prompt_kernel, nblk=nblk, scale=hd ** -0.5),
        grid=(bsz, nh, nblk),
        in_specs=[
            pl.BlockSpec((blk, hd), lambda b, h, i: (b * nblk + i, h)),
            pl.BlockSpec((seq, hd), lambda b, h, i: (b, h)),
            pl.BlockSpec((seq, hd), lambda b, h, i: (b, h)),
        ],
        out_specs=pl.BlockSpec((blk, hd), lambda b, h, i: (b * nblk + i, h)),
        out_shape=jax.ShapeDtypeStruct((rows, width), BF16),
        scratch_shapes=[
            pltpu.VMEM((seq, hd), BF16), pltpu.VMEM((seq, hd), BF16),
            pltpu.VMEM((LANES, hd), F32), pltpu.VMEM((blk, LANES), F32),
            pltpu.VMEM((blk, 1), F32), pltpu.VMEM((blk, 1), F32), pltpu.VMEM((blk, hd), F32),
        ],
        compiler_params=_cparams(3, 32),
        name="moba_prompt",
    )(q, k, v)


def _mlstm_prompt_kernel(q_ref, k_ref, v_ref, og_ref, g_ref, gb_ref, ng_ref,
                         h_ref, c_ref, n_ref, m_ref, *, nb, nh, dk, dv):
    lc = q_ref.shape[1]

    @pl.when(pl.program_id(0) == 0)
    def _():
        c_ref[...] = jnp.zeros_like(c_ref)
        n_ref[...] = jnp.zeros_like(n_ref)
        m_ref[...] = jnp.zeros_like(m_ref)

    tt = lax.broadcasted_iota(jnp.int32, (lc, lc), 0)
    ss = lax.broadcasted_iota(jnp.int32, (lc, lc), 1)
    causal = ss <= tt
    eye = ss == tt
    nt = (((1,), (1,)), ((), ()))
    tn = (((0,), (0,)), ((), ()))
    gb = gb_ref[...]
    for b in range(nb):
        gates = g_ref[b] + gb
        for h in range(nh):
            bh = b * nh + h
            i_col = gates[:, h:h + 1]
            fpre = gates[:, nh + h:nh + h + 1]
            f_col = jnp.minimum(fpre, 0.0) - jnp.log1p(jnp.exp(-jnp.abs(fpre)))
            f_row = jnp.sum(jnp.where(eye, f_col, 0.0), axis=0, keepdims=True)
            i_row = jnp.sum(jnp.where(eye, i_col, 0.0), axis=0, keepdims=True)
            b_col = jnp.sum(jnp.where(causal, f_row, 0.0), axis=1, keepdims=True)
            b_row = jnp.sum(jnp.where(ss >= tt, f_col, 0.0), axis=0, keepdims=True)
            m_prev = m_ref[bh][:, 0:1]
            d = jnp.where(causal, b_col - b_row + i_row, NEG_BIG)
            inter = b_col + m_prev
            m_t = jnp.maximum(jnp.max(d, axis=1, keepdims=True), inter)
            w = jnp.exp(d - m_t)
            q = q_ref[b, :, h * dk:(h + 1) * dk]
            k = k_ref[b, :, h * dk:(h + 1) * dk] * (dk ** -0.5)
            vb = v_ref[b, :, h * dv:(h + 1) * dv].astype(BF16)
            qb = q.astype(BF16)
            s = lax.dot_general(qb, k.astype(BF16), nt, preferred_element_type=F32) * w
            wi = jnp.exp(inter - m_t)
            cst = c_ref[bh]
            nrow = n_ref[bh]
            num = wi * jnp.dot(qb, cst.astype(BF16), preferred_element_type=F32) + jnp.dot(
                s.astype(BF16), vb, preferred_element_type=F32)
            den = wi * jnp.sum(q * nrow, axis=1, keepdims=True) + jnp.sum(s, axis=1, keepdims=True)
            hh = num / jnp.maximum(jnp.abs(den), jnp.exp(-m_t))
            b_last = b_col[lc - 1:lc, :]
            g_col = b_last - b_col + i_col
            m_new = jnp.maximum(b_last + m_prev, jnp.max(g_col, axis=0, keepdims=True))
            wc = jnp.exp(b_last + m_prev - m_new)
            kw = k * jnp.exp(g_col - m_new)
            c_ref[bh] = wc * cst + lax.dot_general(kw.astype(BF16), vb, tn, preferred_element_type=F32)
            n_ref[bh] = wc * nrow + jnp.sum(kw, axis=0, keepdims=True)
            m_ref[bh] = jnp.broadcast_to(m_new, (1, LANES))
            hn = hh * lax.rsqrt(jnp.mean(hh * hh, axis=1, keepdims=True) + LN_EPS)
            hn = hn * ng_ref[:, h * dv:(h + 1) * dv] * _sigmoid(og_ref[b, :, h * dv:(h + 1) * dv])
            h_ref[b, :, h * dv:(h + 1) * dv] = hn.astype(h_ref.dtype)


def _mlstm_prompt(zm, gates, gate_bias, norm_g, bsz):
    rows, _ = zm.shape
    seq = rows // bsz
    nh = B_HEADS
    v_w = norm_g.shape[1]
    qk_w = v_w // 2
    dk, dv = qk_w // nh, v_w // nh
    lc = MLSTM_CHUNK
    z3 = zm.reshape(bsz, seq, zm.shape[1])
    g3 = gates.reshape(bsz, seq, LANES)
    nbh = bsz * nh
    full = lambda shape: pl.BlockSpec(shape, lambda c: (0,) * len(shape))
    return pl.pallas_call(
        functools.partial(_mlstm_prompt_kernel, nb=bsz, nh=nh, dk=dk, dv=dv),
        grid=(seq // lc,),
        in_specs=[
            pl.BlockSpec((bsz, lc, qk_w), lambda c: (0, c, 0)),
            pl.BlockSpec((bsz, lc, qk_w), lambda c: (0, c, 1)),
            pl.BlockSpec((bsz, lc, v_w), lambda c: (0, c, 1)),
            pl.BlockSpec((bsz, lc, v_w), lambda c: (0, c, 2)),
            pl.BlockSpec((bsz, lc, LANES), lambda c: (0, c, 0)),
            full((1, LANES)), full((1, v_w)),
        ],
        out_specs=[
            pl.BlockSpec((bsz, lc, v_w), lambda c: (0, c, 0)),
            full((nbh, dk, dv)), full((nbh, 1, dk)), full((nbh, 1, LANES)),
        ],
        out_shape=[
            jax.ShapeDtypeStruct((bsz, seq, v_w), BF16),
            jax.ShapeDtypeStruct((nbh, dk, dv), F32),
            jax.ShapeDtypeStruct((nbh, 1, dk), F32),
            jax.ShapeDtypeStruct((nbh, 1, LANES), F32),
        ],
        compiler_params=_cparams(1, 32),
        name="mlstm_prompt",
    )(z3, z3, z3, z3, g3, gate_bias, norm_g)


def _rglru_gates(xconv, wa_ref, ba, wx_ref, bx, lam):
    bd = wa_ref.shape[1]
    r_parts, i_parts = [], []
    for n in range(wa_ref.shape[0]):
        xb = xconv[:, n * bd:(n + 1) * bd].astype(BF16)
        r_parts.append(jnp.dot(xb, wa_ref[n].astype(BF16), preferred_element_type=F32))
        i_parts.append(jnp.dot(xb, wx_ref[n].astype(BF16), preferred_element_type=F32))
    r = _sigmoid(jnp.concatenate(r_parts, axis=1) + ba)
    ig = _sigmoid(jnp.concatenate(i_parts, axis=1) + bx)
    log_a = (-RG_C) * r * _softplus(-lam)
    a = jnp.exp(log_a)
    mult = jnp.sqrt(-jnp.tanh(log_a) * (a * a + 1.0))
    return a, mult * ig * xconv


def _rglru_prompt_kernel(gate_ref, xr_ref, cw_ref, cb_ref, wa_ref, ba_ref, wx_ref, bx_ref, lam_ref,
                         y_ref, hl_ref, buf_ref, xext_ref, a_ref, b_ref, hs_ref, hc_ref):
    t = pl.program_id(1)
    tt = xr_ref.shape[0]
    pad = 8
    nbuf = CONV_WIDTH - 1

    @pl.when(t == 0)
    def _():
        xext_ref[0:pad, :] = jnp.zeros((pad, xext_ref.shape[1]), F32)
        hc_ref[...] = jnp.zeros_like(hc_ref)

    @pl.when(t > 0)
    def _():
        xext_ref[0:pad, :] = xext_ref[tt:tt + pad, :]

    xr = xr_ref[...]
    xext_ref[pad:pad + tt, :] = xr
    xconv = cb_ref[...] + xext_ref[pad - nbuf:pad - nbuf + tt, :] * cw_ref[0:1, :]
    for j in range(1, CONV_WIDTH):
        xconv = xconv + xext_ref[pad - nbuf + j:pad - nbuf + j + tt, :] * cw_ref[j:j + 1, :]
    a, bterm = _rglru_gates(xconv, wa_ref, ba_ref[...], wx_ref, bx_ref[...], lam_ref[...])
    a_ref[...] = a
    b_ref[...] = bterm

    def step(i, h):
        h = a_ref[pl.ds(i, 1), :] * h + b_ref[pl.ds(i, 1), :]
        hs_ref[pl.ds(i, 1), :] = h
        return h

    h_last = lax.fori_loop(0, tt, step, hc_ref[...], unroll=8)
    hc_ref[...] = h_last
    y_ref[...] = (_gelu_tanh(gate_ref[...]) * hs_ref[...]).astype(y_ref.dtype)
    hl_ref[...] = h_last
    buf_ref[...] = xr[tt - nbuf:tt, :]


def _rglru_prompt(zr, layer, conv_w, conv_b, w_a, b_a, w_x, b_x, lam, bsz, tt):
    rows, w2 = zr.shape
    width = w2 // 2
    seq = rows // bsz
    nt = seq // tt
    nblk, bd = w_a.shape[1], w_a.shape[2]
    vec = lambda a: a.reshape(a.shape[0], 1, width)
    vspec = pl.BlockSpec((None, 1, width), lambda b, t: (layer, 0, 0))
    wspec = pl.BlockSpec((None, nblk, bd, bd), lambda b, t: (layer, 0, 0, 0))
    nbuf = CONV_WIDTH - 1
    return pl.pallas_call(
        _rglru_prompt_kernel,
        grid=(bsz, nt),
        in_specs=[
            pl.BlockSpec((tt, width), lambda b, t: (b * nt + t, 0)),
            pl.BlockSpec((tt, width), lambda b, t: (b * nt + t, 1)),
            pl.BlockSpec((None, CONV_WIDTH, width), lambda b, t: (layer, 0, 0)),
            vspec, wspec, vspec, wspec, vspec, vspec,
        ],
        out_specs=[
            pl.BlockSpec((tt, width), lambda b, t: (b * nt + t, 0)),
            pl.BlockSpec((None, 1, width), lambda b, t: (b, 0, 0)),
            pl.BlockSpec((None, nbuf, width), lambda b, t: (b, 0, 0)),
        ],
        out_shape=[
            jax.ShapeDtypeStruct((rows, width), BF16),
            jax.ShapeDtypeStruct((bsz, 1, width), F32),
            jax.ShapeDtypeStruct((bsz, nbuf, width), F32),
        ],
        scratch_shapes=[
            pltpu.VMEM((tt + 8, width), F32), pltpu.VMEM((tt, width), F32),
            pltpu.VMEM((tt, width), F32), pltpu.VMEM((tt, width), F32), pltpu.VMEM((1, width), F32),
        ],
        compiler_params=_cparams(2, 48),
        name="rglru_prompt",
    )(zr, zr, conv_w, vec(conv_b), w_a, vec(b_a), w_x, vec(b_x), vec(lam))


def _moba_gate_kernel(tbl_ref, q_ref, k_ref, sel_ref, ksum_ref, *, nheads, npages, pages_per_blk):
    b = pl.program_id(0)
    p = pl.program_id(1)
    hd = A_HEAD_DIM
    blk_row = p // pages_per_blk
    part = jnp.sum(k_ref[...], axis=0, keepdims=True)

    @pl.when(p % pages_per_blk == 0)
    def _():
        ksum_ref[pl.ds(blk_row, 1), :] = part

    @pl.when(p % pages_per_blk != 0)
    def _():
        ksum_ref[pl.ds(blk_row, 1), :] += part

    @pl.when(p == npages - 1)
    def _():
        nblk = ksum_ref.shape[0]
        inv = 1.0 / (pages_per_blk * k_ref.shape[0])
        rowi = lax.broadcasted_iota(jnp.int32, (nblk, 1), 0).astype(F32)
        out_r = lax.broadcasted_iota(jnp.int32, sel_ref.shape, 0)
        out_c = lax.broadcasted_iota(jnp.int32, sel_ref.shape, 1)
        out = jnp.zeros(sel_ref.shape, jnp.int32)
        qrow = q_ref[pl.ds(b, 1), :]
        for h in range(nheads):
            km = ksum_ref[:, h * hd:(h + 1) * hd] * inv
            g = jnp.sum(km * qrow[:, h * hd:(h + 1) * hd], axis=1, keepdims=True)
            for i in range(MOBA_TOPK):
                mx = jnp.max(g, axis=0, keepdims=True)
                idx = jnp.min(jnp.where(g == mx, rowi, 1e9), axis=0, keepdims=True)
                out = jnp.where((out_r == h) & (out_c == i), idx.astype(jnp.int32), out)
                g = jnp.where(rowi == idx, -jnp.inf, g)
        sel_ref[...] = out


def _moba_sample_gate(q, cache_k4, layer, tbl_flat, nreq, npages):
    page = cache_k4.shape[2]
    width = cache_k4.shape[3]
    nheads = width // A_HEAD_DIM
    ppb = MOBA_BLOCK // page
    nblk = npages // ppb
    grid_spec = pltpu.PrefetchScalarGridSpec(
        num_scalar_prefetch=1,
        grid=(nreq, npages),
        in_specs=[
            pl.BlockSpec(q.shape, lambda b, p, tbl: (0, 0)),
            pl.BlockSpec((None, None, page, width), lambda b, p, tbl: (layer, tbl[b * npages + p], 0, 0)),
        ],
        out_specs=pl.BlockSpec((None, 8, LANES), lambda b, p, tbl: (b, 0, 0)),
        scratch_shapes=[pltpu.VMEM((nblk, width), F32)],
    )
    return pl.pallas_call(
        functools.partial(_moba_gate_kernel, nheads=nheads, npages=npages, pages_per_blk=ppb),
        grid_spec=grid_spec,
        out_shape=jax.ShapeDtypeStruct((nreq, 8, LANES), jnp.int32),
        compiler_params=_cparams(2, 32),
        name="moba_sample_gate",
    )(tbl_flat, q, cache_k4)


def _moba_sample_attn_kernel(tbl_ref, sel_ref, q_ref, kn_ref, vn_ref, k_ref, v_ref, o_ref,
                             m_ref, l_ref, acc_ref, *, nsteps, scale):
    b = pl.program_id(0)
    j = pl.program_id(2)
    q = q_ref[pl.ds(b, 1), :]

    @pl.when(j == 0)
    def _():
        m_ref[...] = jnp.sum(q * kn_ref[pl.ds(b, 1), :], axis=1, keepdims=True) * scale
        l_ref[...] = jnp.ones_like(l_ref)
        acc_ref[...] = vn_ref[pl.ds(b, 1), :]

    s = jnp.sum(k_ref[...] * q, axis=1, keepdims=True) * scale
    m_prev = m_ref[...]
    m_new = jnp.maximum(m_prev, jnp.max(s, axis=0, keepdims=True))
    a = jnp.exp(m_prev - m_new)
    p = jnp.exp(s - m_new)
    l_ref[...] = a * l_ref[...] + jnp.sum(p, axis=0, keepdims=True)
    acc_ref[...] = a * acc_ref[...] + jnp.sum(p * v_ref[...], axis=0, keepdims=True)
    m_ref[...] = m_new

    @pl.when(j == nsteps - 1)
    def _():
        o_ref[...] = (acc_ref[...] / l_ref[...]).astype(o_ref.dtype)


def _moba_sample_attn(q, k_new, v_new, cache_k4, cache_v4, layer, tbl_flat, sel_flat, nreq, npages):
    page = cache_k4.shape[2]
    width = cache_k4.shape[3]
    hd = A_HEAD_DIM
    nheads = width // hd
    ppb = MOBA_BLOCK // page
    nsteps = MOBA_TOPK * ppb

    def page_map(b, h, j, tbl, sel):
        blk = sel[(b * 8 + h) * LANES + j // ppb]
        return (layer, tbl[b * npages + blk * ppb + j % ppb], 0, h)

    head = pl.BlockSpec((q.shape[0], hd), lambda b, h, j, tbl, sel: (0, h))
    grid_spec = pltpu.PrefetchScalarGridSpec(
        num_scalar_prefetch=2,
        grid=(nreq, nheads, nsteps),
        in_specs=[head, head, head,
                  pl.BlockSpec((None, None, page, hd), page_map),
                  pl.BlockSpec((None, None, page, hd), page_map)],
        out_specs=pl.BlockSpec((None, 1, hd), lambda b, h, j, tbl, sel: (b, 0, h)),
        scratch_shapes=[pltpu.VMEM((1, 1), F32), pltpu.VMEM((1, 1), F32), pltpu.VMEM((1, hd), F32)],
    )
    return pl.pallas_call(
        functools.partial(_moba_sample_attn_kernel, nsteps=nsteps, scale=hd ** -0.5),
        grid_spec=grid_spec,
        out_shape=jax.ShapeDtypeStruct((nreq, 1, width), F32),
        compiler_params=_cparams(3, 32),
        name="moba_sample_attn",
    )(tbl_flat, sel_flat, q, k_new, v_new, cache_k4, cache_v4)


def _mlstm_sample_kernel(q_ref, k_ref, v_ref, og_ref, g_ref, gb_ref, ng_ref, c0_ref, n0_ref, m0_ref,
                         h_ref, c_ref, n_ref, m_ref, *, nh, dk, dv):
    b = pl.program_id(0)
    rr = lax.broadcasted_iota(jnp.int32, (dk, dk), 0)
    cc = lax.broadcasted_iota(jnp.int32, (dk, dk), 1)
    eye = rr == cc
    gates = g_ref[pl.ds(b, 1), :] + gb_ref[...]
    q_all = q_ref[pl.ds(b, 1), :]
    k_all = k_ref[pl.ds(b, 1), :]
    v_all = v_ref[pl.ds(b, 1), :]
    og_all = og_ref[pl.ds(b, 1), :]
    for h in range(nh):
        ii = gates[:, h:h + 1]
        fpre = gates[:, nh + h:nh + h + 1]
        ff = jnp.minimum(fpre, 0.0) - jnp.log1p(jnp.exp(-jnp.abs(fpre)))
        m_prev = m0_ref[h][:, 0:1]
        q = q_all[:, h * dk:(h + 1) * dk]
        k = k_all[:, h * dk:(h + 1) * dk] * (dk ** -0.5)
        v = v_all[:, h * dv:(h + 1) * dv]
        inter = ff + m_prev
        m_t = jnp.maximum(ii, inter)
        s = jnp.sum(q * k, axis=1, keepdims=True) * jnp.exp(ii - m_t)
        wi = jnp.exp(inter - m_t)
        q_col = jnp.sum(jnp.where(eye, q, 0.0), axis=1, keepdims=True)
        k_col = jnp.sum(jnp.where(eye, k, 0.0), axis=1, keepdims=True)
        cst = c0_ref[h]
        nrow = n0_ref[h]
        num = wi * jnp.sum(q_col * cst, axis=0, keepdims=True) + s * v
        den = wi * jnp.sum(q * nrow, axis=1, keepdims=True) + s
        hh = num / jnp.maximum(jnp.abs(den), jnp.exp(-m_t))
        m_new = m_t
        wc = jnp.exp(inter - m_new)
        wg = jnp.exp(ii - m_new)
        c_ref[h] = wc * cst + (wg * k_col) * v
        n_ref[h] = wc * nrow + wg * k
        m_ref[h] = jnp.broadcast_to(m_new, (1, LANES))
        hn = hh * lax.rsqrt(jnp.mean(hh * hh, axis=1, keepdims=True) + LN_EPS)
        hn = hn * ng_ref[:, h * dv:(h + 1) * dv] * _sigmoid(og_all[:, h * dv:(h + 1) * dv])
        h_ref[:, h * dv:(h + 1) * dv] = hn.astype(h_ref.dtype)


def _mlstm_sample(zm, gates, gate_bias, norm_g, c0, n0, m0, layer, nreq):
    nh = B_HEADS
    v_w = norm_g.shape[1]
    qk_w = v_w // 2
    dk, dv = qk_w // nh, v_w // nh
    rows = zm.shape[0]
    full = lambda shape: pl.BlockSpec(shape, lambda b: (0,) * len(shape))
    return pl.pallas_call(
        functools.partial(_mlstm_sample_kernel, nh=nh, dk=dk, dv=dv),
        grid=(nreq,),
        in_specs=[
            pl.BlockSpec((rows, qk_w), lambda b: (0, 0)),
            pl.BlockSpec((rows, qk_w), lambda b: (0, 1)),
            pl.BlockSpec((rows, v_w), lambda b: (0, 1)),
            pl.BlockSpec((rows, v_w), lambda b: (0, 2)),
            full((rows, LANES)), full((1, LANES)), full((1, v_w)),
            pl.BlockSpec((None, None, nh, dk, dv), lambda b: (layer, b, 0, 0, 0)),
            pl.BlockSpec((None, None, nh, 1, dk), lambda b: (layer, b, 0, 0, 0)),
            pl.BlockSpec((None, None, nh, 1, LANES), lambda b: (layer, b, 0, 0, 0)),
        ],
        out_specs=[
            pl.BlockSpec((None, 1, v_w), lambda b: (b, 0, 0)),
            pl.BlockSpec((None, nh, dk, dv), lambda b: (b, 0, 0, 0)),
            pl.BlockSpec((None, nh, 1, dk), lambda b: (b, 0, 0, 0)),
            pl.BlockSpec((None, nh, 1, LANES), lambda b: (b, 0, 0, 0)),
        ],
        out_shape=[
            jax.ShapeDtypeStruct((nreq, 1, v_w), F32),
            jax.ShapeDtypeStruct((nreq, nh, dk, dv), F32),
            jax.ShapeDtypeStruct((nreq, nh, 1, dk), F32),
            jax.ShapeDtypeStruct((nreq, nh, 1, LANES), F32),
        ],
        compiler_params=_cparams(1, 32),
        name="mlstm_sample",
    )(zm, zm, zm, zm, gates, gate_bias, norm_g, c0, n0, m0)


def _rglru_sample_kernel(gate_ref, xr_ref, buf_ref, h0_ref, cw_ref, cb_ref, wa_ref, ba_ref, wx_ref,
                         bx_ref, lam_ref, y_ref, h_ref, nbuf_ref):
    nbuf = CONV_WIDTH - 1
    xr = xr_ref[...]
    xconv = cb_ref[...] + buf_ref[0] * cw_ref[0:1, :]
    for j in range(1, nbuf):
        xconv = xconv + buf_ref[j] * cw_ref[j:j + 1, :]
    xconv = xconv + xr * cw_ref[nbuf:nbuf + 1, :]
    a, bterm = _rglru_gates(xconv, wa_ref, ba_ref[...], wx_ref, bx_ref[...], lam_ref[...])
    h = a * h0_ref[...] + bterm
    h_ref[...] = h
    y = _gelu_tanh(gate_ref[...]) * h
    pad_rows = y_ref.shape[0] - y.shape[0]
    y_ref[...] = jnp.concatenate([y, jnp.zeros((pad_rows, y.shape[1]), F32)], axis=0).astype(y_ref.dtype)
    for j in range(nbuf - 1):
        nbuf_ref[j] = buf_ref[j + 1]
    nbuf_ref[nbuf - 1] = xr


def _rglru_sample(zr, layer, buf_t, h0, conv_w, conv_b, w_a, b_a, w_x, b_x, lam, nreq):
    rows, w2 = zr.shape
    width = w2 // 2
    nblk, bd = w_a.shape[1], w_a.shape[2]
    nbuf = CONV_WIDTH - 1
    vec = lambda a: a.reshape(a.shape[0], 1, width)
    vspec = pl.BlockSpec((None, 1, width), lambda i: (layer, 0, 0))
    wspec = pl.BlockSpec((None, nblk, bd, bd), lambda i: (layer, 0, 0, 0))
    return pl.pallas_call(
        _rglru_sample_kernel,
        grid=(1,),
        in_specs=[
            pl.BlockSpec((nreq, width), lambda i: (0, 0)),
            pl.BlockSpec((nreq, width), lambda i: (0, 1)),
            pl.BlockSpec((None, nbuf, nreq, width), lambda i: (layer, 0, 0, 0)),
            pl.BlockSpec((None, nreq, width), lambda i: (layer, 0, 0)),
            pl.BlockSpec((None, CONV_WIDTH, width), lambda i: (layer, 0, 0)),
            vspec, wspec, vspec, wspec, vspec, vspec,
        ],
        out_specs=[
            pl.BlockSpec((rows, width), lambda i: (0, 0)),
            pl.BlockSpec((nreq, width), lambda i: (0, 0)),
            pl.BlockSpec((nbuf, nreq, width), lambda i: (0, 0, 0)),
        ],
        out_shape=[
            jax.ShapeDtypeStruct((rows, width), BF16),
            jax.ShapeDtypeStruct((nreq, width), F32),
            jax.ShapeDtypeStruct((nbuf, nreq, width), F32),
        ],
        compiler_params=_cparams(1, 32),
        name="rglru_sample",
    )(zr, zr, buf_t, h0, conv_w, vec(conv_b), w_a, vec(b_a), w_x, vec(b_x), vec(lam))


def _rope_tables(pos):
    half = A_HEAD_DIM // 2
    inv = ROPE_THETA ** (-jnp.arange(half, dtype=F32) / half)
    ang = pos.astype(F32)[:, None] * inv[None, :]
    cos, sin = jnp.cos(ang), jnp.sin(ang)
    return jnp.concatenate([cos, cos], axis=1), jnp.concatenate([-sin, sin], axis=1)


def kernel(x_prompt, x_sample, cache_k, cache_v, state_mlstm_c, state_mlstm_n, state_mlstm_m,
           state_rglru_h, state_conv, page_table, c_prompt, c_sample, w_ada, b_ada, ln_g, ln_b,
           w_ffn1_in, w_ffn1_out, w_ffn2_in, w_ffn2_out, w_in_ab, b_igate, b_fgate, mlstm_norm_g,
           w_out_ab, w_in_rg, conv_w, conv_b, w_rg_a, b_rg_a, w_rg_x, b_rg_x, lru_lambda, w_out_rg):
    bsz, seq, d = x_prompt.shape
    nreq = x_sample.shape[0]
    depth = w_ada.shape[0]
    n_even = w_in_ab.shape[0]
    a_w = cache_k.shape[3] * cache_k.shape[4]
    v_w = mlstm_norm_g.shape[1]
    qk_w = v_w // 2
    m_w = 2 * qk_w + 2 * v_w
    nh_b = B_HEADS
    past_len = page_table.shape[1] * cache_k.shape[2]
    npages = page_table.shape[1]
    rows_p = bsz * seq
    rs = SAMPLE_ROWS

    tm = 512
    tn = 512
    tk = 512
    tt_rg = 256

    c_all = jnp.concatenate([c_prompt, c_sample, jnp.zeros((rs - bsz - nreq, d), F32)], axis=0)
    mods = _mods(c_all, w_ada, b_ada).reshape(depth, rs, 9, d)
    mods_p = jnp.transpose(mods[:, :bsz], (0, 2, 1, 3)).reshape(depth * 9 * bsz, 1, d)
    mods_s = jnp.transpose(mods[:, bsz:bsz + nreq], (0, 2, 1, 3))
    mods_s = jnp.pad(mods_s, ((0, 0), (0, 0), (0, rs - nreq), (0, 0))).reshape(depth * 9, rs, d)
    tiles_per_batch = seq // tm

    def mod_spec_p(l, s, j):
        base = ((l * 3 + s) * 3 + j) * bsz

        def make(n_axes):
            if n_axes == 1:
                return pl.BlockSpec((None, 1, d), lambda i: (base + i // tiles_per_batch, 0, 0))
            return pl.BlockSpec((None, 1, d), lambda i, k: (base + i // tiles_per_batch, 0, 0))
        return make

    def mod_spec_s(l, s, j):
        row = (l * 3 + s) * 3 + j

        def make(n_axes):
            if n_axes == 1:
                return pl.BlockSpec((None, rs, d), lambda i: (row, 0, 0))
            return pl.BlockSpec((None, rs, d), lambda i, k: (row, 0, 0))
        return make

    ln_g3 = ln_g.reshape(depth * 3, 1, d)
    ln_b3 = ln_b.reshape(depth * 3, 1, d)

    cos_p, sin_p = _rope_tables(jnp.arange(seq, dtype=jnp.int32))
    cos_s, sin_s = _rope_tables(jnp.full((rs,), past_len, dtype=jnp.int32))

    tbl_flat = page_table.reshape(-1).astype(jnp.int32)
    n_phys, page = cache_k.shape[1], cache_k.shape[2]
    cache_k4 = cache_k.reshape(n_even, n_phys, page, a_w)
    cache_v4 = cache_v.reshape(n_even, n_phys, page, a_w)
    n0_s = state_mlstm_n[:, :, :, None, :]
    m0_s = jnp.broadcast_to(state_mlstm_m[:, :, :, None, None], state_mlstm_m.shape + (1, LANES))
    conv_t = jnp.transpose(state_conv, (0, 2, 1, 3))
    gate_bias = jnp.pad(jnp.concatenate([b_igate, b_fgate], axis=1), ((0, 0), (0, LANES - 2 * nh_b)))
    w_gates = jnp.pad(w_in_ab[:, :, 3 * a_w + m_w:], ((0, 0), (0, 0), (0, LANES - 2 * nh_b)))

    xp = x_prompt.reshape(rows_p, d)
    xs = jnp.pad(x_sample.reshape(nreq, d), ((0, rs - nreq), (0, 0)))
    up = _modulate(xp, mod_spec_p(0, 0, 1), mod_spec_p(0, 0, 0), mods_p, tm)
    us = _modulate(xs, mod_spec_s(0, 0, 1), mod_spec_s(0, 0, 0), mods_s, rs)

    groups = [
        dict(x=xp, u=up, tm=tm, mods=mods_p, spec=mod_spec_p, sample=False),
        dict(x=xs, u=us, tm=rs, mods=mods_s, spec=mod_spec_s, sample=True),
    ]
    outs = [dict(k=[], v=[], c=[], n=[], m=[], h=[], buf=[]) for _ in groups]

    for l in range(depth):
        for gi, g in enumerate(groups):
            o = outs[gi]
            gtm = g["tm"]
            spec = g["spec"]

            def post(a, w, wl, s, res_w, nxt):
                nl, ns = nxt
                return _out_ln(a, w, wl, g["x"], g["mods"], spec(l, s, 2), spec(nl, ns, 1),
                               spec(nl, ns, 0), ln_g3, ln_b3, l * 3 + s, res_w, gtm, tk)

            act = _swiglu_in(g["u"], w_ffn1_in, l, gtm, tn)
            g["x"], g["u"] = post(act, w_ffn1_out, l, 0, FFN_RES, (l, 1))

            if l % 2 == 0:
                e = l // 2
                rope = (cos_s, sin_s) if g["sample"] else (cos_p, sin_p)
                q = _proj(g["u"], w_in_ab, e, 0, a_w, gtm, tn, F32, rope=rope, name="proj_q")
                k = _proj(g["u"], w_in_ab, e, a_w, a_w, gtm, tn, F32, rope=rope, name="proj_k")
                v = _proj(g["u"], w_in_ab, e, 2 * a_w, a_w, gtm, tn, F32, name="proj_v")
                zm = _proj(g["u"], w_in_ab, e, 3 * a_w, m_w, gtm, tn, F32, name="proj_mlstm")
                gates = _proj(g["u"], w_gates, e, 0, LANES, gtm, LANES, F32, name="proj_gates")
                gb = gate_bias[e:e + 1]
                ng = mlstm_norm_g[e:e + 1]
                if g["sample"]:
                    sel = _moba_sample_gate(q, cache_k4, e, tbl_flat, nreq, npages)
                    oa = _moba_sample_attn(q, k, v, cache_k4, cache_v4, e, tbl_flat,
                                           sel.reshape(-1), nreq, npages)
                    hm, c1, n1, m1 = _mlstm_sample(zm, gates, gb, ng, state_mlstm_c, n0_s, m0_s, e, nreq)
                    mix = jnp.concatenate([oa.reshape(nreq, a_w), hm.reshape(nreq, v_w)], axis=1)
                    mix = jnp.pad(mix, ((0, rs - nreq), (0, 0))).astype(BF16)
                    o["k"].append(k[:nreq].reshape(nreq, 1, -1, A_HEAD_DIM))
                    o["v"].append(v[:nreq].reshape(nreq, 1, -1, A_HEAD_DIM))
                    o["c"].append(c1)
                    o["n"].append(n1.reshape(nreq, nh_b, -1))
                    o["m"].append(m1[:, :, 0, 0])
                else:
                    oa = _moba_prompt(q, k, v, bsz)
                    hm, c1, n1, m1 = _mlstm_prompt(zm, gates, gb, ng, bsz)
                    mix = jnp.concatenate([oa, hm.reshape(rows_p, v_w)], axis=1)
                    o["k"].append(k.reshape(bsz, seq, -1, A_HEAD_DIM))
                    o["v"].append(v.reshape(bsz, seq, -1, A_HEAD_DIM))
                    o["c"].append(c1.reshape(bsz, nh_b, c1.shape[1], c1.shape[2]))
                    o["n"].append(n1.reshape(bsz, nh_b, -1))
                    o["m"].append(m1[:, 0, 0].reshape(bsz, nh_b))
                w_mix, wl = w_out_ab, e
            else:
                od = l // 2
                zr = _proj(g["u"], w_in_rg, od, 0, w_in_rg.shape[2], gtm, tn, F32, name="proj_rg")
                if g["sample"]:
                    mix, h1, nb = _rglru_sample(zr, od, conv_t, state_rglru_h, conv_w, conv_b, w_rg_a,
                                                b_rg_a, w_rg_x, b_rg_x, lru_lambda, nreq)
                    o["h"].append(h1)
                    o["buf"].append(jnp.transpose(nb, (1, 0, 2)))
                else:
                    mix, h1, nb = _rglru_prompt(zr, od, conv_w, conv_b, w_rg_a, b_rg_a, w_rg_x, b_rg_x,
                                                lru_lambda, bsz, tt_rg)
                    o["h"].append(h1.reshape(bsz, -1))
                    o["buf"].append(nb)
                w_mix, wl = w_out_rg, od
            g["x"], g["u"] = post(mix, w_mix, wl, 1, 1.0, (l, 2))

            act = _swiglu_in(g["u"], w_ffn2_in, l, gtm, tn)
            g["x"], g["u"] = post(act, w_ffn2_out, l, 2, FFN_RES, (min(l + 1, depth - 1), 0))

    op, os_ = outs
    st = jnp.stack
    return (groups[0]["x"].reshape(bsz, seq, d), groups[1]["x"][:nreq].reshape(nreq, 1, d),
            st(op["k"]), st(op["v"]), st(os_["k"]), st(os_["v"]),
            st(op["c"]), st(op["n"]), st(op["m"]), st(os_["c"]), st(os_["n"]), st(os_["m"]),
            st(op["h"]), st(op["buf"]), st(os_["h"]), st(os_["buf"]))
```

```python
import functools

import numpy as np
import jax
import jax.numpy as jnp
from jax import lax
from jax.experimental import pallas as pl
from jax.experimental.pallas import tpu as pltpu

F32 = jnp.float32
BF16 = jnp.bfloat16

DEPTH = 4
MOBA_BLOCK = 256
MOBA_TOPK = 3
MOBA_GROUP = 4
ROPE_THETA = 10000.0
A_HEAD_DIM = 128
B_HEADS = 4
MLSTM_CHUNK = 64
RG_BLOCKS = 16
CONV_WIDTH = 4
RG_C = 8.0
FFN_RES = 0.5
ALPHA = (2.0 * DEPTH) ** 0.25
LN_EPS = 1e-5

LANES = 128
SAMPLE_ROWS = 16
NEG_BIG = -1e30
MIB = 1024 * 1024


def _cparams(n_axes, vmem_mib):
    return pltpu.CompilerParams(
        dimension_semantics=("arbitrary",) * n_axes,
        vmem_limit_bytes=int(vmem_mib * MIB),
    )


def _sigmoid(x):
    return 1.0 / (1.0 + jnp.exp(-x))


def _softplus(x):
    return jnp.maximum(x, 0.0) + jnp.log1p(jnp.exp(-jnp.abs(x)))


def _gelu_tanh(x):
    return 0.5 * x * (1.0 + jnp.tanh(0.7978845608028654 * (x + 0.044715 * x * x * x)))


def _mods_kernel(c_ref, w_ref, b_ref, o_ref):
    c = c_ref[...]
    s = (c * _sigmoid(c)).astype(BF16)
    o_ref[...] = jnp.dot(s, w_ref[...].astype(BF16), preferred_element_type=F32) + b_ref[...]


def _mods(c_all, w_ada, b_ada):
    depth, k, n = w_ada.shape
    rows = c_all.shape[0]
    tn = 1024 if n % 1024 == 0 else n
    return pl.pallas_call(
        _mods_kernel,
        grid=(depth, n // tn),
        in_specs=[
            pl.BlockSpec((rows, k), lambda l, j: (0, 0)),
            pl.BlockSpec((None, k, tn), lambda l, j: (l, 0, j)),
            pl.BlockSpec((None, 1, tn), lambda l, j: (l, 0, j)),
        ],
        out_specs=pl.BlockSpec((None, rows, tn), lambda l, j: (l, 0, j)),
        out_shape=jax.ShapeDtypeStruct((depth, rows, n), F32),
        compiler_params=_cparams(2, 48),
        name="adaln_mods",
    )(c_all, w_ada, b_ada.reshape(depth, 1, n))


def _modulate_kernel(x_ref, sc_ref, sh_ref, u_ref):
    u_ref[...] = (x_ref[...] * (1.0 + sc_ref[...]) + sh_ref[...]).astype(u_ref.dtype)


def _modulate(x, sc_spec, sh_spec, mods, tm):
    m, d = x.shape
    return pl.pallas_call(
        _modulate_kernel,
        grid=(m // tm,),
        in_specs=[pl.BlockSpec((tm, d), lambda i: (i, 0)), sc_spec(tm), sh_spec(tm)],
        out_specs=pl.BlockSpec((tm, d), lambda i: (i, 0)),
        out_shape=jax.ShapeDtypeStruct((m, d), BF16),
        compiler_params=_cparams(1, 32),
        name="modulate0",
    )(x, mods, mods)


def _proj_kernel(x_ref, w_ref, o_ref, wb_ref):
    @pl.when(pl.program_id(1) == 0)
    def _():
        wb_ref[...] = w_ref[...].astype(BF16)

    o_ref[...] = jnp.dot(x_ref[...], wb_ref[...], preferred_element_type=F32).astype(o_ref.dtype)


def _proj_rope_kernel(x_ref, w_ref, cos_ref, sin_ref, o_ref, wb_ref):
    @pl.when(pl.program_id(1) == 0)
    def _():
        wb_ref[...] = w_ref[...].astype(BF16)

    z = jnp.dot(x_ref[...], wb_ref[...], preferred_element_type=F32)
    cos = cos_ref[...]
    sin = sin_ref[...]
    for j in range(z.shape[1] // A_HEAD_DIM):
        zj = z[:, j * A_HEAD_DIM:(j + 1) * A_HEAD_DIM]
        o_ref[:, j * A_HEAD_DIM:(j + 1) * A_HEAD_DIM] = (
            zj * cos + pltpu.roll(zj, A_HEAD_DIM // 2, 1) * sin).astype(o_ref.dtype)


def _swiglu_in_kernel(x_ref, wg_ref, wv_ref, o_ref, wgb_ref, wvb_ref):
    @pl.when(pl.program_id(1) == 0)
    def _():
        wgb_ref[...] = wg_ref[...].astype(BF16)
        wvb_ref[...] = wv_ref[...].astype(BF16)

    x = x_ref[...]
    g = jnp.dot(x, wgb_ref[...], preferred_element_type=F32)
    v = jnp.dot(x, wvb_ref[...], preferred_element_type=F32)
    o_ref[...] = (g * _sigmoid(g) * v).astype(o_ref.dtype)


def _proj(x, w, layer, col0, ncols, tm, tn, out_dtype, rope=None, name="proj"):
    m, k = x.shape
    c0 = col0 // tn
    in_specs = [
        pl.BlockSpec((tm, k), lambda j, i: (i, 0)),
        pl.BlockSpec((None, k, tn), lambda j, i: (layer, 0, c0 + j)),
    ]
    args = [x, w]
    body = _proj_kernel
    if rope is not None:
        cos, sin = rope
        period = cos.shape[0] // tm
        in_specs += [pl.BlockSpec((tm, A_HEAD_DIM), lambda j, i: (i % period, 0))] * 2
        args += [cos, sin]
        body = _proj_rope_kernel
    return pl.pallas_call(
        body,
        grid=(ncols // tn, m // tm),
        in_specs=in_specs,
        out_specs=pl.BlockSpec((tm, tn), lambda j, i: (i, j)),
        out_shape=jax.ShapeDtypeStruct((m, ncols), out_dtype),
        scratch_shapes=[pltpu.VMEM((k, tn), BF16)],
        compiler_params=_cparams(2, 48),
        name=name,
    )(*args)


def _swiglu_in(x, w, layer, tm, tn):
    m, k = x.shape
    dff = w.shape[2] // 2
    nv = dff // tn
    return pl.pallas_call(
        _swiglu_in_kernel,
        grid=(dff // tn, m // tm),
        in_specs=[
            pl.BlockSpec((tm, k), lambda j, i: (i, 0)),
            pl.BlockSpec((None, k, tn), lambda j, i: (layer, 0, j)),
            pl.BlockSpec((None, k, tn), lambda j, i: (layer, 0, nv + j)),
        ],
        out_specs=pl.BlockSpec((tm, tn), lambda j, i: (i, j)),
        out_shape=jax.ShapeDtypeStruct((m, dff), BF16),
        scratch_shapes=[pltpu.VMEM((k, tn), BF16), pltpu.VMEM((k, tn), BF16)],
        compiler_params=_cparams(2, 56),
        name="swiglu_in",
    )(x, w, w)


def _cast_kernel(w_ref, o_ref):
    o_ref[...] = w_ref[...].astype(o_ref.dtype)


def _cast_bf16(w, layer, tk):
    _, k, n = w.shape
    return pl.pallas_call(
        _cast_kernel,
        grid=(k // tk,),
        in_specs=[pl.BlockSpec((None, tk, n), lambda i: (layer, i, 0))],
        out_specs=pl.BlockSpec((tk, n), lambda i: (i, 0)),
        out_shape=jax.ShapeDtypeStruct((k, n), BF16),
        compiler_params=_cparams(1, 32),
        name="cast_bf16",
    )(w)


def _out_ln_kernel(*refs, n_parts, res_w):
    a_refs = refs[:n_parts]
    w_ref, x_ref, gate_ref, lng_ref, lnb_ref, sc_ref, sh_ref, xo_ref, uo_ref = refs[n_parts:]
    acc = None
    off = 0
    for a_ref in a_refs:
        kk = a_ref.shape[1]
        part = jnp.dot(a_ref[...], w_ref[off:off + kk, :], preferred_element_type=F32)
        acc = part if acc is None else acc + part
        off += kk
    y = ALPHA * x_ref[...] + (res_w * (1.0 + gate_ref[...])) * acc
    mu = jnp.mean(y, axis=-1, keepdims=True)
    yc = y - mu
    var = jnp.mean(yc * yc, axis=-1, keepdims=True)
    xn = yc * lax.rsqrt(var + LN_EPS) * lng_ref[...] + lnb_ref[...]
    xo_ref[...] = xn
    uo_ref[...] = (xn * (1.0 + sc_ref[...]) + sh_ref[...]).astype(uo_ref.dtype)


def _out_ln(parts, wb, x, mods, gate_spec, sc_spec, sh_spec, ln_g, ln_b, ln_row, res_w, tm):
    m = parts[0].shape[0]
    k, d = wb.shape
    ln_spec = pl.BlockSpec((None, 1, d), lambda i: (ln_row, 0, 0))
    row = lambda width: pl.BlockSpec((tm, width), lambda i: (i, 0))
    return pl.pallas_call(
        functools.partial(_out_ln_kernel, n_parts=len(parts), res_w=res_w),
        grid=(m // tm,),
        in_specs=[row(p.shape[1]) for p in parts] + [
            pl.BlockSpec((k, d), lambda i: (0, 0), pipeline_mode=pl.Buffered(1)),
            row(d), gate_spec(tm), ln_spec, ln_spec, sc_spec(tm), sh_spec(tm),
        ],
        out_specs=[row(d), row(d)],
        out_shape=[jax.ShapeDtypeStruct((m, d), F32), jax.ShapeDtypeStruct((m, d), BF16)],
        compiler_params=_cparams(1, 56),
        name="out_postnorm",
    )(*parts, wb, x, mods, ln_g, ln_b, mods, mods)


def _moba_prompt_kernel(q_ref, k_ref, v_ref, o_ref, kb_ref, vb_ref, kmean_ref,
                        m_ref, acc_ref, *, nblk, scale):
    qi = pl.program_id(2)
    blk = MOBA_BLOCK
    hd = q_ref.shape[1]

    @pl.when(qi == 0)
    def _():
        kf = k_ref[...]
        seq = kf.shape[0]
        kb_ref[:, 0:hd] = kf.astype(BF16)
        r = lax.broadcasted_iota(jnp.int32, (seq, hd), 0)
        c = lax.broadcasted_iota(jnp.int32, (seq, hd), 1)
        in_blk = (r >= c * blk) & (r < c * blk + blk)
        kb_ref[:, hd:2 * hd] = jnp.where(in_blk, 1.0, 0.0).astype(BF16)
        vb_ref[:, 0:hd] = v_ref[...].astype(BF16)
        vb_ref[:, hd:2 * hd] = jnp.ones((seq, hd), BF16)
        kmean_ref[...] = jnp.zeros_like(kmean_ref)
        kmean_ref[0:nblk, :] = jnp.mean(kf.reshape(nblk, blk, hd), axis=1)

    q = q_ref[...]
    gate = lax.dot_general(q, kmean_ref[...], (((1,), (1,)), ((), ())),
                           precision=lax.Precision.HIGHEST, preferred_element_type=F32)
    lane = lax.broadcasted_iota(jnp.int32, gate.shape, 1)
    lanef = lane.astype(F32)
    g = jnp.where(lane < qi, gate, -jnp.inf)
    selm = jnp.zeros(gate.shape, F32)
    for _ in range(MOBA_TOPK):
        mx = jnp.max(g, axis=1, keepdims=True)
        cand = jnp.where(g == mx, lanef, 1e9)
        cand = jnp.where(mx > -jnp.inf, cand, 1e9)
        pick = lanef == jnp.min(cand, axis=1, keepdims=True)
        selm = jnp.where(pick, 1.0, selm)
        g = jnp.where(pick, -jnp.inf, g)
    qb = q.astype(BF16)
    q_aug = jnp.concatenate([qb, ((1.0 - selm) * NEG_BIG).astype(BF16)], axis=1)
    nt = (((1,), (1,)), ((), ()))
    start = pl.multiple_of(qi * blk, blk)
    s = lax.dot_general(qb, kb_ref[pl.ds(start, blk), 0:hd], nt, preferred_element_type=F32) * scale
    row = lax.broadcasted_iota(jnp.int32, s.shape, 0)
    col = lax.broadcasted_iota(jnp.int32, s.shape, 1)
    s = jnp.where(col <= row, s, NEG_BIG)
    m0 = jnp.max(s, axis=1, keepdims=True)
    p = jnp.exp(s - m0)
    m_ref[...] = jnp.broadcast_to(m0, m_ref.shape)
    acc_ref[...] = jnp.dot(p.astype(BF16), vb_ref[pl.ds(start, blk), :], preferred_element_type=F32)

    grp = MOBA_GROUP
    span = grp * blk

    def past_group(gi, carry):
        st = pl.multiple_of(gi * span, span)
        sn = lax.dot_general(q_aug, kb_ref[pl.ds(st, span), :], nt, preferred_element_type=F32) * scale
        m_prev = m_ref[...]
        m_new = jnp.maximum(m_prev, jnp.max(sn, axis=1, keepdims=True))
        a = jnp.exp(m_prev - m_new)
        pn = jnp.exp(sn - jnp.concatenate([m_new] * (span // hd), axis=1))
        acc_ref[...] = jnp.concatenate([a, a], axis=1) * acc_ref[...] + jnp.dot(
            pn.astype(BF16), vb_ref[pl.ds(st, span), :], preferred_element_type=F32)
        m_ref[...] = m_new
        return carry

    lax.fori_loop(0, (qi + grp - 1) // grp, past_group, 0)
    acc = acc_ref[...]
    o_ref[...] = (acc[:, 0:hd] / acc[:, hd:2 * hd]).astype(o_ref.dtype)


def _moba_prompt(q, k, v, bsz):
    rows, width = q.shape
    seq = rows // bsz
    nh = width // A_HEAD_DIM
    nblk = seq // MOBA_BLOCK
    blk = MOBA_BLOCK
    hd = A_HEAD_DIM
    return pl.pallas_call(
        functools.partial(_moba_prompt_kernel, nblk=nblk, scale=hd ** -0.5),
        grid=(bsz, nh, nblk),
        in_specs=[
            pl.BlockSpec((blk, hd), lambda b, h, i: (b * nblk + i, h)),
            pl.BlockSpec((seq, hd), lambda b, h, i: (b, h)),
            pl.BlockSpec((seq, hd), lambda b, h, i: (b, h)),
        ],
        out_specs=pl.BlockSpec((blk, hd), lambda b, h, i: (b * nblk + i, h)),
        out_shape=jax.ShapeDtypeStruct((rows, width), BF16),
        scratch_shapes=[
            pltpu.VMEM((seq, 2 * hd), BF16), pltpu.VMEM((seq, 2 * hd), BF16),
            pltpu.VMEM((LANES, hd), F32),
            pltpu.VMEM((blk, hd), F32), pltpu.VMEM((blk, 2 * hd), F32),
        ],
        compiler_params=_cparams(3, 32),
        name="moba_prompt",
    )(q, k, v)


def _mlstm_prompt_kernel(q_ref, k_ref, v_ref, og_ref, g_ref, gb_ref, ng_ref,
                         h_ref, c_ref, n_ref, m_ref, *, nb, nh, dk, dv):
    lc = q_ref.shape[1]

    @pl.when(pl.program_id(0) == 0)
    def _():
        c_ref[...] = jnp.zeros_like(c_ref)
        n_ref[...] = jnp.zeros_like(n_ref)
        m_ref[...] = jnp.zeros_like(m_ref)

    tt = lax.broadcasted_iota(jnp.int32, (lc, lc), 0)
    ss = lax.broadcasted_iota(jnp.int32, (lc, lc), 1)
    causal = ss <= tt
    eye = ss == tt
    nt = (((1,), (1,)), ((), ()))
    tn = (((0,), (0,)), ((), ()))
    gb = gb_ref[...]
    for b in range(nb):
        gates = g_ref[b] + gb
        for h in range(nh):
            bh = b * nh + h
            i_col = gates[:, h:h + 1]
            fpre = gates[:, nh + h:nh + h + 1]
            f_col = jnp.minimum(fpre, 0.0) - jnp.log1p(jnp.exp(-jnp.abs(fpre)))
            f_row = jnp.sum(jnp.where(eye, f_col, 0.0), axis=0, keepdims=True)
            i_row = jnp.sum(jnp.where(eye, i_col, 0.0), axis=0, keepdims=True)
            b_col = jnp.sum(jnp.where(causal, f_row, 0.0), axis=1, keepdims=True)
            b_row = jnp.sum(jnp.where(ss >= tt, f_col, 0.0), axis=0, keepdims=True)
            m_prev = m_ref[bh][:, 0:1]
            d = jnp.where(causal, b_col - b_row + i_row, NEG_BIG)
            inter = b_col + m_prev
            m_t = jnp.maximum(jnp.max(d, axis=1, keepdims=True), inter)
            w = jnp.exp(d - m_t)
            q = q_ref[b, :, h * dk:(h + 1) * dk]
            k = k_ref[b, :, h * dk:(h + 1) * dk] * (dk ** -0.5)
            vb = v_ref[b, :, h * dv:(h + 1) * dv].astype(BF16)
            qb = q.astype(BF16)
            s = lax.dot_general(qb, k.astype(BF16), nt, preferred_element_type=F32) * w
            wi = jnp.exp(inter - m_t)
            cst = c_ref[bh]
            nrow = n_ref[bh]
            num = wi * jnp.dot(qb, cst.astype(BF16), preferred_element_type=F32) + jnp.dot(
                s.astype(BF16), vb, preferred_element_type=F32)
            den = wi * jnp.sum(q * nrow, axis=1, keepdims=True) + jnp.sum(s, axis=1, keepdims=True)
            hh = num / jnp.maximum(jnp.abs(den), jnp.exp(-m_t))
            b_last = b_col[lc - 1:lc, :]
            g_col = b_last - b_col + i_col
            m_new = jnp.maximum(b_last + m_prev, jnp.max(g_col, axis=0, keepdims=True))
            wc = jnp.exp(b_last + m_prev - m_new)
            kw = k * jnp.exp(g_col - m_new)
            c_ref[bh] = wc * cst + lax.dot_general(kw.astype(BF16), vb, tn, preferred_element_type=F32)
            n_ref[bh] = wc * nrow + jnp.sum(kw, axis=0, keepdims=True)
            m_ref[bh] = jnp.broadcast_to(m_new, (1, LANES))
            hn = hh * lax.rsqrt(jnp.mean(hh * hh, axis=1, keepdims=True) + LN_EPS)
            hn = hn * ng_ref[:, h * dv:(h + 1) * dv] * _sigmoid(og_ref[b, :, h * dv:(h + 1) * dv])
            h_ref[b, :, h * dv:(h + 1) * dv] = hn.astype(h_ref.dtype)


def _mlstm_prompt(zm, gates, gate_bias, norm_g, bsz):
    rows, _ = zm.shape
    seq = rows // bsz
    nh = B_HEADS
    v_w = norm_g.shape[1]
    qk_w = v_w // 2
    dk, dv = qk_w // nh, v_w // nh
    lc = MLSTM_CHUNK
    z3 = zm.reshape(bsz, seq, zm.shape[1])
    g3 = gates.reshape(bsz, seq, LANES)
    nbh = bsz * nh
    full = lambda shape: pl.BlockSpec(shape, lambda c: (0,) * len(shape))
    return pl.pallas_call(
        functools.partial(_mlstm_prompt_kernel, nb=bsz, nh=nh, dk=dk, dv=dv),
        grid=(seq // lc,),
        in_specs=[
            pl.BlockSpec((bsz, lc, qk_w), lambda c: (0, c, 0)),
            pl.BlockSpec((bsz, lc, qk_w), lambda c: (0, c, 1)),
            pl.BlockSpec((bsz, lc, v_w), lambda c: (0, c, 1)),
            pl.BlockSpec((bsz, lc, v_w), lambda c: (0, c, 2)),
            pl.BlockSpec((bsz, lc, LANES), lambda c: (0, c, 0)),
            full((1, LANES)), full((1, v_w)),
        ],
        out_specs=[
            pl.BlockSpec((bsz, lc, v_w), lambda c: (0, c, 0)),
            full((nbh, dk, dv)), full((nbh, 1, dk)), full((nbh, 1, LANES)),
        ],
        out_shape=[
            jax.ShapeDtypeStruct((bsz, seq, v_w), BF16),
            jax.ShapeDtypeStruct((nbh, dk, dv), F32),
            jax.ShapeDtypeStruct((nbh, 1, dk), F32),
            jax.ShapeDtypeStruct((nbh, 1, LANES), F32),
        ],
        compiler_params=_cparams(1, 32),
        name="mlstm_prompt",
    )(z3, z3, z3, z3, g3, gate_bias, norm_g)


def _rglru_gates(xconv, wa_ref, ba, wx_ref, bx, lam):
    bd = wa_ref.shape[1]
    r_parts, i_parts = [], []
    for n in range(wa_ref.shape[0]):
        xb = xconv[:, n * bd:(n + 1) * bd].astype(BF16)
        r_parts.append(jnp.dot(xb, wa_ref[n].astype(BF16), preferred_element_type=F32))
        i_parts.append(jnp.dot(xb, wx_ref[n].astype(BF16), preferred_element_type=F32))
    r = _sigmoid(jnp.concatenate(r_parts, axis=1) + ba)
    ig = _sigmoid(jnp.concatenate(i_parts, axis=1) + bx)
    log_a = (-RG_C) * r * _softplus(-lam)
    a = jnp.exp(log_a)
    mult = jnp.sqrt(-jnp.tanh(log_a) * (a * a + 1.0))
    return a, mult * ig * xconv


def _rglru_prompt_kernel(gate_ref, xr_ref, cw_ref, cb_ref, wa_ref, ba_ref, wx_ref, bx_ref, lam_ref,
                         y_ref, hl_ref, buf_ref, xext_ref, a_ref, b_ref, hs_ref, hc_ref):
    t = pl.program_id(1)
    tt = xr_ref.shape[0]
    pad = 8
    nbuf = CONV_WIDTH - 1

    @pl.when(t == 0)
    def _():
        xext_ref[0:pad, :] = jnp.zeros((pad, xext_ref.shape[1]), F32)
        hc_ref[...] = jnp.zeros_like(hc_ref)

    @pl.when(t > 0)
    def _():
        xext_ref[0:pad, :] = xext_ref[tt:tt + pad, :]

    xr = xr_ref[...]
    xext_ref[pad:pad + tt, :] = xr
    xconv = cb_ref[...] + xext_ref[pad - nbuf:pad - nbuf + tt, :] * cw_ref[0:1, :]
    for j in range(1, CONV_WIDTH):
        xconv = xconv + xext_ref[pad - nbuf + j:pad - nbuf + j + tt, :] * cw_ref[j:j + 1, :]
    a, bterm = _rglru_gates(xconv, wa_ref, ba_ref[...], wx_ref, bx_ref[...], lam_ref[...])
    a_ref[...] = a
    b_ref[...] = bterm

    def step(i, h):
        h = a_ref[pl.ds(i, 1), :] * h + b_ref[pl.ds(i, 1), :]
        hs_ref[pl.ds(i, 1), :] = h
        return h

    h_last = lax.fori_loop(0, tt, step, hc_ref[...], unroll=8)
    hc_ref[...] = h_last
    y_ref[...] = (_gelu_tanh(gate_ref[...]) * hs_ref[...]).astype(y_ref.dtype)
    hl_ref[...] = h_last
    buf_ref[...] = xr[tt - nbuf:tt, :]


def _rglru_prompt(zr, layer, conv_w, conv_b, w_a, b_a, w_x, b_x, lam, bsz, tt):
    rows, w2 = zr.shape
    width = w2 // 2
    seq = rows // bsz
    nt = seq // tt
    nblk, bd = w_a.shape[1], w_a.shape[2]
    vec = lambda a: a.reshape(a.shape[0], 1, width)
    vspec = pl.BlockSpec((None, 1, width), lambda b, t: (layer, 0, 0))
    wspec = pl.BlockSpec((None, nblk, bd, bd), lambda b, t: (layer, 0, 0, 0))
    nbuf = CONV_WIDTH - 1
    return pl.pallas_call(
        _rglru_prompt_kernel,
        grid=(bsz, nt),
        in_specs=[
            pl.BlockSpec((tt, width), lambda b, t: (b * nt + t, 0)),
            pl.BlockSpec((tt, width), lambda b, t: (b * nt + t, 1)),
            pl.BlockSpec((None, CONV_WIDTH, width), lambda b, t: (layer, 0, 0)),
            vspec, wspec, vspec, wspec, vspec, vspec,
        ],
        out_specs=[
            pl.BlockSpec((tt, width), lambda b, t: (b * nt + t, 0)),
            pl.BlockSpec((None, 1, width), lambda b, t: (b, 0, 0)),
            pl.BlockSpec((None, nbuf, width), lambda b, t: (b, 0, 0)),
        ],
        out_shape=[
            jax.ShapeDtypeStruct((rows, width), BF16),
            jax.ShapeDtypeStruct((bsz, 1, width), F32),
            jax.ShapeDtypeStruct((bsz, nbuf, width), F32),
        ],
        scratch_shapes=[
            pltpu.VMEM((tt + 8, width), F32), pltpu.VMEM((tt, width), F32),
            pltpu.VMEM((tt, width), F32), pltpu.VMEM((tt, width), F32), pltpu.VMEM((1, width), F32),
        ],
        compiler_params=_cparams(2, 48),
        name="rglru_prompt",
    )(zr, zr, conv_w, vec(conv_b), w_a, vec(b_a), w_x, vec(b_x), vec(lam))


def _moba_gate_kernel(tbl_ref, q_ref, *refs, nheads, nsteps, pages_per_step, pages_per_blk):
    b = pl.program_id(0)
    st = pl.program_id(1)
    hd = A_HEAD_DIM
    k_refs = refs[:pages_per_step]
    sel_ref, ksum_ref = refs[pages_per_step:]
    page = k_refs[0].shape[0]
    blks_per_step = pages_per_step // pages_per_blk
    for i in range(blks_per_step):
        tot = jnp.sum(k_refs[i * pages_per_blk][...], axis=0)
        for u in range(1, pages_per_blk):
            tot = tot + jnp.sum(k_refs[i * pages_per_blk + u][...], axis=0)
        ksum_ref[st * blks_per_step + i] = tot

    @pl.when(st == nsteps - 1)
    def _():
        nblk = ksum_ref.shape[0]
        inv = 1.0 / (pages_per_blk * page)
        rowi = lax.broadcasted_iota(jnp.int32, (nblk, 1), 0).astype(F32)
        out_r = lax.broadcasted_iota(jnp.int32, sel_ref.shape, 0)
        out_c = lax.broadcasted_iota(jnp.int32, sel_ref.shape, 1)
        out = jnp.zeros(sel_ref.shape, jnp.int32)
        qrow = q_ref[pl.ds(b, 1), :]
        for h in range(nheads):
            km = ksum_ref[:, h, :] * inv
            g = jnp.sum(km * qrow[:, h * hd:(h + 1) * hd], axis=1, keepdims=True)
            for i in range(MOBA_TOPK):
                mx = jnp.max(g, axis=0, keepdims=True)
                idx = jnp.min(jnp.where(g == mx, rowi, 1e9), axis=0, keepdims=True)
                out = jnp.where((out_r == h) & (out_c == i), idx.astype(jnp.int32), out)
                g = jnp.where(rowi == idx, -jnp.inf, g)
        sel_ref[...] = out


GATE_PAGES_PER_STEP = 8


def _moba_sample_gate(q, cache_k, layer, tbl_flat, nreq, npages):
    page, nheads, hd = cache_k.shape[2:]
    ppb = MOBA_BLOCK // page
    nblk = npages // ppb
    pps = GATE_PAGES_PER_STEP
    nsteps = npages // pps

    def page_spec(u):
        return pl.BlockSpec((None, None, page, nheads, hd),
                            lambda b, s, tbl: (layer, tbl[b * npages + s * pps + u], 0, 0, 0))

    grid_spec = pltpu.PrefetchScalarGridSpec(
        num_scalar_prefetch=1,
        grid=(nreq, nsteps),
        in_specs=[pl.BlockSpec(q.shape, lambda b, s, tbl: (0, 0))] + [page_spec(u) for u in range(pps)],
        out_specs=pl.BlockSpec((None, nheads, LANES), lambda b, s, tbl: (b, 0, 0)),
        scratch_shapes=[pltpu.VMEM((nblk, nheads, hd), F32)],
    )
    return pl.pallas_call(
        functools.partial(_moba_gate_kernel, nheads=nheads, nsteps=nsteps, pages_per_step=pps,
                          pages_per_blk=ppb),
        grid_spec=grid_spec,
        out_shape=jax.ShapeDtypeStruct((nreq, nheads, LANES), jnp.int32),
        compiler_params=_cparams(2, 32),
        name="moba_sample_gate",
    )(tbl_flat, q, *([cache_k] * pps))


def _moba_sample_attn_kernel(tbl_ref, sel_ref, q_ref, kn_ref, vn_ref, *refs, npg, scale):
    k_refs = refs[:npg]
    v_refs = refs[npg:2 * npg]
    o_ref = refs[2 * npg]
    b = pl.program_id(0)
    h = pl.program_id(1)
    nheads = q_ref.shape[1]
    is_h = lax.broadcasted_iota(jnp.int32, (1, nheads, 1), 1) == h
    q_m = jnp.where(is_h, q_ref[b][None], 0.0)

    def score(kp):
        part = jnp.sum(kp * q_m, axis=2, keepdims=True)
        return jnp.sum(part, axis=1, keepdims=True) * scale

    s_own = score(kn_ref[b][None])
    ss = [score(k_ref[...]) for k_ref in k_refs]
    m = s_own
    for s in ss:
        m = jnp.maximum(m, jnp.max(s, axis=0, keepdims=True))
    l = jnp.exp(s_own - m)
    acc = l * vn_ref[b][None]
    for s, v_ref in zip(ss, v_refs):
        p = jnp.exp(s - m)
        l = l + jnp.sum(p, axis=0, keepdims=True)
        acc = acc + jnp.sum(p * v_ref[...], axis=0, keepdims=True)
    out = jnp.sum(jnp.where(is_h, acc / l, 0.0), axis=1)
    o_ref[...] = out.astype(o_ref.dtype)


def _moba_sample_attn(q, k_new, v_new, cache_k, cache_v, layer, tbl_flat, sel_flat, nreq, npages):
    page, nheads, hd = cache_k.shape[2:]
    ppb = MOBA_BLOCK // page
    npg = MOBA_TOPK * ppb

    def page_spec(j):
        def page_map(b, h, tbl, sel):
            blk = sel[(b * nheads + h) * LANES + j // ppb]
            return (layer, tbl[b * npages + blk * ppb + j % ppb], 0, 0, 0)
        return pl.BlockSpec((None, None, page, nheads, hd), page_map)

    head = pl.BlockSpec(q.shape, lambda b, h, tbl, sel: (0, 0, 0))
    pages = [page_spec(j) for j in range(npg)]
    grid_spec = pltpu.PrefetchScalarGridSpec(
        num_scalar_prefetch=2,
        grid=(nreq, nheads),
        in_specs=[head, head, head] + pages + pages,
        out_specs=pl.BlockSpec((None, 1, hd), lambda b, h, tbl, sel: (b, 0, h)),
    )
    return pl.pallas_call(
        functools.partial(_moba_sample_attn_kernel, npg=npg, scale=hd ** -0.5),
        grid_spec=grid_spec,
        out_shape=jax.ShapeDtypeStruct((nreq, 1, nheads * hd), F32),
        compiler_params=_cparams(2, 32),
        name="moba_sample_attn",
    )(tbl_flat, sel_flat, q, k_new, v_new, *([cache_k] * npg), *([cache_v] * npg))


def _mlstm_sample_kernel(q_ref, k_ref, v_ref, og_ref, g_ref, gb_ref, ng_ref, c0_ref, n0_ref, m0_ref,
                         h_ref, c_ref, n_ref, m_ref, *, nh, dk, dv):
    b = pl.program_id(0)
    rr = lax.broadcasted_iota(jnp.int32, (dk, dk), 0)
    cc = lax.broadcasted_iota(jnp.int32, (dk, dk), 1)
    eye = rr == cc
    gates = g_ref[pl.ds(b, 1), :] + gb_ref[...]
    q_all = q_ref[pl.ds(b, 1), :]
    k_all = k_ref[pl.ds(b, 1), :]
    v_all = v_ref[pl.ds(b, 1), :]
    og_all = og_ref[pl.ds(b, 1), :]
    for h in range(nh):
        ii = gates[:, h:h + 1]
        fpre = gates[:, nh + h:nh + h + 1]
        ff = jnp.minimum(fpre, 0.0) - jnp.log1p(jnp.exp(-jnp.abs(fpre)))
        m_prev = m0_ref[h][:, 0:1]
        q = q_all[:, h * dk:(h + 1) * dk]
        k = k_all[:, h * dk:(h + 1) * dk] * (dk ** -0.5)
        v = v_all[:, h * dv:(h + 1) * dv]
        inter = ff + m_prev
        m_t = jnp.maximum(ii, inter)
        s = jnp.sum(q * k, axis=1, keepdims=True) * jnp.exp(ii - m_t)
        wi = jnp.exp(inter - m_t)
        q_col = jnp.sum(jnp.where(eye, q, 0.0), axis=1, keepdims=True)
        k_col = jnp.sum(jnp.where(eye, k, 0.0), axis=1, keepdims=True)
        cst = c0_ref[h]
        nrow = n0_ref[h]
        num = wi * jnp.sum(q_col * cst, axis=0, keepdims=True) + s * v
        den = wi * jnp.sum(q * nrow, axis=1, keepdims=True) + s
        hh = num / jnp.maximum(jnp.abs(den), jnp.exp(-m_t))
        m_new = m_t
        wc = jnp.exp(inter - m_new)
        wg = jnp.exp(ii - m_new)
        c_ref[h] = wc * cst + (wg * k_col) * v
        n_ref[h] = wc * nrow + wg * k
        m_ref[h] = jnp.broadcast_to(m_new, (1, LANES))
        hn = hh * lax.rsqrt(jnp.mean(hh * hh, axis=1, keepdims=True) + LN_EPS)
        hn = hn * ng_ref[:, h * dv:(h + 1) * dv] * _sigmoid(og_all[:, h * dv:(h + 1) * dv])
        h_ref[:, h * dv:(h + 1) * dv] = hn.astype(h_ref.dtype)


def _mlstm_sample(zm, gates, gate_bias, norm_g, c0, n0, m0, layer, nreq):
    nh = B_HEADS
    v_w = norm_g.shape[1]
    qk_w = v_w // 2
    dk, dv = qk_w // nh, v_w // nh
    rows = zm.shape[0]
    full = lambda shape: pl.BlockSpec(shape, lambda b: (0,) * len(shape))
    return pl.pallas_call(
        functools.partial(_mlstm_sample_kernel, nh=nh, dk=dk, dv=dv),
        grid=(nreq,),
        in_specs=[
            pl.BlockSpec((rows, qk_w), lambda b: (0, 0)),
            pl.BlockSpec((rows, qk_w), lambda b: (0, 1)),
            pl.BlockSpec((rows, v_w), lambda b: (0, 1)),
            pl.BlockSpec((rows, v_w), lambda b: (0, 2)),
            full((rows, LANES)), full((1, LANES)), full((1, v_w)),
            pl.BlockSpec((None, None, nh, dk, dv), lambda b: (layer, b, 0, 0, 0)),
            pl.BlockSpec((None, None, nh, 1, dk), lambda b: (layer, b, 0, 0, 0)),
            pl.BlockSpec((None, None, nh, 1, LANES), lambda b: (layer, b, 0, 0, 0)),
        ],
        out_specs=[
            pl.BlockSpec((None, 1, v_w), lambda b: (b, 0, 0)),
            pl.BlockSpec((None, nh, dk, dv), lambda b: (b, 0, 0, 0)),
            pl.BlockSpec((None, nh, 1, dk), lambda b: (b, 0, 0, 0)),
            pl.BlockSpec((None, nh, 1, LANES), lambda b: (b, 0, 0, 0)),
        ],
        out_shape=[
            jax.ShapeDtypeStruct((nreq, 1, v_w), F32),
            jax.ShapeDtypeStruct((nreq, nh, dk, dv), F32),
            jax.ShapeDtypeStruct((nreq, nh, 1, dk), F32),
            jax.ShapeDtypeStruct((nreq, nh, 1, LANES), F32),
        ],
        compiler_params=_cparams(1, 32),
        name="mlstm_sample",
    )(zm, zm, zm, zm, gates, gate_bias, norm_g, c0, n0, m0)


def _rglru_sample_kernel(gate_ref, xr_ref, buf_ref, h0_ref, cw_ref, cb_ref, wa_ref, ba_ref, wx_ref,
                         bx_ref, lam_ref, y_ref, h_ref, nbuf_ref):
    nbuf = CONV_WIDTH - 1
    xr = xr_ref[...]
    xconv = cb_ref[...] + buf_ref[0] * cw_ref[0:1, :]
    for j in range(1, nbuf):
        xconv = xconv + buf_ref[j] * cw_ref[j:j + 1, :]
    xconv = xconv + xr * cw_ref[nbuf:nbuf + 1, :]
    a, bterm = _rglru_gates(xconv, wa_ref, ba_ref[...], wx_ref, bx_ref[...], lam_ref[...])
    h = a * h0_ref[...] + bterm
    h_ref[...] = h
    y = _gelu_tanh(gate_ref[...]) * h
    pad_rows = y_ref.shape[0] - y.shape[0]
    y_ref[...] = jnp.concatenate([y, jnp.zeros((pad_rows, y.shape[1]), F32)], axis=0).astype(y_ref.dtype)
    for j in range(nbuf - 1):
        nbuf_ref[j] = buf_ref[j + 1]
    nbuf_ref[nbuf - 1] = xr


def _rglru_sample(zr, layer, buf_t, h0, conv_w, conv_b, w_a, b_a, w_x, b_x, lam, nreq):
    rows, w2 = zr.shape
    width = w2 // 2
    nblk, bd = w_a.shape[1], w_a.shape[2]
    nbuf = CONV_WIDTH - 1
    vec = lambda a: a.reshape(a.shape[0], 1, width)
    vspec = pl.BlockSpec((None, 1, width), lambda i: (layer, 0, 0))
    wspec = pl.BlockSpec((None, nblk, bd, bd), lambda i: (layer, 0, 0, 0))
    return pl.pallas_call(
        _rglru_sample_kernel,
        grid=(1,),
        in_specs=[
            pl.BlockSpec((nreq, width), lambda i: (0, 0)),
            pl.BlockSpec((nreq, width), lambda i: (0, 1)),
            pl.BlockSpec((None, nbuf, nreq, width), lambda i: (layer, 0, 0, 0)),
            pl.BlockSpec((None, nreq, width), lambda i: (layer, 0, 0)),
            pl.BlockSpec((None, CONV_WIDTH, width), lambda i: (layer, 0, 0)),
            vspec, wspec, vspec, wspec, vspec, vspec,
        ],
        out_specs=[
            pl.BlockSpec((rows, width), lambda i: (0, 0)),
            pl.BlockSpec((nreq, width), lambda i: (0, 0)),
            pl.BlockSpec((nbuf, nreq, width), lambda i: (0, 0, 0)),
        ],
        out_shape=[
            jax.ShapeDtypeStruct((rows, width), BF16),
            jax.ShapeDtypeStruct((nreq, width), F32),
            jax.ShapeDtypeStruct((nbuf, nreq, width), F32),
        ],
        compiler_params=_cparams(1, 32),
        name="rglru_sample",
    )(zr, zr, buf_t, h0, conv_w, vec(conv_b), w_a, vec(b_a), w_x, vec(b_x), vec(lam))


def _rope_tables(pos):
    half = A_HEAD_DIM // 2
    inv = ROPE_THETA ** (-jnp.arange(half, dtype=F32) / half)
    ang = pos.astype(F32)[:, None] * inv[None, :]
    cos, sin = jnp.cos(ang), jnp.sin(ang)
    return jnp.concatenate([cos, cos], axis=1), jnp.concatenate([-sin, sin], axis=1)


def kernel(x_prompt, x_sample, cache_k, cache_v, state_mlstm_c, state_mlstm_n, state_mlstm_m,
           state_rglru_h, state_conv, page_table, c_prompt, c_sample, w_ada, b_ada, ln_g, ln_b,
           w_ffn1_in, w_ffn1_out, w_ffn2_in, w_ffn2_out, w_in_ab, b_igate, b_fgate, mlstm_norm_g,
           w_out_ab, w_in_rg, conv_w, conv_b, w_rg_a, b_rg_a, w_rg_x, b_rg_x, lru_lambda, w_out_rg):
    bsz, seq, d = x_prompt.shape
    nreq = x_sample.shape[0]
    depth = w_ada.shape[0]
    n_even = w_in_ab.shape[0]
    a_w = cache_k.shape[3] * cache_k.shape[4]
    v_w = mlstm_norm_g.shape[1]
    qk_w = v_w // 2
    m_w = 2 * qk_w + 2 * v_w
    nh_b = B_HEADS
    past_len = page_table.shape[1] * cache_k.shape[2]
    npages = page_table.shape[1]
    rows_p = bsz * seq
    rs = SAMPLE_ROWS

    tm = 1024
    tn = 512
    tm_out = 256
    tk_cast = 512
    tt_rg = 256

    c_all = jnp.concatenate([c_prompt, c_sample, jnp.zeros((rs - bsz - nreq, d), F32)], axis=0)
    mods = _mods(c_all, w_ada, b_ada).reshape(depth, rs, 9, d)
    mods_p = jnp.transpose(mods[:, :bsz], (0, 2, 1, 3)).reshape(depth * 9 * bsz, 1, d)
    mods_s = jnp.transpose(mods[:, bsz:bsz + nreq], (0, 2, 1, 3))
    mods_s = jnp.pad(mods_s, ((0, 0), (0, 0), (0, rs - nreq), (0, 0))).reshape(depth * 9, rs, d)

    def mod_spec_p(l, s, j):
        base = ((l * 3 + s) * 3 + j) * bsz

        def make(tile_rows):
            tiles_per_batch = seq // tile_rows
            return pl.BlockSpec((None, 1, d), lambda i: (base + i // tiles_per_batch, 0, 0))
        return make

    def mod_spec_s(l, s, j):
        row = (l * 3 + s) * 3 + j

        def make(tile_rows):
            return pl.BlockSpec((None, rs, d), lambda i: (row, 0, 0))
        return make

    ln_g3 = ln_g.reshape(depth * 3, 1, d)
    ln_b3 = ln_b.reshape(depth * 3, 1, d)

    cos_p, sin_p = _rope_tables(jnp.arange(seq, dtype=jnp.int32))
    cos_s, sin_s = _rope_tables(jnp.full((rs,), past_len, dtype=jnp.int32))

    tbl_flat = page_table.reshape(-1).astype(jnp.int32)
    n0_s = state_mlstm_n[:, :, :, None, :]
    m0_s = jnp.broadcast_to(state_mlstm_m[:, :, :, None, None], state_mlstm_m.shape + (1, LANES))
    conv_t = jnp.transpose(state_conv, (0, 2, 1, 3))
    gate_bias = jnp.pad(jnp.concatenate([b_igate, b_fgate], axis=1), ((0, 0), (0, LANES - 2 * nh_b)))
    w_gates = jnp.pad(w_in_ab[:, :, 3 * a_w + m_w:], ((0, 0), (0, 0), (0, LANES - 2 * nh_b)))

    xp = x_prompt.reshape(rows_p, d)
    xs = jnp.pad(x_sample.reshape(nreq, d), ((0, rs - nreq), (0, 0)))
    up = _modulate(xp, mod_spec_p(0, 0, 1), mod_spec_p(0, 0, 0), mods_p, tm)
    us = _modulate(xs, mod_spec_s(0, 0, 1), mod_spec_s(0, 0, 0), mods_s, rs)

    groups = [
        dict(x=xp, u=up, tm=tm, tm_out=tm_out, mods=mods_p, spec=mod_spec_p, sample=False),
        dict(x=xs, u=us, tm=rs, tm_out=rs, mods=mods_s, spec=mod_spec_s, sample=True),
    ]
    outs = [dict(k=[], v=[], c=[], n=[], m=[], h=[], buf=[]) for _ in groups]

    for l in range(depth):
        wb_ffn1 = _cast_bf16(w_ffn1_out, l, tk_cast)
        wb_ffn2 = _cast_bf16(w_ffn2_out, l, tk_cast)
        wb_mix = _cast_bf16(w_out_ab if l % 2 == 0 else w_out_rg, l // 2, tk_cast)
        for gi, g in enumerate(groups):
            o = outs[gi]
            gtm = g["tm"]
            spec = g["spec"]

            def post(parts, wb, s, res_w, nxt):
                nl, ns = nxt
                return _out_ln(parts, wb, g["x"], g["mods"], spec(l, s, 2), spec(nl, ns, 1),
                               spec(nl, ns, 0), ln_g3, ln_b3, l * 3 + s, res_w, g["tm_out"])

            act = _swiglu_in(g["u"], w_ffn1_in, l, gtm, tn)
            g["x"], g["u"] = post([act], wb_ffn1, 0, FFN_RES, (l, 1))

            if l % 2 == 0:
                e = l // 2
                rope = (cos_s, sin_s) if g["sample"] else (cos_p, sin_p)
                q = _proj(g["u"], w_in_ab, e, 0, a_w, gtm, tn, F32, rope=rope, name="proj_q")
                k = _proj(g["u"], w_in_ab, e, a_w, a_w, gtm, tn, F32, rope=rope, name="proj_k")
                v = _proj(g["u"], w_in_ab, e, 2 * a_w, a_w, gtm, tn, F32, name="proj_v")
                zm = _proj(g["u"], w_in_ab, e, 3 * a_w, m_w, gtm, tn, F32, name="proj_mlstm")
                gates = _proj(g["u"], w_gates, e, 0, LANES, gtm, LANES, F32, name="proj_gates")
                gb = gate_bias[e:e + 1]
                ng = mlstm_norm_g[e:e + 1]
                if g["sample"]:
                    sel = _moba_sample_gate(q, cache_k, e, tbl_flat, nreq, npages)
                    heads3 = lambda t: t.reshape(rs, -1, A_HEAD_DIM)
                    oa = _moba_sample_attn(heads3(q), heads3(k), heads3(v), cache_k, cache_v, e,
                                           tbl_flat, sel.reshape(-1), nreq, npages)
                    hm, c1, n1, m1 = _mlstm_sample(zm, gates, gb, ng, state_mlstm_c, n0_s, m0_s, e, nreq)
                    pad16 = lambda t, w: jnp.pad(t.reshape(nreq, w), ((0, rs - nreq), (0, 0))).astype(BF16)
                    mix = [pad16(oa, a_w), pad16(hm, v_w)]
                    o["k"].append(k[:nreq].reshape(nreq, 1, -1, A_HEAD_DIM))
                    o["v"].append(v[:nreq].reshape(nreq, 1, -1, A_HEAD_DIM))
                    o["c"].append(c1)
                    o["n"].append(n1.reshape(nreq, nh_b, -1))
                    o["m"].append(m1[:, :, 0, 0])
                else:
                    oa = _moba_prompt(q, k, v, bsz)
                    hm, c1, n1, m1 = _mlstm_prompt(zm, gates, gb, ng, bsz)
                    mix = [oa, hm.reshape(rows_p, v_w)]
                    o["k"].append(k.reshape(bsz, seq, -1, A_HEAD_DIM))
                    o["v"].append(v.reshape(bsz, seq, -1, A_HEAD_DIM))
                    o["c"].append(c1.reshape(bsz, nh_b, c1.shape[1], c1.shape[2]))
                    o["n"].append(n1.reshape(bsz, nh_b, -1))
                    o["m"].append(m1[:, 0, 0].reshape(bsz, nh_b))
            else:
                od = l // 2
                zr = _proj(g["u"], w_in_rg, od, 0, w_in_rg.shape[2], gtm, tn, F32, name="proj_rg")
                if g["sample"]:
                    mix, h1, nb = _rglru_sample(zr, od, conv_t, state_rglru_h, conv_w, conv_b, w_rg_a,
                                                b_rg_a, w_rg_x, b_rg_x, lru_lambda, nreq)
                    o["h"].append(h1)
                    o["buf"].append(jnp.transpose(nb, (1, 0, 2)))
                else:
                    mix, h1, nb = _rglru_prompt(zr, od, conv_w, conv_b, w_rg_a, b_rg_a, w_rg_x, b_rg_x,
                                                lru_lambda, bsz, tt_rg)
                    o["h"].append(h1.reshape(bsz, -1))
                    o["buf"].append(nb)
                mix = [mix]
            g["x"], g["u"] = post(mix, wb_mix, 1, 1.0, (l, 2))

            act = _swiglu_in(g["u"], w_ffn2_in, l, gtm, tn)
            g["x"], g["u"] = post([act], wb_ffn2, 2, FFN_RES, (min(l + 1, depth - 1), 0))

    op, os_ = outs
    st = jnp.stack
    return (groups[0]["x"].reshape(bsz, seq, d), groups[1]["x"][:nreq].reshape(nreq, 1, d),
            st(op["k"]), st(op["v"]), st(os_["k"]), st(os_["v"]),
            st(op["c"]), st(op["n"]), st(op["m"]), st(os_["c"]), st(os_["n"]), st(os_["m"]),
            st(op["h"]), st(op["buf"]), st(os_["h"]), st(os_["buf"]))
```

```python
import functools

import numpy as np
import jax
import jax.numpy as jnp
from jax import lax
from jax.experimental import pallas as pl
from jax.experimental.pallas import tpu as pltpu

F32 = jnp.float32
BF16 = jnp.bfloat16

DEPTH = 4
MOBA_BLOCK = 256
MOBA_TOPK = 3
MOBA_GROUP = 4
ROPE_THETA = 10000.0
A_HEAD_DIM = 128
B_HEADS = 4
MLSTM_CHUNK = 64
RG_BLOCKS = 16
CONV_WIDTH = 4
RG_C = 8.0
FFN_RES = 0.5
ALPHA = (2.0 * DEPTH) ** 0.25
LN_EPS = 1e-5

LANES = 128
SAMPLE_ROWS = 16
NEG_BIG = -1e30
MIB = 1024 * 1024


def _cparams(n_axes, vmem_mib):
    return pltpu.CompilerParams(
        dimension_semantics=("arbitrary",) * n_axes,
        vmem_limit_bytes=int(vmem_mib * MIB),
    )


def _sigmoid(x):
    return 0.5 * jnp.tanh(0.5 * x) + 0.5


def _softplus(x):
    return jnp.maximum(x, 0.0) + jnp.log1p(jnp.exp(-jnp.abs(x)))


def _gelu_tanh(x):
    return 0.5 * x * (1.0 + jnp.tanh(0.7978845608028654 * (x + 0.044715 * x * x * x)))


def _mods_kernel(c_ref, w_ref, b_ref, o_ref):
    c = c_ref[...]
    s = (c * _sigmoid(c)).astype(BF16)
    o_ref[...] = jnp.dot(s, w_ref[...].astype(BF16), preferred_element_type=F32) + b_ref[...]


def _mods(c_all, w_ada, b_ada):
    depth, k, n = w_ada.shape
    rows = c_all.shape[0]
    tn = 1024 if n % 1024 == 0 else n
    return pl.pallas_call(
        _mods_kernel,
        grid=(depth, n // tn),
        in_specs=[
            pl.BlockSpec((rows, k), lambda l, j: (0, 0)),
            pl.BlockSpec((None, k, tn), lambda l, j: (l, 0, j)),
            pl.BlockSpec((None, 1, tn), lambda l, j: (l, 0, j)),
        ],
        out_specs=pl.BlockSpec((None, rows, tn), lambda l, j: (l, 0, j)),
        out_shape=jax.ShapeDtypeStruct((depth, rows, n), F32),
        compiler_params=_cparams(2, 48),
        name="adaln_mods",
    )(c_all, w_ada, b_ada.reshape(depth, 1, n))


def _modulate_kernel(x_ref, sc_ref, sh_ref, u_ref):
    u_ref[...] = (x_ref[...] * (1.0 + sc_ref[...]) + sh_ref[...]).astype(u_ref.dtype)


def _modulate(x, sc_spec, sh_spec, mods, tm):
    m, d = x.shape
    return pl.pallas_call(
        _modulate_kernel,
        grid=(m // tm,),
        in_specs=[pl.BlockSpec((tm, d), lambda i: (i, 0)), sc_spec(tm), sh_spec(tm)],
        out_specs=pl.BlockSpec((tm, d), lambda i: (i, 0)),
        out_shape=jax.ShapeDtypeStruct((m, d), BF16),
        compiler_params=_cparams(1, 32),
        name="modulate0",
    )(x, mods, mods)


def _rope_store(z, cos, sin, o_ref):
    for j in range(z.shape[1] // A_HEAD_DIM):
        zj = z[:, j * A_HEAD_DIM:(j + 1) * A_HEAD_DIM]
        o_ref[:, j * A_HEAD_DIM:(j + 1) * A_HEAD_DIM] = (
            zj * cos + pltpu.roll(zj, A_HEAD_DIM // 2, 1) * sin).astype(o_ref.dtype)


def _proj_kernel(*refs, rope, n_alias, w_mode):
    refs = refs[:len(refs) - 3 - n_alias] + refs[len(refs) - 3:]
    if rope:
        x_ref, xs_ref, w_ref, cos_ref, sin_ref, coss_ref, sins_ref, o_ref, os_ref, wb_ref = refs
    else:
        x_ref, xs_ref, w_ref, o_ref, os_ref, wb_ref = refs

    def mm(x):
        if w_mode == "nk_contract_last":
            return lax.dot_general(x, wb_ref[...], (((1,), (1,)), ((), ())), preferred_element_type=F32)
        return jnp.dot(x, wb_ref[...], preferred_element_type=F32)

    @pl.when(pl.program_id(1) == 0)
    def _():
        w = w_ref[...]
        if w_mode == "nk_transpose":
            w = w.T
        wb_ref[...] = w.astype(BF16)
        zs = mm(xs_ref[...])
        if rope:
            _rope_store(zs, coss_ref[...], sins_ref[...], os_ref)
        else:
            os_ref[...] = zs.astype(os_ref.dtype)

    z = mm(x_ref[...])
    if rope:
        _rope_store(z, cos_ref[...], sin_ref[...], o_ref)
    else:
        o_ref[...] = z.astype(o_ref.dtype)


def _swiglu_in_kernel(x_ref, xs_ref, wg_ref, wv_ref, o_ref, os_ref, wgb_ref, wvb_ref):
    def act(x):
        g = jnp.dot(x, wgb_ref[...], preferred_element_type=F32)
        v = jnp.dot(x, wvb_ref[...], preferred_element_type=F32)
        return g * _sigmoid(g) * v

    @pl.when(pl.program_id(1) == 0)
    def _():
        wgb_ref[...] = wg_ref[...].astype(BF16)
        wvb_ref[...] = wv_ref[...].astype(BF16)
        os_ref[...] = act(xs_ref[...]).astype(os_ref.dtype)

    o_ref[...] = act(x_ref[...]).astype(o_ref.dtype)


def _proj(x, xs, w, layer, col0, ncols, tm, tn, out_dtype, rope=None, stack=None, w_t=False,
          name="proj"):
    m, k = x.shape
    rows_s = xs.shape[0]
    c0 = col0 // tn
    if w_t:
        w_spec = pl.BlockSpec((None, tn, k), lambda j, i: (layer, c0 + j, 0))
        w_mode = "nk_transpose" if tn % LANES == 0 else "nk_contract_last"
    else:
        w_spec = pl.BlockSpec((None, k, tn), lambda j, i: (layer, 0, c0 + j))
        w_mode = "kn"
    wb_shape = (tn, k) if w_mode == "nk_contract_last" else (k, tn)
    in_specs = [
        pl.BlockSpec((tm, k), lambda j, i: (i, 0)),
        pl.BlockSpec((rows_s, k), lambda j, i: (0, 0)),
        w_spec,
    ]
    args = [x, xs, w]
    if rope is not None:
        period = rope[0].shape[0] // tm
        in_specs += [pl.BlockSpec((tm, A_HEAD_DIM), lambda j, i: (i % period, 0))] * 2
        in_specs += [pl.BlockSpec((rows_s, A_HEAD_DIM), lambda j, i: (0, 0))] * 2
        args += list(rope)
    aliases = {}
    if stack is None:
        out_specs = [pl.BlockSpec((tm, tn), lambda j, i: (i, j)),
                     pl.BlockSpec((rows_s, tn), lambda j, i: (0, j))]
        out_shape = [jax.ShapeDtypeStruct((m, ncols), out_dtype),
                     jax.ShapeDtypeStruct((rows_s, ncols), out_dtype)]
    else:
        n_stack, idx, prev = stack
        out_specs = [pl.BlockSpec((None, tm, tn), lambda j, i: (idx, i, j)),
                     pl.BlockSpec((None, rows_s, tn), lambda j, i: (idx, 0, j))]
        out_shape = [jax.ShapeDtypeStruct((n_stack, m, ncols), out_dtype),
                     jax.ShapeDtypeStruct((n_stack, rows_s, ncols), out_dtype)]
        if prev is not None:
            aliases = {len(args): 0, len(args) + 1: 1}
            in_specs += [pl.BlockSpec(memory_space=pl.ANY)] * 2
            args += list(prev)
    return pl.pallas_call(
        functools.partial(_proj_kernel, rope=rope is not None, n_alias=len(aliases), w_mode=w_mode),
        grid=(ncols // tn, m // tm),
        in_specs=in_specs,
        out_specs=out_specs,
        out_shape=out_shape,
        input_output_aliases=aliases,
        scratch_shapes=[pltpu.VMEM(wb_shape, BF16)],
        compiler_params=_cparams(2, 48),
        name=name,
    )(*args)


def _swiglu_in(x, xs, w, layer, tm, tn):
    m, k = x.shape
    rows_s = xs.shape[0]
    dff = w.shape[2] // 2
    nv = dff // tn
    return pl.pallas_call(
        _swiglu_in_kernel,
        grid=(dff // tn, m // tm),
        in_specs=[
            pl.BlockSpec((tm, k), lambda j, i: (i, 0)),
            pl.BlockSpec((rows_s, k), lambda j, i: (0, 0)),
            pl.BlockSpec((None, k, tn), lambda j, i: (layer, 0, j)),
            pl.BlockSpec((None, k, tn), lambda j, i: (layer, 0, nv + j)),
        ],
        out_specs=[pl.BlockSpec((tm, tn), lambda j, i: (i, j)),
                   pl.BlockSpec((rows_s, tn), lambda j, i: (0, j))],
        out_shape=[jax.ShapeDtypeStruct((m, dff), BF16), jax.ShapeDtypeStruct((rows_s, dff), BF16)],
        scratch_shapes=[pltpu.VMEM((k, tn), BF16), pltpu.VMEM((k, tn), BF16)],
        compiler_params=_cparams(2, 56),
        name="swiglu_in",
    )(x, xs, w, w)


def _cast_kernel(w_ref, o_ref):
    o_ref[...] = w_ref[...].astype(o_ref.dtype)


def _cast_bf16(w, layer, tk):
    _, k, n = w.shape
    return pl.pallas_call(
        _cast_kernel,
        grid=(k // tk,),
        in_specs=[pl.BlockSpec((None, tk, n), lambda i: (layer, i, 0))],
        out_specs=pl.BlockSpec((tk, n), lambda i: (i, 0)),
        out_shape=jax.ShapeDtypeStruct((k, n), BF16),
        compiler_params=_cparams(1, 32),
        name="cast_bf16",
    )(w)


def _out_ln_kernel(*refs, n_parts, res_w):
    a_refs = refs[:n_parts]
    w_ref, x_ref, gate_ref, lng_ref, lnb_ref, sc_ref, sh_ref, xo_ref, uo_ref = refs[n_parts:]
    acc = None
    off = 0
    for a_ref in a_refs:
        kk = a_ref.shape[1]
        part = jnp.dot(a_ref[...], w_ref[off:off + kk, :], preferred_element_type=F32)
        acc = part if acc is None else acc + part
        off += kk
    y = ALPHA * x_ref[...] + (res_w * (1.0 + gate_ref[...])) * acc
    mu = jnp.mean(y, axis=-1, keepdims=True)
    yc = y - mu
    var = jnp.mean(yc * yc, axis=-1, keepdims=True)
    xn = yc * lax.rsqrt(var + LN_EPS) * lng_ref[...] + lnb_ref[...]
    xo_ref[...] = xn
    uo_ref[...] = (xn * (1.0 + sc_ref[...]) + sh_ref[...]).astype(uo_ref.dtype)


def _out_ln(parts, wb, x, mods, gate_spec, sc_spec, sh_spec, ln_g, ln_b, ln_row, res_w, tm):
    m = parts[0].shape[0]
    k, d = wb.shape
    ln_spec = pl.BlockSpec((None, 1, d), lambda i: (ln_row, 0, 0))
    row = lambda width: pl.BlockSpec((tm, width), lambda i: (i, 0))
    return pl.pallas_call(
        functools.partial(_out_ln_kernel, n_parts=len(parts), res_w=res_w),
        grid=(m // tm,),
        in_specs=[row(p.shape[1]) for p in parts] + [
            pl.BlockSpec((k, d), lambda i: (0, 0), pipeline_mode=pl.Buffered(1)),
            row(d), gate_spec(tm), ln_spec, ln_spec, sc_spec(tm), sh_spec(tm),
        ],
        out_specs=[row(d), row(d)],
        out_shape=[jax.ShapeDtypeStruct((m, d), F32), jax.ShapeDtypeStruct((m, d), BF16)],
        compiler_params=_cparams(1, 56),
        name="out_postnorm",
    )(*parts, wb, x, mods, ln_g, ln_b, mods, mods)


def _moba_prompt_kernel(q_ref, k_ref, v_ref, o_ref, kb_ref, vb_ref, kmean_ref,
                        m_ref, acc_ref, *, nblk, scale):
    qi = pl.program_id(2)
    blk = MOBA_BLOCK
    hd = q_ref.shape[1]

    @pl.when(qi == 0)
    def _():
        kf = k_ref[...]
        seq = kf.shape[0]
        kb_ref[:, 0:hd] = kf.astype(BF16)
        r = lax.broadcasted_iota(jnp.int32, (seq, hd), 0)
        c = lax.broadcasted_iota(jnp.int32, (seq, hd), 1)
        in_blk = (r >= c * blk) & (r < c * blk + blk)
        kb_ref[:, hd:2 * hd] = jnp.where(in_blk, 1.0, 0.0).astype(BF16)
        vb_ref[:, 0:hd] = v_ref[...].astype(BF16)
        vb_ref[:, hd:2 * hd] = jnp.ones((seq, hd), BF16)
        kmean_ref[...] = jnp.zeros_like(kmean_ref)
        kmean_ref[0:nblk, :] = jnp.mean(kf.reshape(nblk, blk, hd), axis=1)

    q = q_ref[...]
    gate = lax.dot_general(q, kmean_ref[...], (((1,), (1,)), ((), ())),
                           precision=lax.Precision.HIGHEST, preferred_element_type=F32)
    lane = lax.broadcasted_iota(jnp.int32, gate.shape, 1)
    lanef = lane.astype(F32)
    g = jnp.where(lane < qi, gate, -jnp.inf)
    selm = jnp.zeros(gate.shape, F32)
    for _ in range(MOBA_TOPK):
        mx = jnp.max(g, axis=1, keepdims=True)
        cand = jnp.where(g == mx, lanef, 1e9)
        cand = jnp.where(mx > -jnp.inf, cand, 1e9)
        pick = lanef == jnp.min(cand, axis=1, keepdims=True)
        selm = jnp.where(pick, 1.0, selm)
        g = jnp.where(pick, -jnp.inf, g)
    qb = q.astype(BF16)
    q_aug = jnp.concatenate([qb, ((1.0 - selm) * NEG_BIG).astype(BF16)], axis=1)
    nt = (((1,), (1,)), ((), ()))
    start = pl.multiple_of(qi * blk, blk)
    s = lax.dot_general(qb, kb_ref[pl.ds(start, blk), 0:hd], nt, preferred_element_type=F32) * scale
    row = lax.broadcasted_iota(jnp.int32, s.shape, 0)
    col = lax.broadcasted_iota(jnp.int32, s.shape, 1)
    s = jnp.where(col <= row, s, NEG_BIG)
    m0 = jnp.max(s, axis=1, keepdims=True)
    p = jnp.exp(s - m0)
    m_ref[...] = jnp.broadcast_to(m0, m_ref.shape)
    acc_ref[...] = jnp.dot(p.astype(BF16), vb_ref[pl.ds(start, blk), :], preferred_element_type=F32)

    grp = MOBA_GROUP
    span = grp * blk

    def past_group(gi, carry):
        st = pl.multiple_of(gi * span, span)
        sn = lax.dot_general(q_aug, kb_ref[pl.ds(st, span), :], nt, preferred_element_type=F32) * scale
        m_prev = m_ref[...]
        m_new = jnp.maximum(m_prev, jnp.max(sn, axis=1, keepdims=True))
        a = jnp.exp(m_prev - m_new)
        pn = jnp.exp(sn - jnp.concatenate([m_new] * (span // hd), axis=1))
        acc_ref[...] = jnp.concatenate([a, a], axis=1) * acc_ref[...] + jnp.dot(
            pn.astype(BF16), vb_ref[pl.ds(st, span), :], preferred_element_type=F32)
        m_ref[...] = m_new
        return carry

    lax.fori_loop(0, (qi + grp - 1) // grp, past_group, 0)
    acc = acc_ref[...]
    o_ref[...] = (acc[:, 0:hd] / acc[:, hd:2 * hd]).astype(o_ref.dtype)


def _moba_prompt(q, k, v, slab, bsz):
    rows, width = q.shape
    seq = rows // bsz
    nh = width // A_HEAD_DIM
    nblk = seq // MOBA_BLOCK
    assert nblk % MOBA_GROUP == 0
    blk = MOBA_BLOCK
    hd = A_HEAD_DIM
    return pl.pallas_call(
        functools.partial(_moba_prompt_kernel, nblk=nblk, scale=hd ** -0.5),
        grid=(bsz, nh, nblk),
        in_specs=[
            pl.BlockSpec((blk, hd), lambda b, h, i: (b * nblk + i, h)),
            pl.BlockSpec((None, seq, hd), lambda b, h, i: (slab, b, h)),
            pl.BlockSpec((None, seq, hd), lambda b, h, i: (slab, b, h)),
        ],
        out_specs=pl.BlockSpec((blk, hd), lambda b, h, i: (b * nblk + i, h)),
        out_shape=jax.ShapeDtypeStruct((rows, width), BF16),
        scratch_shapes=[
            pltpu.VMEM((seq, 2 * hd), BF16), pltpu.VMEM((seq, 2 * hd), BF16),
            pltpu.VMEM((LANES, hd), F32),
            pltpu.VMEM((blk, hd), F32), pltpu.VMEM((blk, 2 * hd), F32),
        ],
        compiler_params=_cparams(3, 32),
        name="moba_prompt",
    )(q, k, v)


def _mlstm_prompt_kernel(q_ref, k_ref, v_ref, og_ref, g_ref, gb_ref, ng_ref,
                         h_ref, c_ref, n_ref, m_ref, *, nb, nh, dk, dv):
    lc = q_ref.shape[1]

    @pl.when(pl.program_id(0) == 0)
    def _():
        c_ref[...] = jnp.zeros_like(c_ref)
        n_ref[...] = jnp.zeros_like(n_ref)
        m_ref[...] = jnp.zeros_like(m_ref)

    tt = lax.broadcasted_iota(jnp.int32, (lc, lc), 0)
    ss = lax.broadcasted_iota(jnp.int32, (lc, lc), 1)
    causal = ss <= tt
    eye = ss == tt
    nt = (((1,), (1,)), ((), ()))
    tn = (((0,), (0,)), ((), ()))
    gb = gb_ref[...]
    for b in range(nb):
        gates = g_ref[b] + gb
        for h in range(nh):
            bh = b * nh + h
            i_col = gates[:, h:h + 1]
            fpre = gates[:, nh + h:nh + h + 1]
            f_col = jnp.minimum(fpre, 0.0) - jnp.log1p(jnp.exp(-jnp.abs(fpre)))
            f_row = jnp.sum(jnp.where(eye, f_col, 0.0), axis=0, keepdims=True)
            i_row = jnp.sum(jnp.where(eye, i_col, 0.0), axis=0, keepdims=True)
            b_col = jnp.sum(jnp.where(causal, f_row, 0.0), axis=1, keepdims=True)
            b_row = jnp.sum(jnp.where(ss >= tt, f_col, 0.0), axis=0, keepdims=True)
            m_prev = m_ref[bh][:, 0:1]
            d = jnp.where(causal, b_col - b_row + i_row, NEG_BIG)
            inter = b_col + m_prev
            m_t = jnp.maximum(jnp.max(d, axis=1, keepdims=True), inter)
            w = jnp.exp(d - m_t)
            q = q_ref[b, :, h * dk:(h + 1) * dk]
            k = k_ref[b, :, h * dk:(h + 1) * dk] * (dk ** -0.5)
            vb = v_ref[b, :, h * dv:(h + 1) * dv].astype(BF16)
            qb = q.astype(BF16)
            s = lax.dot_general(qb, k.astype(BF16), nt, preferred_element_type=F32) * w
            wi = jnp.exp(inter - m_t)
            cst = c_ref[bh]
            nrow = n_ref[bh]
            num = wi * jnp.dot(qb, cst.astype(BF16), preferred_element_type=F32) + jnp.dot(
                s.astype(BF16), vb, preferred_element_type=F32)
            den = wi * jnp.sum(q * nrow, axis=1, keepdims=True) + jnp.sum(s, axis=1, keepdims=True)
            hh = num / jnp.maximum(jnp.abs(den), jnp.exp(-m_t))
            b_last = b_col[lc - 1:lc, :]
            g_col = b_last - b_col + i_col
            m_new = jnp.maximum(b_last + m_prev, jnp.max(g_col, axis=0, keepdims=True))
            wc = jnp.exp(b_last + m_prev - m_new)
            kw = k * jnp.exp(g_col - m_new)
            c_ref[bh] = wc * cst + lax.dot_general(kw.astype(BF16), vb, tn, preferred_element_type=F32)
            n_ref[bh] = wc * nrow + jnp.sum(kw, axis=0, keepdims=True)
            m_ref[bh] = jnp.broadcast_to(m_new, (1, LANES))
            hn = hh * lax.rsqrt(jnp.mean(hh * hh, axis=1, keepdims=True) + LN_EPS)
            hn = hn * ng_ref[:, h * dv:(h + 1) * dv] * _sigmoid(og_ref[b, :, h * dv:(h + 1) * dv])
            h_ref[b, :, h * dv:(h + 1) * dv] = hn.astype(h_ref.dtype)


def _mlstm_prompt(zm, gates, gate_bias, norm_g, bsz):
    rows, _ = zm.shape
    seq = rows // bsz
    nh = B_HEADS
    v_w = norm_g.shape[1]
    qk_w = v_w // 2
    dk, dv = qk_w // nh, v_w // nh
    lc = MLSTM_CHUNK
    z3 = zm.reshape(bsz, seq, zm.shape[1])
    gw = gates.shape[1]
    g3 = gates.reshape(bsz, seq, gw)
    nbh = bsz * nh
    full = lambda shape: pl.BlockSpec(shape, lambda c: (0,) * len(shape))
    return pl.pallas_call(
        functools.partial(_mlstm_prompt_kernel, nb=bsz, nh=nh, dk=dk, dv=dv),
        grid=(seq // lc,),
        in_specs=[
            pl.BlockSpec((bsz, lc, qk_w), lambda c: (0, c, 0)),
            pl.BlockSpec((bsz, lc, qk_w), lambda c: (0, c, 1)),
            pl.BlockSpec((bsz, lc, v_w), lambda c: (0, c, 1)),
            pl.BlockSpec((bsz, lc, v_w), lambda c: (0, c, 2)),
            pl.BlockSpec((bsz, lc, gw), lambda c: (0, c, 0)),
            full((1, gw)), full((1, v_w)),
        ],
        out_specs=[
            pl.BlockSpec((bsz, lc, v_w), lambda c: (0, c, 0)),
            full((nbh, dk, dv)), full((nbh, 1, dk)), full((nbh, 1, LANES)),
        ],
        out_shape=[
            jax.ShapeDtypeStruct((bsz, seq, v_w), BF16),
            jax.ShapeDtypeStruct((nbh, dk, dv), F32),
            jax.ShapeDtypeStruct((nbh, 1, dk), F32),
            jax.ShapeDtypeStruct((nbh, 1, LANES), F32),
        ],
        compiler_params=_cparams(1, 32),
        name="mlstm_prompt",
    )(z3, z3, z3, z3, g3, gate_bias, norm_g)


def _rglru_gates(xconv, wa_ref, ba, wx_ref, bx, lam):
    bd = wa_ref.shape[1]
    r_parts, i_parts = [], []
    for n in range(wa_ref.shape[0]):
        xb = xconv[:, n * bd:(n + 1) * bd].astype(BF16)
        r_parts.append(jnp.dot(xb, wa_ref[n].astype(BF16), preferred_element_type=F32))
        i_parts.append(jnp.dot(xb, wx_ref[n].astype(BF16), preferred_element_type=F32))
    r = _sigmoid(jnp.concatenate(r_parts, axis=1) + ba)
    ig = _sigmoid(jnp.concatenate(i_parts, axis=1) + bx)
    log_a = (-RG_C) * r * _softplus(-lam)
    a = jnp.exp(log_a)
    mult = jnp.sqrt(1.0 - a * a)
    return a, mult * ig * xconv


def _rglru_prompt_kernel(gate_ref, xr_ref, cw_ref, cb_ref, wa_ref, ba_ref, wx_ref, bx_ref, lam_ref,
                         y_ref, hl_ref, buf_ref, xext_ref, a_ref, b_ref, hs_ref, hc_ref):
    t = pl.program_id(1)
    tt = xr_ref.shape[0]
    pad = 8
    nbuf = CONV_WIDTH - 1

    @pl.when(t == 0)
    def _():
        xext_ref[0:pad, :] = jnp.zeros((pad, xext_ref.shape[1]), F32)
        hc_ref[...] = jnp.zeros_like(hc_ref)

    @pl.when(t > 0)
    def _():
        xext_ref[0:pad, :] = xext_ref[tt:tt + pad, :]

    xr = xr_ref[...]
    xext_ref[pad:pad + tt, :] = xr
    xconv = cb_ref[...] + xext_ref[pad - nbuf:pad - nbuf + tt, :] * cw_ref[0:1, :]
    for j in range(1, CONV_WIDTH):
        xconv = xconv + xext_ref[pad - nbuf + j:pad - nbuf + j + tt, :] * cw_ref[j:j + 1, :]
    a, bterm = _rglru_gates(xconv, wa_ref, ba_ref[...], wx_ref, bx_ref[...], lam_ref[...])
    a_ref[...] = a
    b_ref[...] = bterm

    def step(i, h):
        h = a_ref[pl.ds(i, 1), :] * h + b_ref[pl.ds(i, 1), :]
        hs_ref[pl.ds(i, 1), :] = h
        return h

    h_last = lax.fori_loop(0, tt, step, hc_ref[...], unroll=8)
    hc_ref[...] = h_last
    y_ref[...] = (_gelu_tanh(gate_ref[...]) * hs_ref[...]).astype(y_ref.dtype)
    hl_ref[...] = h_last
    buf_ref[...] = xr[tt - nbuf:tt, :]


def _rglru_prompt(zr, layer, conv_w, conv_b, w_a, b_a, w_x, b_x, lam, bsz, tt):
    rows, w2 = zr.shape
    width = w2 // 2
    seq = rows // bsz
    nt = seq // tt
    nblk, bd = w_a.shape[1], w_a.shape[2]
    vec = lambda a: a.reshape(a.shape[0], 1, width)
    vspec = pl.BlockSpec((None, 1, width), lambda b, t: (layer, 0, 0))
    wspec = pl.BlockSpec((None, nblk, bd, bd), lambda b, t: (layer, 0, 0, 0))
    nbuf = CONV_WIDTH - 1
    return pl.pallas_call(
        _rglru_prompt_kernel,
        grid=(bsz, nt),
        in_specs=[
            pl.BlockSpec((tt, width), lambda b, t: (b * nt + t, 0)),
            pl.BlockSpec((tt, width), lambda b, t: (b * nt + t, 1)),
            pl.BlockSpec((None, CONV_WIDTH, width), lambda b, t: (layer, 0, 0)),
            vspec, wspec, vspec, wspec, vspec, vspec,
        ],
        out_specs=[
            pl.BlockSpec((tt, width), lambda b, t: (b * nt + t, 0)),
            pl.BlockSpec((None, 1, width), lambda b, t: (b, 0, 0)),
            pl.BlockSpec((None, nbuf, width), lambda b, t: (b, 0, 0)),
        ],
        out_shape=[
            jax.ShapeDtypeStruct((rows, width), BF16),
            jax.ShapeDtypeStruct((bsz, 1, width), F32),
            jax.ShapeDtypeStruct((bsz, nbuf, width), F32),
        ],
        scratch_shapes=[
            pltpu.VMEM((tt + 8, width), F32), pltpu.VMEM((tt, width), F32),
            pltpu.VMEM((tt, width), F32), pltpu.VMEM((tt, width), F32), pltpu.VMEM((1, width), F32),
        ],
        compiler_params=_cparams(2, 48),
        name="rglru_prompt",
    )(zr, zr, conv_w, vec(conv_b), w_a, vec(b_a), w_x, vec(b_x), vec(lam))


def _moba_gate_kernel(tbl_ref, q_ref, *refs, nheads, nsteps, pages_per_step, pages_per_blk):
    b = pl.program_id(0)
    st = pl.program_id(1)
    hd = A_HEAD_DIM
    k_refs = refs[:pages_per_step]
    sel_ref, ksum_ref = refs[pages_per_step:]
    page = k_refs[0].shape[0]
    blks_per_step = pages_per_step // pages_per_blk
    for i in range(blks_per_step):
        tot = jnp.sum(k_refs[i * pages_per_blk][...], axis=0)
        for u in range(1, pages_per_blk):
            tot = tot + jnp.sum(k_refs[i * pages_per_blk + u][...], axis=0)
        ksum_ref[st * blks_per_step + i] = tot

    @pl.when(st == nsteps - 1)
    def _():
        nblk = ksum_ref.shape[0]
        inv = 1.0 / (pages_per_blk * page)
        rowi = lax.broadcasted_iota(jnp.int32, (nblk, 1), 0).astype(F32)
        out_r = lax.broadcasted_iota(jnp.int32, sel_ref.shape, 0)
        out_c = lax.broadcasted_iota(jnp.int32, sel_ref.shape, 1)
        out = jnp.zeros(sel_ref.shape, jnp.int32)
        qrow = q_ref[pl.ds(b, 1), :]
        for h in range(nheads):
            km = ksum_ref[:, h, :] * inv
            g = jnp.sum(km * qrow[:, h * hd:(h + 1) * hd], axis=1, keepdims=True)
            for i in range(MOBA_TOPK):
                mx = jnp.max(g, axis=0, keepdims=True)
                idx = jnp.min(jnp.where(g == mx, rowi, 1e9), axis=0, keepdims=True)
                out = jnp.where((out_r == h) & (out_c == i), idx.astype(jnp.int32), out)
                g = jnp.where(rowi == idx, -jnp.inf, g)
        sel_ref[...] = out


GATE_PAGES_PER_STEP = 8


def _moba_sample_gate(q, cache_k, layer, tbl_flat, nreq, npages):
    page, nheads, hd = cache_k.shape[2:]
    ppb = MOBA_BLOCK // page
    nblk = npages // ppb
    pps = GATE_PAGES_PER_STEP
    nsteps = npages // pps

    def page_spec(u):
        return pl.BlockSpec((None, None, page, nheads, hd),
                            lambda b, s, tbl: (layer, tbl[b * npages + s * pps + u], 0, 0, 0))

    grid_spec = pltpu.PrefetchScalarGridSpec(
        num_scalar_prefetch=1,
        grid=(nreq, nsteps),
        in_specs=[pl.BlockSpec(q.shape, lambda b, s, tbl: (0, 0))] + [page_spec(u) for u in range(pps)],
        out_specs=pl.BlockSpec((None, nheads, LANES), lambda b, s, tbl: (b, 0, 0)),
        scratch_shapes=[pltpu.VMEM((nblk, nheads, hd), F32)],
    )
    return pl.pallas_call(
        functools.partial(_moba_gate_kernel, nheads=nheads, nsteps=nsteps, pages_per_step=pps,
                          pages_per_blk=ppb),
        grid_spec=grid_spec,
        out_shape=jax.ShapeDtypeStruct((nreq, nheads, LANES), jnp.int32),
        compiler_params=_cparams(2, 32),
        name="moba_sample_gate",
    )(tbl_flat, q, *([cache_k] * pps))


def _moba_sample_attn_kernel(tbl_ref, sel_ref, q_ref, kn_ref, vn_ref, *refs, npg, scale):
    k_refs = refs[:npg]
    v_refs = refs[npg:2 * npg]
    o_ref = refs[2 * npg]
    b = pl.program_id(0)
    h = pl.program_id(1)
    nheads = q_ref.shape[1]
    is_h = lax.broadcasted_iota(jnp.int32, (1, nheads, 1), 1) == h
    q_m = jnp.where(is_h, q_ref[b][None], 0.0)

    def score(kp):
        part = jnp.sum(kp * q_m, axis=2, keepdims=True)
        return jnp.sum(part, axis=1, keepdims=True) * scale

    s_own = score(kn_ref[b][None])
    ss = [score(k_ref[...]) for k_ref in k_refs]
    m = s_own
    for s in ss:
        m = jnp.maximum(m, jnp.max(s, axis=0, keepdims=True))
    l = jnp.exp(s_own - m)
    acc = l * vn_ref[b][None]
    for s, v_ref in zip(ss, v_refs):
        p = jnp.exp(s - m)
        l = l + jnp.sum(p, axis=0, keepdims=True)
        acc = acc + jnp.sum(p * v_ref[...], axis=0, keepdims=True)
    out = jnp.sum(jnp.where(is_h, acc / l, 0.0), axis=1)
    o_ref[...] = out.astype(o_ref.dtype)


def _moba_sample_attn(q, k_new, v_new, cache_k, cache_v, layer, tbl_flat, sel_flat, nreq, npages):
    page, nheads, hd = cache_k.shape[2:]
    ppb = MOBA_BLOCK // page
    npg = MOBA_TOPK * ppb

    def page_spec(j):
        def page_map(b, h, tbl, sel):
            blk = sel[(b * nheads + h) * LANES + j // ppb]
            return (layer, tbl[b * npages + blk * ppb + j % ppb], 0, 0, 0)
        return pl.BlockSpec((None, None, page, nheads, hd), page_map)

    head = pl.BlockSpec(q.shape, lambda b, h, tbl, sel: (0, 0, 0))
    pages = [page_spec(j) for j in range(npg)]
    grid_spec = pltpu.PrefetchScalarGridSpec(
        num_scalar_prefetch=2,
        grid=(nreq, nheads),
        in_specs=[head, head, head] + pages + pages,
        out_specs=pl.BlockSpec((None, 1, hd), lambda b, h, tbl, sel: (b, 0, h)),
    )
    return pl.pallas_call(
        functools.partial(_moba_sample_attn_kernel, npg=npg, scale=hd ** -0.5),
        grid_spec=grid_spec,
        out_shape=jax.ShapeDtypeStruct((nreq, 1, nheads * hd), F32),
        compiler_params=_cparams(2, 32),
        name="moba_sample_attn",
    )(tbl_flat, sel_flat, q, k_new, v_new, *([cache_k] * npg), *([cache_v] * npg))


def _mlstm_sample_kernel(q_ref, k_ref, v_ref, og_ref, g_ref, gb_ref, ng_ref, c0_ref, n0_ref, m0_ref,
                         h_ref, c_ref, n_ref, m_ref, *, nh, dk, dv):
    b = pl.program_id(0)
    rr = lax.broadcasted_iota(jnp.int32, (dk, dk), 0)
    cc = lax.broadcasted_iota(jnp.int32, (dk, dk), 1)
    eye = rr == cc
    gates = g_ref[pl.ds(b, 1), :] + gb_ref[...]
    q_all = q_ref[pl.ds(b, 1), :]
    k_all = k_ref[pl.ds(b, 1), :]
    v_all = v_ref[pl.ds(b, 1), :]
    og_all = og_ref[pl.ds(b, 1), :]
    for h in range(nh):
        ii = gates[:, h:h + 1]
        fpre = gates[:, nh + h:nh + h + 1]
        ff = jnp.minimum(fpre, 0.0) - jnp.log1p(jnp.exp(-jnp.abs(fpre)))
        m_prev = m0_ref[h][:, 0:1]
        q = q_all[:, h * dk:(h + 1) * dk]
        k = k_all[:, h * dk:(h + 1) * dk] * (dk ** -0.5)
        v = v_all[:, h * dv:(h + 1) * dv]
        inter = ff + m_prev
        m_t = jnp.maximum(ii, inter)
        s = jnp.sum(q * k, axis=1, keepdims=True) * jnp.exp(ii - m_t)
        wi = jnp.exp(inter - m_t)
        q_col = jnp.sum(jnp.where(eye, q, 0.0), axis=1, keepdims=True)
        k_col = jnp.sum(jnp.where(eye, k, 0.0), axis=1, keepdims=True)
        cst = c0_ref[h]
        nrow = n0_ref[h]
        num = wi * jnp.sum(q_col * cst, axis=0, keepdims=True) + s * v
        den = wi * jnp.sum(q * nrow, axis=1, keepdims=True) + s
        hh = num / jnp.maximum(jnp.abs(den), jnp.exp(-m_t))
        m_new = m_t
        wc = jnp.exp(inter - m_new)
        wg = jnp.exp(ii - m_new)
        c_ref[h] = wc * cst + (wg * k_col) * v
        n_ref[h] = wc * nrow + wg * k
        m_ref[h] = jnp.broadcast_to(m_new, (1, LANES))
        hn = hh * lax.rsqrt(jnp.mean(hh * hh, axis=1, keepdims=True) + LN_EPS)
        hn = hn * ng_ref[:, h * dv:(h + 1) * dv] * _sigmoid(og_all[:, h * dv:(h + 1) * dv])
        h_ref[:, h * dv:(h + 1) * dv] = hn.astype(h_ref.dtype)


def _mlstm_sample(zm, gates, gate_bias, norm_g, c0, n0, m0, layer, nreq):
    nh = B_HEADS
    v_w = norm_g.shape[1]
    qk_w = v_w // 2
    dk, dv = qk_w // nh, v_w // nh
    rows = zm.shape[0]
    full = lambda shape: pl.BlockSpec(shape, lambda b: (0,) * len(shape))
    return pl.pallas_call(
        functools.partial(_mlstm_sample_kernel, nh=nh, dk=dk, dv=dv),
        grid=(nreq,),
        in_specs=[
            pl.BlockSpec((rows, qk_w), lambda b: (0, 0)),
            pl.BlockSpec((rows, qk_w), lambda b: (0, 1)),
            pl.BlockSpec((rows, v_w), lambda b: (0, 1)),
            pl.BlockSpec((rows, v_w), lambda b: (0, 2)),
            full(gates.shape), full(gate_bias.shape), full((1, v_w)),
            pl.BlockSpec((None, None, nh, dk, dv), lambda b: (layer, b, 0, 0, 0)),
            pl.BlockSpec((None, None, nh, 1, dk), lambda b: (layer, b, 0, 0, 0)),
            pl.BlockSpec((None, None, nh, 1, LANES), lambda b: (layer, b, 0, 0, 0)),
        ],
        out_specs=[
            pl.BlockSpec((None, 1, v_w), lambda b: (b, 0, 0)),
            pl.BlockSpec((None, nh, dk, dv), lambda b: (b, 0, 0, 0)),
            pl.BlockSpec((None, nh, 1, dk), lambda b: (b, 0, 0, 0)),
            pl.BlockSpec((None, nh, 1, LANES), lambda b: (b, 0, 0, 0)),
        ],
        out_shape=[
            jax.ShapeDtypeStruct((nreq, 1, v_w), F32),
            jax.ShapeDtypeStruct((nreq, nh, dk, dv), F32),
            jax.ShapeDtypeStruct((nreq, nh, 1, dk), F32),
            jax.ShapeDtypeStruct((nreq, nh, 1, LANES), F32),
        ],
        compiler_params=_cparams(1, 32),
        name="mlstm_sample",
    )(zm, zm, zm, zm, gates, gate_bias, norm_g, c0, n0, m0)


def _rglru_sample_kernel(gate_ref, xr_ref, buf_ref, h0_ref, cw_ref, cb_ref, wa_ref, ba_ref, wx_ref,
                         bx_ref, lam_ref, y_ref, h_ref, nbuf_ref):
    nbuf = CONV_WIDTH - 1
    xr = xr_ref[...]
    xconv = cb_ref[...] + buf_ref[0] * cw_ref[0:1, :]
    for j in range(1, nbuf):
        xconv = xconv + buf_ref[j] * cw_ref[j:j + 1, :]
    xconv = xconv + xr * cw_ref[nbuf:nbuf + 1, :]
    a, bterm = _rglru_gates(xconv, wa_ref, ba_ref[...], wx_ref, bx_ref[...], lam_ref[...])
    h = a * h0_ref[...] + bterm
    h_ref[...] = h
    y = _gelu_tanh(gate_ref[...]) * h
    pad_rows = y_ref.shape[0] - y.shape[0]
    y_ref[...] = jnp.concatenate([y, jnp.zeros((pad_rows, y.shape[1]), F32)], axis=0).astype(y_ref.dtype)
    for j in range(nbuf - 1):
        nbuf_ref[j] = buf_ref[j + 1]
    nbuf_ref[nbuf - 1] = xr


def _rglru_sample(zr, layer, buf_t, h0, conv_w, conv_b, w_a, b_a, w_x, b_x, lam, nreq):
    rows, w2 = zr.shape
    width = w2 // 2
    nblk, bd = w_a.shape[1], w_a.shape[2]
    nbuf = CONV_WIDTH - 1
    vec = lambda a: a.reshape(a.shape[0], 1, width)
    vspec = pl.BlockSpec((None, 1, width), lambda i: (layer, 0, 0))
    wspec = pl.BlockSpec((None, nblk, bd, bd), lambda i: (layer, 0, 0, 0))
    return pl.pallas_call(
        _rglru_sample_kernel,
        grid=(1,),
        in_specs=[
            pl.BlockSpec((nreq, width), lambda i: (0, 0)),
            pl.BlockSpec((nreq, width), lambda i: (0, 1)),
            pl.BlockSpec((None, nbuf, nreq, width), lambda i: (layer, 0, 0, 0)),
            pl.BlockSpec((None, nreq, width), lambda i: (layer, 0, 0)),
            pl.BlockSpec((None, CONV_WIDTH, width), lambda i: (layer, 0, 0)),
            vspec, wspec, vspec, wspec, vspec, vspec,
        ],
        out_specs=[
            pl.BlockSpec((rows, width), lambda i: (0, 0)),
            pl.BlockSpec((nreq, width), lambda i: (0, 0)),
            pl.BlockSpec((nbuf, nreq, width), lambda i: (0, 0, 0)),
        ],
        out_shape=[
            jax.ShapeDtypeStruct((rows, width), BF16),
            jax.ShapeDtypeStruct((nreq, width), F32),
            jax.ShapeDtypeStruct((nbuf, nreq, width), F32),
        ],
        compiler_params=_cparams(1, 32),
        name="rglru_sample",
    )(zr, zr, buf_t, h0, conv_w, vec(conv_b), w_a, vec(b_a), w_x, vec(b_x), vec(lam))


def _rope_tables(pos):
    half = A_HEAD_DIM // 2
    inv = ROPE_THETA ** (-jnp.arange(half, dtype=F32) / half)
    ang = pos.astype(F32)[:, None] * inv[None, :]
    cos, sin = jnp.cos(ang), jnp.sin(ang)
    return jnp.concatenate([cos, cos], axis=1), jnp.concatenate([-sin, sin], axis=1)


def kernel(x_prompt, x_sample, cache_k, cache_v, state_mlstm_c, state_mlstm_n, state_mlstm_m,
           state_rglru_h, state_conv, page_table, c_prompt, c_sample, w_ada, b_ada, ln_g, ln_b,
           w_ffn1_in, w_ffn1_out, w_ffn2_in, w_ffn2_out, w_in_ab, b_igate, b_fgate, mlstm_norm_g,
           w_out_ab, w_in_rg, conv_w, conv_b, w_rg_a, b_rg_a, w_rg_x, b_rg_x, lru_lambda, w_out_rg):
    bsz, seq, d = x_prompt.shape
    nreq = x_sample.shape[0]
    depth = w_ada.shape[0]
    n_even = w_in_ab.shape[0]
    a_w = cache_k.shape[3] * cache_k.shape[4]
    v_w = mlstm_norm_g.shape[1]
    qk_w = v_w // 2
    m_w = 2 * qk_w + 2 * v_w
    nh_b = B_HEADS
    past_len = page_table.shape[1] * cache_k.shape[2]
    npages = page_table.shape[1]
    rows_p = bsz * seq
    rs = SAMPLE_ROWS

    tm = 1024
    tn = 512
    tm_out = 256
    tk_cast = 512
    tt_rg = 256

    c_all = jnp.concatenate([c_prompt, c_sample, jnp.zeros((rs - bsz - nreq, d), F32)], axis=0)
    mods = _mods(c_all, w_ada, b_ada).reshape(depth, rs, 9, d)
    mods_p = jnp.transpose(mods[:, :bsz], (0, 2, 1, 3)).reshape(depth * 9 * bsz, 1, d)
    mods_s = jnp.transpose(mods[:, bsz:bsz + nreq], (0, 2, 1, 3))
    mods_s = jnp.pad(mods_s, ((0, 0), (0, 0), (0, rs - nreq), (0, 0))).reshape(depth * 9, rs, d)

    def mod_spec_p(l, s, j):
        base = ((l * 3 + s) * 3 + j) * bsz

        def make(tile_rows):
            tiles_per_batch = seq // tile_rows
            return pl.BlockSpec((None, 1, d), lambda i: (base + i // tiles_per_batch, 0, 0))
        return make

    def mod_spec_s(l, s, j):
        row = (l * 3 + s) * 3 + j

        def make(tile_rows):
            return pl.BlockSpec((None, rs, d), lambda i: (row, 0, 0))
        return make

    ln_g3 = ln_g.reshape(depth * 3, 1, d)
    ln_b3 = ln_b.reshape(depth * 3, 1, d)

    cos_p, sin_p = _rope_tables(jnp.arange(seq, dtype=jnp.int32))
    cos_s, sin_s = _rope_tables(jnp.full((rs,), past_len, dtype=jnp.int32))

    tbl_flat = page_table.reshape(-1).astype(jnp.int32)
    n0_s = state_mlstm_n[:, :, :, None, :]
    m0_s = jnp.broadcast_to(state_mlstm_m[:, :, :, None, None], state_mlstm_m.shape + (1, LANES))
    conv_t = jnp.transpose(state_conv, (0, 2, 1, 3))
    gate_bias = jnp.concatenate([b_igate, b_fgate], axis=1)
    w_ab_t = jnp.swapaxes(w_in_ab, 1, 2)

    xp = x_prompt.reshape(rows_p, d)
    xs = jnp.pad(x_sample.reshape(nreq, d), ((0, rs - nreq), (0, 0)))
    up = _modulate(xp, mod_spec_p(0, 0, 1), mod_spec_p(0, 0, 0), mods_p, tm)
    us = _modulate(xs, mod_spec_s(0, 0, 1), mod_spec_s(0, 0, 0), mods_s, rs)

    op = dict(k=[], v=[], c=[], n=[], m=[], h=[], buf=[])
    os_ = dict(k=[], v=[], c=[], n=[], m=[], h=[], buf=[])
    rope = (cos_p, sin_p, cos_s, sin_s)
    k_st = v_st = None
    pad_rows = lambda t, w: jnp.pad(t.reshape(nreq, w), ((0, rs - nreq), (0, 0))).astype(BF16)
    heads3 = lambda t: t.reshape(rs, -1, A_HEAD_DIM)

    for l in range(depth):
        wb_ffn1 = _cast_bf16(w_ffn1_out, l, tk_cast)
        wb_ffn2 = _cast_bf16(w_ffn2_out, l, tk_cast)
        wb_mix = _cast_bf16(w_out_ab if l % 2 == 0 else w_out_rg, l // 2, tk_cast)

        def post(parts_p, parts_s, wb, s, res_w, nxt):
            nl, ns = nxt
            new_p = _out_ln(parts_p, wb, xp, mods_p, mod_spec_p(l, s, 2), mod_spec_p(nl, ns, 1),
                            mod_spec_p(nl, ns, 0), ln_g3, ln_b3, l * 3 + s, res_w, tm_out)
            new_s = _out_ln(parts_s, wb, xs, mods_s, mod_spec_s(l, s, 2), mod_spec_s(nl, ns, 1),
                            mod_spec_s(nl, ns, 0), ln_g3, ln_b3, l * 3 + s, res_w, rs)
            return new_p, new_s

        act_p, act_s = _swiglu_in(up, us, w_ffn1_in, l, tm, tn)
        (xp, up), (xs, us) = post([act_p], [act_s], wb_ffn1, 0, FFN_RES, (l, 1))

        if l % 2 == 0:
            e = l // 2
            q_p, q_s = _proj(up, us, w_ab_t, e, 0, a_w, tm, tn, F32, rope=rope, w_t=True, name="proj_q")
            k_st = _proj(up, us, w_ab_t, e, a_w, a_w, tm, tn, F32, rope=rope, w_t=True,
                         stack=(n_even, e, k_st), name="proj_k")
            v_st = _proj(up, us, w_ab_t, e, 2 * a_w, a_w, tm, tn, F32, w_t=True,
                         stack=(n_even, e, v_st), name="proj_v")
            k_s, v_s = k_st[1][e], v_st[1][e]
            zm_p, zm_s = _proj(up, us, w_ab_t, e, 3 * a_w, m_w, tm, tn, F32, w_t=True, name="proj_mlstm")
            n_gates = 2 * nh_b
            g_p, g_s = _proj(up, us, w_ab_t, e, 3 * a_w + m_w, n_gates, tm, n_gates, F32, w_t=True,
                             name="proj_gates")
            gb = gate_bias[e:e + 1]
            ng = mlstm_norm_g[e:e + 1]

            oa = _moba_prompt(q_p, k_st[0], v_st[0], e, bsz)
            hm, c1, n1, m1 = _mlstm_prompt(zm_p, g_p, gb, ng, bsz)
            mix_p = [oa, hm.reshape(rows_p, v_w)]
            op["c"].append(c1.reshape(bsz, nh_b, c1.shape[1], c1.shape[2]))
            op["n"].append(n1.reshape(bsz, nh_b, -1))
            op["m"].append(m1[:, 0, 0].reshape(bsz, nh_b))

            sel = _moba_sample_gate(q_s, cache_k, e, tbl_flat, nreq, npages)
            oa = _moba_sample_attn(heads3(q_s), heads3(k_s), heads3(v_s), cache_k, cache_v, e,
                                   tbl_flat, sel.reshape(-1), nreq, npages)
            hm, c1, n1, m1 = _mlstm_sample(zm_s, g_s, gb, ng, state_mlstm_c, n0_s, m0_s, e, nreq)
            mix_s = [pad_rows(oa, a_w), pad_rows(hm, v_w)]
            os_["c"].append(c1)
            os_["n"].append(n1.reshape(nreq, nh_b, -1))
            os_["m"].append(m1[:, :, 0, 0])
        else:
            od = l // 2
            zr_p, zr_s = _proj(up, us, w_in_rg, od, 0, w_in_rg.shape[2], tm, tn, F32, name="proj_rg")
            y_p, h1, nb = _rglru_prompt(zr_p, od, conv_w, conv_b, w_rg_a, b_rg_a, w_rg_x, b_rg_x,
                                        lru_lambda, bsz, tt_rg)
            op["h"].append(h1.reshape(bsz, -1))
            op["buf"].append(nb)
            y_s, h1, nb = _rglru_sample(zr_s, od, conv_t, state_rglru_h, conv_w, conv_b, w_rg_a,
                                        b_rg_a, w_rg_x, b_rg_x, lru_lambda, nreq)
            os_["h"].append(h1)
            os_["buf"].append(jnp.transpose(nb, (1, 0, 2)))
            mix_p, mix_s = [y_p], [y_s]
        (xp, up), (xs, us) = post(mix_p, mix_s, wb_mix, 1, 1.0, (l, 2))

        act_p, act_s = _swiglu_in(up, us, w_ffn2_in, l, tm, tn)
        (xp, up), (xs, us) = post([act_p], [act_s], wb_ffn2, 2, FFN_RES, (min(l + 1, depth - 1), 0))

    st = jnp.stack
    kv_p = lambda t: t.reshape(n_even, bsz, seq, -1, A_HEAD_DIM)
    kv_s = lambda t: t[:, :nreq].reshape(n_even, nreq, 1, -1, A_HEAD_DIM)
    return (xp.reshape(bsz, seq, d), xs[:nreq].reshape(nreq, 1, d),
            kv_p(k_st[0]), kv_p(v_st[0]), kv_s(k_st[1]), kv_s(v_st[1]),
            st(op["c"]), st(op["n"]), st(op["m"]), st(os_["c"]), st(os_["n"]), st(os_["m"]),
            st(op["h"]), st(op["buf"]), st(os_["h"]), st(os_["buf"]))
```

```python
import functools

import numpy as np
import jax
import jax.numpy as jnp
from jax import lax
from jax.experimental import pallas as pl
from jax.experimental.pallas import tpu as pltpu

F32 = jnp.float32
BF16 = jnp.bfloat16

DEPTH = 4
MOBA_BLOCK = 256
MOBA_TOPK = 3
MOBA_GROUP = 4
MOBA_HEADS_PER_STEP = 4
ROPE_THETA = 10000.0
A_HEAD_DIM = 128
B_HEADS = 4
MLSTM_CHUNK = 64
RG_BLOCKS = 16
CONV_WIDTH = 4
RG_C = 8.0
FFN_RES = 0.5
ALPHA = (2.0 * DEPTH) ** 0.25
LN_EPS = 1e-5

LANES = 128
SAMPLE_ROWS = 16
NEG_BIG = -1e30
MIB = 1024 * 1024


def _cparams(n_axes, vmem_mib):
    return pltpu.CompilerParams(
        dimension_semantics=("arbitrary",) * n_axes,
        vmem_limit_bytes=int(vmem_mib * MIB),
    )


def _sigmoid(x):
    return 0.5 * jnp.tanh(0.5 * x) + 0.5


def _softplus(x):
    return jnp.maximum(x, 0.0) + jnp.log1p(jnp.exp(-jnp.abs(x)))


def _gelu_tanh(x):
    return 0.5 * x * (1.0 + jnp.tanh(0.7978845608028654 * (x + 0.044715 * x * x * x)))


def _mods_kernel(c_ref, w_ref, b_ref, o_ref):
    c = c_ref[...]
    s = (c * _sigmoid(c)).astype(BF16)
    o_ref[...] = jnp.dot(s, w_ref[...].astype(BF16), preferred_element_type=F32) + b_ref[...]


def _mods(c_all, w_ada, b_ada):
    depth, k, n = w_ada.shape
    rows = c_all.shape[0]
    tn = 1024 if n % 1024 == 0 else n
    return pl.pallas_call(
        _mods_kernel,
        grid=(depth, n // tn),
        in_specs=[
            pl.BlockSpec((rows, k), lambda l, j: (0, 0)),
            pl.BlockSpec((None, k, tn), lambda l, j: (l, 0, j)),
            pl.BlockSpec((None, 1, tn), lambda l, j: (l, 0, j)),
        ],
        out_specs=pl.BlockSpec((None, rows, tn), lambda l, j: (l, 0, j)),
        out_shape=jax.ShapeDtypeStruct((depth, rows, n), F32),
        compiler_params=_cparams(2, 48),
        name="adaln_mods",
    )(c_all, w_ada, b_ada.reshape(depth, 1, n))


def _modulate_kernel(x_ref, sc_ref, sh_ref, u_ref):
    u_ref[...] = (x_ref[...] * (1.0 + sc_ref[...]) + sh_ref[...]).astype(u_ref.dtype)


def _modulate(x, sc_spec, sh_spec, mods, tm):
    m, d = x.shape
    return pl.pallas_call(
        _modulate_kernel,
        grid=(m // tm,),
        in_specs=[pl.BlockSpec((tm, d), lambda i: (i, 0)), sc_spec(tm), sh_spec(tm)],
        out_specs=pl.BlockSpec((tm, d), lambda i: (i, 0)),
        out_shape=jax.ShapeDtypeStruct((m, d), BF16),
        compiler_params=_cparams(1, 32),
        name="modulate0",
    )(x, mods, mods)


def _rope_store(z, cos, sin, o_ref):
    for j in range(z.shape[1] // A_HEAD_DIM):
        zj = z[:, j * A_HEAD_DIM:(j + 1) * A_HEAD_DIM]
        o_ref[:, j * A_HEAD_DIM:(j + 1) * A_HEAD_DIM] = (
            zj * cos + pltpu.roll(zj, A_HEAD_DIM // 2, 1) * sin).astype(o_ref.dtype)


def _proj_kernel(*refs, rope, n_alias, w_mode):
    refs = refs[:len(refs) - 3 - n_alias] + refs[len(refs) - 3:]
    if rope:
        x_ref, xs_ref, w_ref, cos_ref, sin_ref, coss_ref, sins_ref, o_ref, os_ref, wb_ref = refs
    else:
        x_ref, xs_ref, w_ref, o_ref, os_ref, wb_ref = refs

    def mm(x):
        if w_mode == "nk_contract_last":
            return lax.dot_general(x, wb_ref[...], (((1,), (1,)), ((), ())), preferred_element_type=F32)
        return jnp.dot(x, wb_ref[...], preferred_element_type=F32)

    @pl.when(pl.program_id(1) == 0)
    def _():
        w = w_ref[...]
        if w_mode == "nk_transpose":
            w = w.T
        wb_ref[...] = w.astype(BF16)
        zs = mm(xs_ref[...])
        if rope:
            _rope_store(zs, coss_ref[...], sins_ref[...], os_ref)
        else:
            os_ref[...] = zs.astype(os_ref.dtype)

    z = mm(x_ref[...])
    if rope:
        _rope_store(z, cos_ref[...], sin_ref[...], o_ref)
    else:
        o_ref[...] = z.astype(o_ref.dtype)


def _swiglu_in_kernel(x_ref, xs_ref, wg_ref, wv_ref, wo_ref, o_ref, os_ref, wob_ref, wgb_ref, wvb_ref):
    def act(x):
        g = jnp.dot(x, wgb_ref[...], preferred_element_type=F32)
        v = jnp.dot(x, wvb_ref[...], preferred_element_type=F32)
        return g * _sigmoid(g) * v

    @pl.when(pl.program_id(1) == 0)
    def _():
        wgb_ref[...] = wg_ref[...].astype(BF16)
        wvb_ref[...] = wv_ref[...].astype(BF16)
        wob_ref[...] = wo_ref[...].astype(BF16)
        os_ref[...] = act(xs_ref[...]).astype(os_ref.dtype)

    o_ref[...] = act(x_ref[...]).astype(o_ref.dtype)


def _proj(x, xs, w, layer, col0, ncols, tm, tn, out_dtype, rope=None, stack=None, w_t=False,
          name="proj"):
    m, k = x.shape
    rows_s = xs.shape[0]
    c0 = col0 // tn
    if w_t:
        w_spec = pl.BlockSpec((None, tn, k), lambda j, i: (layer, c0 + j, 0))
        w_mode = "nk_transpose" if tn % LANES == 0 else "nk_contract_last"
    else:
        w_spec = pl.BlockSpec((None, k, tn), lambda j, i: (layer, 0, c0 + j))
        w_mode = "kn"
    wb_shape = (tn, k) if w_mode == "nk_contract_last" else (k, tn)
    in_specs = [
        pl.BlockSpec((tm, k), lambda j, i: (i, 0)),
        pl.BlockSpec((rows_s, k), lambda j, i: (0, 0)),
        w_spec,
    ]
    args = [x, xs, w]
    if rope is not None:
        period = rope[0].shape[0] // tm
        in_specs += [pl.BlockSpec((tm, A_HEAD_DIM), lambda j, i: (i % period, 0))] * 2
        in_specs += [pl.BlockSpec((rows_s, A_HEAD_DIM), lambda j, i: (0, 0))] * 2
        args += list(rope)
    aliases = {}
    if stack is None:
        out_specs = [pl.BlockSpec((tm, tn), lambda j, i: (i, j)),
                     pl.BlockSpec((rows_s, tn), lambda j, i: (0, j))]
        out_shape = [jax.ShapeDtypeStruct((m, ncols), out_dtype),
                     jax.ShapeDtypeStruct((rows_s, ncols), out_dtype)]
    else:
        n_stack, idx, prev = stack
        out_specs = [pl.BlockSpec((None, tm, tn), lambda j, i: (idx, i, j)),
                     pl.BlockSpec((None, rows_s, tn), lambda j, i: (idx, 0, j))]
        out_shape = [jax.ShapeDtypeStruct((n_stack, m, ncols), out_dtype),
                     jax.ShapeDtypeStruct((n_stack, rows_s, ncols), out_dtype)]
        if prev is not None:
            aliases = {len(args): 0, len(args) + 1: 1}
            in_specs += [pl.BlockSpec(memory_space=pl.ANY)] * 2
            args += list(prev)
    return pl.pallas_call(
        functools.partial(_proj_kernel, rope=rope is not None, n_alias=len(aliases), w_mode=w_mode),
        grid=(ncols // tn, m // tm),
        in_specs=in_specs,
        out_specs=out_specs,
        out_shape=out_shape,
        input_output_aliases=aliases,
        scratch_shapes=[pltpu.VMEM(wb_shape, BF16)],
        compiler_params=_cparams(2, 48),
        name=name,
    )(*args)


def _swiglu_in(x, xs, w, w_out, layer, tm, tn):
    m, k = x.shape
    rows_s = xs.shape[0]
    dff = w.shape[2] // 2
    d_out = w_out.shape[2]
    nv = dff // tn
    return pl.pallas_call(
        _swiglu_in_kernel,
        grid=(dff // tn, m // tm),
        in_specs=[
            pl.BlockSpec((tm, k), lambda j, i: (i, 0)),
            pl.BlockSpec((rows_s, k), lambda j, i: (0, 0)),
            pl.BlockSpec((None, k, tn), lambda j, i: (layer, 0, j)),
            pl.BlockSpec((None, k, tn), lambda j, i: (layer, 0, nv + j)),
            pl.BlockSpec((None, tn, d_out), lambda j, i: (layer, j, 0)),
        ],
        out_specs=[pl.BlockSpec((tm, tn), lambda j, i: (i, j)),
                   pl.BlockSpec((rows_s, tn), lambda j, i: (0, j)),
                   pl.BlockSpec((tn, d_out), lambda j, i: (j, 0))],
        out_shape=[jax.ShapeDtypeStruct((m, dff), BF16), jax.ShapeDtypeStruct((rows_s, dff), BF16),
                   jax.ShapeDtypeStruct((dff, d_out), BF16)],
        scratch_shapes=[pltpu.VMEM((k, tn), BF16), pltpu.VMEM((k, tn), BF16)],
        compiler_params=_cparams(2, 60),
        name="swiglu_in",
    )(x, xs, w, w, w_out)


def _cast_kernel(w_ref, o_ref):
    o_ref[...] = w_ref[...].astype(o_ref.dtype)


def _cast_bf16(w, layer, tk):
    _, k, n = w.shape
    return pl.pallas_call(
        _cast_kernel,
        grid=(k // tk,),
        in_specs=[pl.BlockSpec((None, tk, n), lambda i: (layer, i, 0))],
        out_specs=pl.BlockSpec((tk, n), lambda i: (i, 0)),
        out_shape=jax.ShapeDtypeStruct((k, n), BF16),
        compiler_params=_cparams(1, 32),
        name="cast_bf16",
    )(w)


def _out_ln_kernel(*refs, n_parts, res_w):
    ap_refs = refs[:n_parts]
    as_refs = refs[n_parts:2 * n_parts]
    (w_ref, x_ref, gate_ref, sc_ref, sh_ref, xs_ref, gates_ref, scs_ref, shs_ref, lng_ref, lnb_ref,
     xo_ref, uo_ref, xso_ref, uso_ref) = refs[2 * n_parts:]

    def run(a_refs, x_ref, gate_ref, sc_ref, sh_ref, xo_ref, uo_ref):
        acc = None
        off = 0
        for a_ref in a_refs:
            kk = a_ref.shape[1]
            part = jnp.dot(a_ref[...], w_ref[off:off + kk, :], preferred_element_type=F32)
            acc = part if acc is None else acc + part
            off += kk
        y = ALPHA * x_ref[...] + (res_w * (1.0 + gate_ref[...])) * acc
        mu = jnp.mean(y, axis=-1, keepdims=True)
        yc = y - mu
        var = jnp.mean(yc * yc, axis=-1, keepdims=True)
        xn = yc * lax.rsqrt(var + LN_EPS) * lng_ref[...] + lnb_ref[...]
        xo_ref[...] = xn
        uo_ref[...] = (xn * (1.0 + sc_ref[...]) + sh_ref[...]).astype(uo_ref.dtype)

    @pl.when(pl.program_id(0) == 0)
    def _():
        run(as_refs, xs_ref, gates_ref, scs_ref, shs_ref, xso_ref, uso_ref)

    run(ap_refs, x_ref, gate_ref, sc_ref, sh_ref, xo_ref, uo_ref)


def _out_ln(parts, parts_s, wb, x, xs, mods, mods_s, specs, specs_s, ln_g, ln_b, ln_row, res_w, tm):
    m = parts[0].shape[0]
    rows_s = xs.shape[0]
    k, d = wb.shape
    ln_spec = pl.BlockSpec((None, 1, d), lambda i: (ln_row, 0, 0))
    row = lambda width: pl.BlockSpec((tm, width), lambda i: (i, 0))
    whole = lambda width: pl.BlockSpec((rows_s, width), lambda i: (0, 0))
    return pl.pallas_call(
        functools.partial(_out_ln_kernel, n_parts=len(parts), res_w=res_w),
        grid=(m // tm,),
        in_specs=[row(p.shape[1]) for p in parts] + [whole(p.shape[1]) for p in parts_s] + [
            pl.BlockSpec((k, d), lambda i: (0, 0), pipeline_mode=pl.Buffered(1)),
            row(d)] + [s(tm) for s in specs] + [whole(d)] + [s(rows_s) for s in specs_s] + [
            ln_spec, ln_spec],
        out_specs=[row(d), row(d), whole(d), whole(d)],
        out_shape=[jax.ShapeDtypeStruct((m, d), F32), jax.ShapeDtypeStruct((m, d), BF16),
                   jax.ShapeDtypeStruct((rows_s, d), F32), jax.ShapeDtypeStruct((rows_s, d), BF16)],
        compiler_params=_cparams(1, 56),
        name="out_postnorm",
    )(*parts, *parts_s, wb, x, mods, mods, mods, xs, mods_s, mods_s, mods_s, ln_g, ln_b)


def _moba_prompt_kernel(q_ref, k_ref, v_ref, o_ref, kb_ref, vb_ref, kmean_ref,
                        m_ref, acc_ref, *, nblk, scale):
    qi = pl.program_id(2)
    blk = MOBA_BLOCK
    hd = A_HEAD_DIM
    nheads = q_ref.shape[1] // hd
    lanes = lambda hh: slice(hh * hd, (hh + 1) * hd)

    @pl.when(qi == 0)
    def _():
        seq = k_ref.shape[0]
        r = lax.broadcasted_iota(jnp.int32, (seq, hd), 0)
        c = lax.broadcasted_iota(jnp.int32, (seq, hd), 1)
        in_blk = (r >= c * blk) & (r < c * blk + blk)
        onehot = jnp.where(in_blk, 1.0, 0.0).astype(BF16)
        kmean_ref[...] = jnp.zeros_like(kmean_ref)
        for hh in range(nheads):
            kf = k_ref[:, lanes(hh)]
            kb_ref[hh, :, 0:hd] = kf.astype(BF16)
            kb_ref[hh, :, hd:2 * hd] = onehot
            vb_ref[hh, :, 0:hd] = v_ref[:, lanes(hh)].astype(BF16)
            vb_ref[hh, :, hd:2 * hd] = jnp.ones((seq, hd), BF16)
            kmean_ref[hh, 0:nblk, :] = jnp.mean(kf.reshape(nblk, blk, hd), axis=1)

    nt = (((1,), (1,)), ((), ()))
    start = pl.multiple_of(qi * blk, blk)
    q_augs = []
    for hh in range(nheads):
        q = q_ref[:, lanes(hh)]
        gate = lax.dot_general(q, kmean_ref[hh], nt, precision=lax.Precision.HIGHEST,
                               preferred_element_type=F32)
        lane = lax.broadcasted_iota(jnp.int32, gate.shape, 1)
        lanef = lane.astype(F32)
        g = jnp.where(lane < qi, gate, -jnp.inf)
        selm = jnp.zeros(gate.shape, F32)
        for _ in range(MOBA_TOPK):
            mx = jnp.max(g, axis=1, keepdims=True)
            cand = jnp.where(g == mx, lanef, 1e9)
            cand = jnp.where(mx > -jnp.inf, cand, 1e9)
            pick = lanef == jnp.min(cand, axis=1, keepdims=True)
            selm = jnp.where(pick, 1.0, selm)
            g = jnp.where(pick, -jnp.inf, g)
        qb = q.astype(BF16)
        q_augs.append(jnp.concatenate([qb, ((1.0 - selm) * NEG_BIG).astype(BF16)], axis=1))
        s = lax.dot_general(qb, kb_ref[hh, pl.ds(start, blk), 0:hd], nt, preferred_element_type=F32) * scale
        row = lax.broadcasted_iota(jnp.int32, s.shape, 0)
        col = lax.broadcasted_iota(jnp.int32, s.shape, 1)
        s = jnp.where(col <= row, s, NEG_BIG)
        m0 = jnp.max(s, axis=1, keepdims=True)
        p = jnp.exp(s - m0)
        m_ref[hh] = jnp.broadcast_to(m0, m_ref.shape[1:])
        acc_ref[hh] = jnp.dot(p.astype(BF16), vb_ref[hh, pl.ds(start, blk), :], preferred_element_type=F32)

    grp = MOBA_GROUP
    span = grp * blk

    def past_group(gi, carry):
        st = pl.multiple_of(gi * span, span)
        for hh in range(nheads):
            sn = lax.dot_general(q_augs[hh], kb_ref[hh, pl.ds(st, span), :], nt,
                                 preferred_element_type=F32) * scale
            m_prev = m_ref[hh]
            m_new = jnp.maximum(m_prev, jnp.max(sn, axis=1, keepdims=True))
            a = jnp.exp(m_prev - m_new)
            pn = jnp.exp(sn - jnp.concatenate([m_new] * (span // hd), axis=1))
            acc_ref[hh] = jnp.concatenate([a, a], axis=1) * acc_ref[hh] + jnp.dot(
                pn.astype(BF16), vb_ref[hh, pl.ds(st, span), :], preferred_element_type=F32)
            m_ref[hh] = m_new
        return carry

    lax.fori_loop(0, (qi + grp - 1) // grp, past_group, 0)
    for hh in range(nheads):
        acc = acc_ref[hh]
        o_ref[:, lanes(hh)] = (acc[:, 0:hd] / acc[:, hd:2 * hd]).astype(o_ref.dtype)


def _moba_prompt(q, k, v, slab, bsz):
    rows, width = q.shape
    seq = rows // bsz
    nh = width // A_HEAD_DIM
    nblk = seq // MOBA_BLOCK
    assert nblk % MOBA_GROUP == 0
    blk = MOBA_BLOCK
    hd = A_HEAD_DIM
    hps = MOBA_HEADS_PER_STEP
    assert nh % hps == 0
    return pl.pallas_call(
        functools.partial(_moba_prompt_kernel, nblk=nblk, scale=hd ** -0.5),
        grid=(bsz, nh // hps, nblk),
        in_specs=[
            pl.BlockSpec((blk, hps * hd), lambda b, h, i: (b * nblk + i, h)),
            pl.BlockSpec((None, seq, hps * hd), lambda b, h, i: (slab, b, h), pipeline_mode=pl.Buffered(1)),
            pl.BlockSpec((None, seq, hps * hd), lambda b, h, i: (slab, b, h), pipeline_mode=pl.Buffered(1)),
        ],
        out_specs=pl.BlockSpec((blk, hps * hd), lambda b, h, i: (b * nblk + i, h)),
        out_shape=jax.ShapeDtypeStruct((rows, width), BF16),
        scratch_shapes=[
            pltpu.VMEM((hps, seq, 2 * hd), BF16), pltpu.VMEM((hps, seq, 2 * hd), BF16),
            pltpu.VMEM((hps, LANES, hd), F32),
            pltpu.VMEM((hps, blk, hd), F32), pltpu.VMEM((hps, blk, 2 * hd), F32),
        ],
        compiler_params=_cparams(3, 52),
        name="moba_prompt",
    )(q, k, v)


def _mlstm_prompt_kernel(q_ref, k_ref, v_ref, og_ref, g_ref, gb_ref, ng_ref,
                         h_ref, c_ref, n_ref, m_ref, *, nb, nh, dk, dv):
    lc = q_ref.shape[1]

    @pl.when(pl.program_id(0) == 0)
    def _():
        c_ref[...] = jnp.zeros_like(c_ref)
        n_ref[...] = jnp.zeros_like(n_ref)
        m_ref[...] = jnp.zeros_like(m_ref)

    tt = lax.broadcasted_iota(jnp.int32, (lc, lc), 0)
    ss = lax.broadcasted_iota(jnp.int32, (lc, lc), 1)
    causal = ss <= tt
    eye = ss == tt
    nt = (((1,), (1,)), ((), ()))
    tn = (((0,), (0,)), ((), ()))
    gb = gb_ref[...]
    for b in range(nb):
        gates = g_ref[b] + gb
        for h in range(nh):
            bh = b * nh + h
            i_col = gates[:, h:h + 1]
            fpre = gates[:, nh + h:nh + h + 1]
            f_col = jnp.minimum(fpre, 0.0) - jnp.log1p(jnp.exp(-jnp.abs(fpre)))
            f_row = jnp.sum(jnp.where(eye, f_col, 0.0), axis=0, keepdims=True)
            i_row = jnp.sum(jnp.where(eye, i_col, 0.0), axis=0, keepdims=True)
            b_col = jnp.sum(jnp.where(causal, f_row, 0.0), axis=1, keepdims=True)
            b_row = jnp.sum(jnp.where(ss >= tt, f_col, 0.0), axis=0, keepdims=True)
            m_prev = m_ref[bh][:, 0:1]
            d = jnp.where(causal, b_col - b_row + i_row, NEG_BIG)
            inter = b_col + m_prev
            m_t = jnp.maximum(jnp.max(d, axis=1, keepdims=True), inter)
            w = jnp.exp(d - m_t)
            q = q_ref[b, :, h * dk:(h + 1) * dk]
            k = k_ref[b, :, h * dk:(h + 1) * dk] * (dk ** -0.5)
            vb = v_ref[b, :, h * dv:(h + 1) * dv].astype(BF16)
            qb = q.astype(BF16)
            s = lax.dot_general(qb, k.astype(BF16), nt, preferred_element_type=F32) * w
            wi = jnp.exp(inter - m_t)
            cst = c_ref[bh]
            nrow = n_ref[bh]
            num = wi * jnp.dot(qb, cst.astype(BF16), preferred_element_type=F32) + jnp.dot(
                s.astype(BF16), vb, preferred_element_type=F32)
            den = wi * jnp.sum(q * nrow, axis=1, keepdims=True) + jnp.sum(s, axis=1, keepdims=True)
            hh = num / jnp.maximum(jnp.abs(den), jnp.exp(-m_t))
            b_last = b_col[lc - 1:lc, :]
            g_col = b_last - b_col + i_col
            m_new = jnp.maximum(b_last + m_prev, jnp.max(g_col, axis=0, keepdims=True))
            wc = jnp.exp(b_last + m_prev - m_new)
            kw = k * jnp.exp(g_col - m_new)
            c_ref[bh] = wc * cst + lax.dot_general(kw.astype(BF16), vb, tn, preferred_element_type=F32)
            n_ref[bh] = wc * nrow + jnp.sum(kw, axis=0, keepdims=True)
            m_ref[bh] = jnp.broadcast_to(m_new, (1, LANES))
            hn = hh * lax.rsqrt(jnp.mean(hh * hh, axis=1, keepdims=True) + LN_EPS)
            hn = hn * ng_ref[:, h * dv:(h + 1) * dv] * _sigmoid(og_ref[b, :, h * dv:(h + 1) * dv])
            h_ref[b, :, h * dv:(h + 1) * dv] = hn.astype(h_ref.dtype)


def _mlstm_prompt(zm, gates, gate_bias, norm_g, bsz):
    rows, _ = zm.shape
    seq = rows // bsz
    nh = B_HEADS
    v_w = norm_g.shape[1]
    qk_w = v_w // 2
    dk, dv = qk_w // nh, v_w // nh
    lc = MLSTM_CHUNK
    z3 = zm.reshape(bsz, seq, zm.shape[1])
    gw = gates.shape[1]
    g3 = gates.reshape(bsz, seq, gw)
    nbh = bsz * nh
    full = lambda shape: pl.BlockSpec(shape, lambda c: (0,) * len(shape))
    return pl.pallas_call(
        functools.partial(_mlstm_prompt_kernel, nb=bsz, nh=nh, dk=dk, dv=dv),
        grid=(seq // lc,),
        in_specs=[
            pl.BlockSpec((bsz, lc, qk_w), lambda c: (0, c, 0)),
            pl.BlockSpec((bsz, lc, qk_w), lambda c: (0, c, 1)),
            pl.BlockSpec((bsz, lc, v_w), lambda c: (0, c, 1)),
            pl.BlockSpec((bsz, lc, v_w), lambda c: (0, c, 2)),
            pl.BlockSpec((bsz, lc, gw), lambda c: (0, c, 0)),
            full((1, gw)), full((1, v_w)),
        ],
        out_specs=[
            pl.BlockSpec((bsz, lc, v_w), lambda c: (0, c, 0)),
            full((nbh, dk, dv)), full((nbh, 1, dk)), full((nbh, 1, LANES)),
        ],
        out_shape=[
            jax.ShapeDtypeStruct((bsz, seq, v_w), BF16),
            jax.ShapeDtypeStruct((nbh, dk, dv), F32),
            jax.ShapeDtypeStruct((nbh, 1, dk), F32),
            jax.ShapeDtypeStruct((nbh, 1, LANES), F32),
        ],
        compiler_params=_cparams(1, 32),
        name="mlstm_prompt",
    )(z3, z3, z3, z3, g3, gate_bias, norm_g)


def _rglru_gates(xconv, wa_ref, ba, wx_ref, bx, lam):
    bd = wa_ref.shape[1]
    r_parts, i_parts = [], []
    for n in range(wa_ref.shape[0]):
        xb = xconv[:, n * bd:(n + 1) * bd].astype(BF16)
        r_parts.append(jnp.dot(xb, wa_ref[n].astype(BF16), preferred_element_type=F32))
        i_parts.append(jnp.dot(xb, wx_ref[n].astype(BF16), preferred_element_type=F32))
    r = _sigmoid(jnp.concatenate(r_parts, axis=1) + ba)
    ig = _sigmoid(jnp.concatenate(i_parts, axis=1) + bx)
    log_a = (-RG_C) * r * _softplus(-lam)
    a = jnp.exp(log_a)
    mult = jnp.sqrt(1.0 - a * a)
    return a, mult * ig * xconv


def _rglru_prompt_kernel(gate_ref, xr_ref, cw_ref, cb_ref, wa_ref, ba_ref, wx_ref, bx_ref, lam_ref,
                         y_ref, hl_ref, buf_ref, xext_ref, a_ref, b_ref, hs_ref, hc_ref):
    t = pl.program_id(1)
    tt = xr_ref.shape[0]
    pad = 8
    nbuf = CONV_WIDTH - 1

    @pl.when(t == 0)
    def _():
        xext_ref[0:pad, :] = jnp.zeros((pad, xext_ref.shape[1]), F32)
        hc_ref[...] = jnp.zeros_like(hc_ref)

    @pl.when(t > 0)
    def _():
        xext_ref[0:pad, :] = xext_ref[tt:tt + pad, :]

    xr = xr_ref[...]
    xext_ref[pad:pad + tt, :] = xr
    xconv = cb_ref[...] + xext_ref[pad - nbuf:pad - nbuf + tt, :] * cw_ref[0:1, :]
    for j in range(1, CONV_WIDTH):
        xconv = xconv + xext_ref[pad - nbuf + j:pad - nbuf + j + tt, :] * cw_ref[j:j + 1, :]
    a, bterm = _rglru_gates(xconv, wa_ref, ba_ref[...], wx_ref, bx_ref[...], lam_ref[...])
    a_ref[...] = a
    b_ref[...] = bterm

    def step(i, h):
        h = a_ref[pl.ds(i, 1), :] * h + b_ref[pl.ds(i, 1), :]
        hs_ref[pl.ds(i, 1), :] = h
        return h

    h_last = lax.fori_loop(0, tt, step, hc_ref[...], unroll=8)
    hc_ref[...] = h_last
    y_ref[...] = (_gelu_tanh(gate_ref[...]) * hs_ref[...]).astype(y_ref.dtype)
    hl_ref[...] = h_last
    buf_ref[...] = xr[tt - nbuf:tt, :]


def _rglru_prompt(zr, layer, conv_w, conv_b, w_a, b_a, w_x, b_x, lam, bsz, tt):
    rows, w2 = zr.shape
    width = w2 // 2
    seq = rows // bsz
    nt = seq // tt
    nblk, bd = w_a.shape[1], w_a.shape[2]
    vec = lambda a: a.reshape(a.shape[0], 1, width)
    vspec = pl.BlockSpec((None, 1, width), lambda b, t: (layer, 0, 0))
    wspec = pl.BlockSpec((None, nblk, bd, bd), lambda b, t: (layer, 0, 0, 0))
    nbuf = CONV_WIDTH - 1
    return pl.pallas_call(
        _rglru_prompt_kernel,
        grid=(bsz, nt),
        in_specs=[
            pl.BlockSpec((tt, width), lambda b, t: (b * nt + t, 0)),
            pl.BlockSpec((tt, width), lambda b, t: (b * nt + t, 1)),
            pl.BlockSpec((None, CONV_WIDTH, width), lambda b, t: (layer, 0, 0)),
            vspec, wspec, vspec, wspec, vspec, vspec,
        ],
        out_specs=[
            pl.BlockSpec((tt, width), lambda b, t: (b * nt + t, 0)),
            pl.BlockSpec((None, 1, width), lambda b, t: (b, 0, 0)),
            pl.BlockSpec((None, nbuf, width), lambda b, t: (b, 0, 0)),
        ],
        out_shape=[
            jax.ShapeDtypeStruct((rows, width), BF16),
            jax.ShapeDtypeStruct((bsz, 1, width), F32),
            jax.ShapeDtypeStruct((bsz, nbuf, width), F32),
        ],
        scratch_shapes=[
            pltpu.VMEM((tt + 8, width), F32), pltpu.VMEM((tt, width), F32),
            pltpu.VMEM((tt, width), F32), pltpu.VMEM((tt, width), F32), pltpu.VMEM((1, width), F32),
        ],
        compiler_params=_cparams(2, 48),
        name="rglru_prompt",
    )(zr, zr, conv_w, vec(conv_b), w_a, vec(b_a), w_x, vec(b_x), vec(lam))


def _moba_gate_kernel(tbl_ref, q_ref, *refs, nheads, nsteps, pages_per_step, pages_per_blk):
    b = pl.program_id(0)
    st = pl.program_id(1)
    hd = A_HEAD_DIM
    k_refs = refs[:pages_per_step]
    sel_ref, ksum_ref = refs[pages_per_step:]
    page = k_refs[0].shape[0]
    blks_per_step = pages_per_step // pages_per_blk
    for i in range(blks_per_step):
        tot = jnp.sum(k_refs[i * pages_per_blk][...], axis=0)
        for u in range(1, pages_per_blk):
            tot = tot + jnp.sum(k_refs[i * pages_per_blk + u][...], axis=0)
        ksum_ref[st * blks_per_step + i] = tot

    @pl.when(st == nsteps - 1)
    def _():
        nblk = ksum_ref.shape[0]
        inv = 1.0 / (pages_per_blk * page)
        rowi = lax.broadcasted_iota(jnp.int32, (nblk, 1), 0).astype(F32)
        out_r = lax.broadcasted_iota(jnp.int32, sel_ref.shape, 0)
        out_c = lax.broadcasted_iota(jnp.int32, sel_ref.shape, 1)
        out = jnp.zeros(sel_ref.shape, jnp.int32)
        qrow = q_ref[pl.ds(b, 1), :]
        for h in range(nheads):
            km = ksum_ref[:, h, :] * inv
            g = jnp.sum(km * qrow[:, h * hd:(h + 1) * hd], axis=1, keepdims=True)
            for i in range(MOBA_TOPK):
                mx = jnp.max(g, axis=0, keepdims=True)
                idx = jnp.min(jnp.where(g == mx, rowi, 1e9), axis=0, keepdims=True)
                out = jnp.where((out_r == h) & (out_c == i), idx.astype(jnp.int32), out)
                g = jnp.where(rowi == idx, -jnp.inf, g)
        sel_ref[...] = out


GATE_PAGES_PER_STEP = 8


def _moba_sample_gate(q, cache_k, layer, tbl_flat, nreq, npages):
    page, nheads, hd = cache_k.shape[2:]
    ppb = MOBA_BLOCK // page
    nblk = npages // ppb
    pps = GATE_PAGES_PER_STEP
    nsteps = npages // pps

    def page_spec(u):
        return pl.BlockSpec((None, None, page, nheads, hd),
                            lambda b, s, tbl: (layer, tbl[b * npages + s * pps + u], 0, 0, 0))

    grid_spec = pltpu.PrefetchScalarGridSpec(
        num_scalar_prefetch=1,
        grid=(nreq, nsteps),
        in_specs=[pl.BlockSpec(q.shape, lambda b, s, tbl: (0, 0))] + [page_spec(u) for u in range(pps)],
        out_specs=pl.BlockSpec((None, nheads, LANES), lambda b, s, tbl: (b, 0, 0)),
        scratch_shapes=[pltpu.VMEM((nblk, nheads, hd), F32)],
    )
    return pl.pallas_call(
        functools.partial(_moba_gate_kernel, nheads=nheads, nsteps=nsteps, pages_per_step=pps,
                          pages_per_blk=ppb),
        grid_spec=grid_spec,
        out_shape=jax.ShapeDtypeStruct((nreq, nheads, LANES), jnp.int32),
        compiler_params=_cparams(2, 32),
        name="moba_sample_gate",
    )(tbl_flat, q, *([cache_k] * pps))


def _moba_sample_attn_kernel(tbl_ref, sel_ref, q_ref, kn_ref, vn_ref, *refs, npg, scale):
    k_refs = refs[:npg]
    v_refs = refs[npg:2 * npg]
    o_ref = refs[2 * npg]
    b = pl.program_id(0)
    h = pl.program_id(1)
    nheads = q_ref.shape[1]
    is_h = lax.broadcasted_iota(jnp.int32, (1, nheads, 1), 1) == h
    q_m = jnp.where(is_h, q_ref[b][None], 0.0)

    def score(kp):
        part = jnp.sum(kp * q_m, axis=2, keepdims=True)
        return jnp.sum(part, axis=1, keepdims=True) * scale

    s_own = score(kn_ref[b][None])
    ss = [score(k_ref[...]) for k_ref in k_refs]
    m = s_own
    for s in ss:
        m = jnp.maximum(m, jnp.max(s, axis=0, keepdims=True))
    l = jnp.exp(s_own - m)
    acc = l * vn_ref[b][None]
    for s, v_ref in zip(ss, v_refs):
        p = jnp.exp(s - m)
        l = l + jnp.sum(p, axis=0, keepdims=True)
        acc = acc + jnp.sum(p * v_ref[...], axis=0, keepdims=True)
    out = jnp.sum(jnp.where(is_h, acc / l, 0.0), axis=1)
    o_ref[...] = out.astype(o_ref.dtype)


def _moba_sample_attn(q, k_new, v_new, cache_k, cache_v, layer, tbl_flat, sel_flat, nreq, npages):
    page, nheads, hd = cache_k.shape[2:]
    ppb = MOBA_BLOCK // page
    npg = MOBA_TOPK * ppb

    def page_spec(j):
        def page_map(b, h, tbl, sel):
            blk = sel[(b * nheads + h) * LANES + j // ppb]
            return (layer, tbl[b * npages + blk * ppb + j % ppb], 0, 0, 0)
        return pl.BlockSpec((None, None, page, nheads, hd), page_map)

    head = pl.BlockSpec(q.shape, lambda b, h, tbl, sel: (0, 0, 0))
    pages = [page_spec(j) for j in range(npg)]
    grid_spec = pltpu.PrefetchScalarGridSpec(
        num_scalar_prefetch=2,
        grid=(nreq, nheads),
        in_specs=[head, head, head] + pages + pages,
        out_specs=pl.BlockSpec((None, 1, hd), lambda b, h, tbl, sel: (b, 0, h)),
    )
    return pl.pallas_call(
        functools.partial(_moba_sample_attn_kernel, npg=npg, scale=hd ** -0.5),
        grid_spec=grid_spec,
        out_shape=jax.ShapeDtypeStruct((nreq, 1, nheads * hd), F32),
        compiler_params=_cparams(2, 32),
        name="moba_sample_attn",
    )(tbl_flat, sel_flat, q, k_new, v_new, *([cache_k] * npg), *([cache_v] * npg))


def _mlstm_sample_kernel(q_ref, k_ref, v_ref, og_ref, g_ref, gb_ref, ng_ref, c0_ref, n0_ref, m0_ref,
                         h_ref, c_ref, n_ref, m_ref, *, nh, dk, dv):
    b = pl.program_id(0)
    rr = lax.broadcasted_iota(jnp.int32, (dk, dk), 0)
    cc = lax.broadcasted_iota(jnp.int32, (dk, dk), 1)
    eye = rr == cc
    gates = g_ref[pl.ds(b, 1), :] + gb_ref[...]
    q_all = q_ref[pl.ds(b, 1), :]
    k_all = k_ref[pl.ds(b, 1), :]
    v_all = v_ref[pl.ds(b, 1), :]
    og_all = og_ref[pl.ds(b, 1), :]
    for h in range(nh):
        ii = gates[:, h:h + 1]
        fpre = gates[:, nh + h:nh + h + 1]
        ff = jnp.minimum(fpre, 0.0) - jnp.log1p(jnp.exp(-jnp.abs(fpre)))
        m_prev = m0_ref[h][:, 0:1]
        q = q_all[:, h * dk:(h + 1) * dk]
        k = k_all[:, h * dk:(h + 1) * dk] * (dk ** -0.5)
        v = v_all[:, h * dv:(h + 1) * dv]
        inter = ff + m_prev
        m_t = jnp.maximum(ii, inter)
        s = jnp.sum(q * k, axis=1, keepdims=True) * jnp.exp(ii - m_t)
        wi = jnp.exp(inter - m_t)
        q_col = jnp.sum(jnp.where(eye, q, 0.0), axis=1, keepdims=True)
        k_col = jnp.sum(jnp.where(eye, k, 0.0), axis=1, keepdims=True)
        cst = c0_ref[h]
        nrow = n0_ref[h]
        num = wi * jnp.sum(q_col * cst, axis=0, keepdims=True) + s * v
        den = wi * jnp.sum(q * nrow, axis=1, keepdims=True) + s
        hh = num / jnp.maximum(jnp.abs(den), jnp.exp(-m_t))
        m_new = m_t
        wc = jnp.exp(inter - m_new)
        wg = jnp.exp(ii - m_new)
        c_ref[h] = wc * cst + (wg * k_col) * v
        n_ref[h] = wc * nrow + wg * k
        m_ref[h] = jnp.broadcast_to(m_new, (1, LANES))
        hn = hh * lax.rsqrt(jnp.mean(hh * hh, axis=1, keepdims=True) + LN_EPS)
        hn = hn * ng_ref[:, h * dv:(h + 1) * dv] * _sigmoid(og_all[:, h * dv:(h + 1) * dv])
        h_ref[:, h * dv:(h + 1) * dv] = hn.astype(h_ref.dtype)


def _mlstm_sample(zm, gates, gate_bias, norm_g, c0, n0, m0, layer, nreq):
    nh = B_HEADS
    v_w = norm_g.shape[1]
    qk_w = v_w // 2
    dk, dv = qk_w // nh, v_w // nh
    rows = zm.shape[0]
    full = lambda shape: pl.BlockSpec(shape, lambda b: (0,) * len(shape))
    return pl.pallas_call(
        functools.partial(_mlstm_sample_kernel, nh=nh, dk=dk, dv=dv),
        grid=(nreq,),
        in_specs=[
            pl.BlockSpec((rows, qk_w), lambda b: (0, 0)),
            pl.BlockSpec((rows, qk_w), lambda b: (0, 1)),
            pl.BlockSpec((rows, v_w), lambda b: (0, 1)),
            pl.BlockSpec((rows, v_w), lambda b: (0, 2)),
            full(gates.shape), full(gate_bias.shape), full((1, v_w)),
            pl.BlockSpec((None, None, nh, dk, dv), lambda b: (layer, b, 0, 0, 0)),
            pl.BlockSpec((None, None, nh, 1, dk), lambda b: (layer, b, 0, 0, 0)),
            pl.BlockSpec((None, None, nh, 1, LANES), lambda b: (layer, b, 0, 0, 0)),
        ],
        out_specs=[
            pl.BlockSpec((None, 1, v_w), lambda b: (b, 0, 0)),
            pl.BlockSpec((None, nh, dk, dv), lambda b: (b, 0, 0, 0)),
            pl.BlockSpec((None, nh, 1, dk), lambda b: (b, 0, 0, 0)),
            pl.BlockSpec((None, nh, 1, LANES), lambda b: (b, 0, 0, 0)),
        ],
        out_shape=[
            jax.ShapeDtypeStruct((nreq, 1, v_w), F32),
            jax.ShapeDtypeStruct((nreq, nh, dk, dv), F32),
            jax.ShapeDtypeStruct((nreq, nh, 1, dk), F32),
            jax.ShapeDtypeStruct((nreq, nh, 1, LANES), F32),
        ],
        compiler_params=_cparams(1, 32),
        name="mlstm_sample",
    )(zm, zm, zm, zm, gates, gate_bias, norm_g, c0, n0, m0)


def _rglru_sample_kernel(gate_ref, xr_ref, buf_ref, h0_ref, cw_ref, cb_ref, wa_ref, ba_ref, wx_ref,
                         bx_ref, lam_ref, y_ref, h_ref, nbuf_ref):
    nbuf = CONV_WIDTH - 1
    xr = xr_ref[...]
    xconv = cb_ref[...] + buf_ref[0] * cw_ref[0:1, :]
    for j in range(1, nbuf):
        xconv = xconv + buf_ref[j] * cw_ref[j:j + 1, :]
    xconv = xconv + xr * cw_ref[nbuf:nbuf + 1, :]
    a, bterm = _rglru_gates(xconv, wa_ref, ba_ref[...], wx_ref, bx_ref[...], lam_ref[...])
    h = a * h0_ref[...] + bterm
    h_ref[...] = h
    y = _gelu_tanh(gate_ref[...]) * h
    pad_rows = y_ref.shape[0] - y.shape[0]
    y_ref[...] = jnp.concatenate([y, jnp.zeros((pad_rows, y.shape[1]), F32)], axis=0).astype(y_ref.dtype)
    for j in range(nbuf - 1):
        nbuf_ref[j] = buf_ref[j + 1]
    nbuf_ref[nbuf - 1] = xr


def _rglru_sample(zr, layer, buf_t, h0, conv_w, conv_b, w_a, b_a, w_x, b_x, lam, nreq):
    rows, w2 = zr.shape
    width = w2 // 2
    nblk, bd = w_a.shape[1], w_a.shape[2]
    nbuf = CONV_WIDTH - 1
    vec = lambda a: a.reshape(a.shape[0], 1, width)
    vspec = pl.BlockSpec((None, 1, width), lambda i: (layer, 0, 0))
    wspec = pl.BlockSpec((None, nblk, bd, bd), lambda i: (layer, 0, 0, 0))
    return pl.pallas_call(
        _rglru_sample_kernel,
        grid=(1,),
        in_specs=[
            pl.BlockSpec((nreq, width), lambda i: (0, 0)),
            pl.BlockSpec((nreq, width), lambda i: (0, 1)),
            pl.BlockSpec((None, nbuf, nreq, width), lambda i: (layer, 0, 0, 0)),
            pl.BlockSpec((None, nreq, width), lambda i: (layer, 0, 0)),
            pl.BlockSpec((None, CONV_WIDTH, width), lambda i: (layer, 0, 0)),
            vspec, wspec, vspec, wspec, vspec, vspec,
        ],
        out_specs=[
            pl.BlockSpec((rows, width), lambda i: (0, 0)),
            pl.BlockSpec((nreq, width), lambda i: (0, 0)),
            pl.BlockSpec((nbuf, nreq, width), lambda i: (0, 0, 0)),
        ],
        out_shape=[
            jax.ShapeDtypeStruct((rows, width), BF16),
            jax.ShapeDtypeStruct((nreq, width), F32),
            jax.ShapeDtypeStruct((nbuf, nreq, width), F32),
        ],
        compiler_params=_cparams(1, 32),
        name="rglru_sample",
    )(zr, zr, buf_t, h0, conv_w, vec(conv_b), w_a, vec(b_a), w_x, vec(b_x), vec(lam))


def _rope_tables(pos):
    half = A_HEAD_DIM // 2
    inv = ROPE_THETA ** (-jnp.arange(half, dtype=F32) / half)
    ang = pos.astype(F32)[:, None] * inv[None, :]
    cos, sin = jnp.cos(ang), jnp.sin(ang)
    return jnp.concatenate([cos, cos], axis=1), jnp.concatenate([-sin, sin], axis=1)


def kernel(x_prompt, x_sample, cache_k, cache_v, state_mlstm_c, state_mlstm_n, state_mlstm_m,
           state_rglru_h, state_conv, page_table, c_prompt, c_sample, w_ada, b_ada, ln_g, ln_b,
           w_ffn1_in, w_ffn1_out, w_ffn2_in, w_ffn2_out, w_in_ab, b_igate, b_fgate, mlstm_norm_g,
           w_out_ab, w_in_rg, conv_w, conv_b, w_rg_a, b_rg_a, w_rg_x, b_rg_x, lru_lambda, w_out_rg):
    bsz, seq, d = x_prompt.shape
    nreq = x_sample.shape[0]
    depth = w_ada.shape[0]
    n_even = w_in_ab.shape[0]
    a_w = cache_k.shape[3] * cache_k.shape[4]
    v_w = mlstm_norm_g.shape[1]
    qk_w = v_w // 2
    m_w = 2 * qk_w + 2 * v_w
    nh_b = B_HEADS
    past_len = page_table.shape[1] * cache_k.shape[2]
    npages = page_table.shape[1]
    rows_p = bsz * seq
    rs = SAMPLE_ROWS

    tm = 1024
    tn = 512
    tm_out = 256
    tk_cast = 512
    tt_rg = 256

    c_all = jnp.concatenate([c_prompt, c_sample, jnp.zeros((rs - bsz - nreq, d), F32)], axis=0)
    mods = _mods(c_all, w_ada, b_ada).reshape(depth, rs, 9, d)
    mods_p = jnp.transpose(mods[:, :bsz], (0, 2, 1, 3)).reshape(depth * 9 * bsz, 1, d)
    mods_s = jnp.transpose(mods[:, bsz:bsz + nreq], (0, 2, 1, 3))
    mods_s = jnp.pad(mods_s, ((0, 0), (0, 0), (0, rs - nreq), (0, 0))).reshape(depth * 9, rs, d)

    def mod_spec_p(l, s, j):
        base = ((l * 3 + s) * 3 + j) * bsz

        def make(tile_rows):
            tiles_per_batch = seq // tile_rows
            return pl.BlockSpec((None, 1, d), lambda i: (base + i // tiles_per_batch, 0, 0))
        return make

    def mod_spec_s(l, s, j):
        row = (l * 3 + s) * 3 + j

        def make(tile_rows):
            return pl.BlockSpec((None, rs, d), lambda i: (row, 0, 0))
        return make

    ln_g3 = ln_g.reshape(depth * 3, 1, d)
    ln_b3 = ln_b.reshape(depth * 3, 1, d)

    cos_p, sin_p = _rope_tables(jnp.arange(seq, dtype=jnp.int32))
    cos_s, sin_s = _rope_tables(jnp.full((rs,), past_len, dtype=jnp.int32))

    tbl_flat = page_table.reshape(-1).astype(jnp.int32)
    n0_s = state_mlstm_n[:, :, :, None, :]
    m0_s = jnp.broadcast_to(state_mlstm_m[:, :, :, None, None], state_mlstm_m.shape + (1, LANES))
    conv_t = jnp.transpose(state_conv, (0, 2, 1, 3))
    gate_bias = jnp.concatenate([b_igate, b_fgate], axis=1)
    w_ab_t = jnp.swapaxes(w_in_ab, 1, 2)

    xp = x_prompt.reshape(rows_p, d)
    xs = jnp.pad(x_sample.reshape(nreq, d), ((0, rs - nreq), (0, 0)))
    up = _modulate(xp, mod_spec_p(0, 0, 1), mod_spec_p(0, 0, 0), mods_p, tm)
    us = _modulate(xs, mod_spec_s(0, 0, 1), mod_spec_s(0, 0, 0), mods_s, rs)

    op = dict(k=[], v=[], c=[], n=[], m=[], h=[], buf=[])
    os_ = dict(k=[], v=[], c=[], n=[], m=[], h=[], buf=[])
    rope = (cos_p, sin_p, cos_s, sin_s)
    k_st = v_st = None
    pad_rows = lambda t, w: jnp.pad(t.reshape(nreq, w), ((0, rs - nreq), (0, 0))).astype(BF16)
    heads3 = lambda t: t.reshape(rs, -1, A_HEAD_DIM)

    for l in range(depth):
        wb_mix = _cast_bf16(w_out_ab if l % 2 == 0 else w_out_rg, l // 2, tk_cast)

        def post(parts_p, parts_s, wb, s, res_w, nxt):
            nl, ns = nxt
            pick = lambda spec: [spec(l, s, 2), spec(nl, ns, 1), spec(nl, ns, 0)]
            xo, uo, xso, uso = _out_ln(parts_p, parts_s, wb, xp, xs, mods_p, mods_s, pick(mod_spec_p),
                                       pick(mod_spec_s), ln_g3, ln_b3, l * 3 + s, res_w, tm_out)
            return (xo, uo), (xso, uso)

        act_p, act_s, wb_ffn1 = _swiglu_in(up, us, w_ffn1_in, w_ffn1_out, l, tm, tn)
        (xp, up), (xs, us) = post([act_p], [act_s], wb_ffn1, 0, FFN_RES, (l, 1))

        if l % 2 == 0:
            e = l // 2
            q_p, q_s = _proj(up, us, w_ab_t, e, 0, a_w, tm, tn, F32, rope=rope, w_t=True, name="proj_q")
            k_st = _proj(up, us, w_ab_t, e, a_w, a_w, tm, tn, F32, rope=rope, w_t=True,
                         stack=(n_even, e, k_st), name="proj_k")
            v_st = _proj(up, us, w_ab_t, e, 2 * a_w, a_w, tm, tn, F32, w_t=True,
                         stack=(n_even, e, v_st), name="proj_v")
            k_s, v_s = k_st[1][e], v_st[1][e]
            zm_p, zm_s = _proj(up, us, w_ab_t, e, 3 * a_w, m_w, tm, tn, F32, w_t=True, name="proj_mlstm")
            n_gates = 2 * nh_b
            g_p, g_s = _proj(up, us, w_ab_t, e, 3 * a_w + m_w, n_gates, tm, n_gates, F32, w_t=True,
                             name="proj_gates")
            gb = gate_bias[e:e + 1]
            ng = mlstm_norm_g[e:e + 1]

            oa = _moba_prompt(q_p, k_st[0], v_st[0], e, bsz)
            hm, c1, n1, m1 = _mlstm_prompt(zm_p, g_p, gb, ng, bsz)
            mix_p = [oa, hm.reshape(rows_p, v_w)]
            op["c"].append(c1.reshape(bsz, nh_b, c1.shape[1], c1.shape[2]))
            op["n"].append(n1.reshape(bsz, nh_b, -1))
            op["m"].append(m1[:, 0, 0].reshape(bsz, nh_b))

            sel = _moba_sample_gate(q_s, cache_k, e, tbl_flat, nreq, npages)
            oa = _moba_sample_attn(heads3(q_s), heads3(k_s), heads3(v_s), cache_k, cache_v, e,
                                   tbl_flat, sel.reshape(-1), nreq, npages)
            hm, c1, n1, m1 = _mlstm_sample(zm_s, g_s, gb, ng, state_mlstm_c, n0_s, m0_s, e, nreq)
            mix_s = [pad_rows(oa, a_w), pad_rows(hm, v_w)]
            os_["c"].append(c1)
            os_["n"].append(n1.reshape(nreq, nh_b, -1))
            os_["m"].append(m1[:, :, 0, 0])
        else:
            od = l // 2
            zr_p, zr_s = _proj(up, us, w_in_rg, od, 0, w_in_rg.shape[2], tm, tn, F32, name="proj_rg")
            y_p, h1, nb = _rglru_prompt(zr_p, od, conv_w, conv_b, w_rg_a, b_rg_a, w_rg_x, b_rg_x,
                                        lru_lambda, bsz, tt_rg)
            op["h"].append(h1.reshape(bsz, -1))
            op["buf"].append(nb)
            y_s, h1, nb = _rglru_sample(zr_s, od, conv_t, state_rglru_h, conv_w, conv_b, w_rg_a,
                                        b_rg_a, w_rg_x, b_rg_x, lru_lambda, nreq)
            os_["h"].append(h1)
            os_["buf"].append(jnp.transpose(nb, (1, 0, 2)))
            mix_p, mix_s = [y_p], [y_s]
        (xp, up), (xs, us) = post(mix_p, mix_s, wb_mix, 1, 1.0, (l, 2))

        act_p, act_s, wb_ffn2 = _swiglu_in(up, us, w_ffn2_in, w_ffn2_out, l, tm, tn)
        (xp, up), (xs, us) = post([act_p], [act_s], wb_ffn2, 2, FFN_RES, (min(l + 1, depth - 1), 0))

    st = jnp.stack
    kv_p = lambda t: t.reshape(n_even, bsz, seq, -1, A_HEAD_DIM)
    kv_s = lambda t: t[:, :nreq].reshape(n_even, nreq, 1, -1, A_HEAD_DIM)
    return (xp.reshape(bsz, seq, d), xs[:nreq].reshape(nreq, 1, d),
            kv_p(k_st[0]), kv_p(v_st[0]), kv_s(k_st[1]), kv_s(v_st[1]),
            st(op["c"]), st(op["n"]), st(op["m"]), st(os_["c"]), st(os_["n"]), st(os_["m"]),
            st(op["h"]), st(op["buf"]), st(os_["h"]), st(os_["buf"]))
```

```python
import functools

import numpy as np
import jax
import jax.numpy as jnp
from jax import lax
from jax.experimental import pallas as pl
from jax.experimental.pallas import tpu as pltpu

F32 = jnp.float32
BF16 = jnp.bfloat16

DEPTH = 4
MOBA_BLOCK = 256
MOBA_TOPK = 3
MOBA_GROUP = 4
MOBA_HEADS_PER_STEP = 4
MLSTM_CHUNKS_PER_STEP = 1
ROPE_THETA = 10000.0
A_HEAD_DIM = 128
B_HEADS = 4
MLSTM_CHUNK = 64
RG_BLOCKS = 16
CONV_WIDTH = 4
RG_C = 8.0
FFN_RES = 0.5
ALPHA = (2.0 * DEPTH) ** 0.25
LN_EPS = 1e-5

LANES = 128
SAMPLE_ROWS = 16
NEG_BIG = -1e30
MIB = 1024 * 1024


def _cparams(n_axes, vmem_mib):
    return pltpu.CompilerParams(
        dimension_semantics=("arbitrary",) * n_axes,
        vmem_limit_bytes=int(vmem_mib * MIB),
    )


def _sigmoid(x):
    return 0.5 * jnp.tanh(0.5 * x) + 0.5


def _softplus(x):
    return jnp.maximum(x, 0.0) + jnp.log1p(jnp.exp(-jnp.abs(x)))


def _gelu_tanh(x):
    return 0.5 * x * (1.0 + jnp.tanh(0.7978845608028654 * (x + 0.044715 * x * x * x)))


def _mods_kernel(c_ref, w_ref, b_ref, o_ref):
    c = c_ref[...]
    s = (c * _sigmoid(c)).astype(BF16)
    o_ref[...] = jnp.dot(s, w_ref[...].astype(BF16), preferred_element_type=F32) + b_ref[...]


def _mods(c_all, w_ada, b_ada):
    depth, k, n = w_ada.shape
    rows = c_all.shape[0]
    tn = 1024 if n % 1024 == 0 else n
    return pl.pallas_call(
        _mods_kernel,
        grid=(depth, n // tn),
        in_specs=[
            pl.BlockSpec((rows, k), lambda l, j: (0, 0)),
            pl.BlockSpec((None, k, tn), lambda l, j: (l, 0, j)),
            pl.BlockSpec((None, 1, tn), lambda l, j: (l, 0, j)),
        ],
        out_specs=pl.BlockSpec((None, rows, tn), lambda l, j: (l, 0, j)),
        out_shape=jax.ShapeDtypeStruct((depth, rows, n), F32),
        compiler_params=_cparams(2, 48),
        name="adaln_mods",
    )(c_all, w_ada, b_ada.reshape(depth, 1, n))


def _modulate_kernel(x_ref, sc_ref, sh_ref, u_ref):
    u_ref[...] = (x_ref[...] * (1.0 + sc_ref[...]) + sh_ref[...]).astype(u_ref.dtype)


def _modulate(x, sc_spec, sh_spec, mods, tm):
    m, d = x.shape
    return pl.pallas_call(
        _modulate_kernel,
        grid=(m // tm,),
        in_specs=[pl.BlockSpec((tm, d), lambda i: (i, 0)), sc_spec(tm), sh_spec(tm)],
        out_specs=pl.BlockSpec((tm, d), lambda i: (i, 0)),
        out_shape=jax.ShapeDtypeStruct((m, d), BF16),
        compiler_params=_cparams(1, 32),
        name="modulate0",
    )(x, mods, mods)


def _rope_store(z, cos, sin, o_ref):
    for j in range(z.shape[1] // A_HEAD_DIM):
        zj = z[:, j * A_HEAD_DIM:(j + 1) * A_HEAD_DIM]
        o_ref[:, j * A_HEAD_DIM:(j + 1) * A_HEAD_DIM] = (
            zj * cos + pltpu.roll(zj, A_HEAD_DIM // 2, 1) * sin).astype(o_ref.dtype)


def _proj_kernel(*refs, rope, n_alias, w_mode):
    refs = refs[:len(refs) - 3 - n_alias] + refs[len(refs) - 3:]
    if rope:
        x_ref, xs_ref, w_ref, cos_ref, sin_ref, coss_ref, sins_ref, o_ref, os_ref, wb_ref = refs
    else:
        x_ref, xs_ref, w_ref, o_ref, os_ref, wb_ref = refs

    def mm(x):
        if w_mode == "nk_contract_last":
            return lax.dot_general(x, wb_ref[...], (((1,), (1,)), ((), ())), preferred_element_type=F32)
        return jnp.dot(x, wb_ref[...], preferred_element_type=F32)

    @pl.when(pl.program_id(1) == 0)
    def _():
        w = w_ref[...]
        if w_mode == "nk_transpose":
            w = w.T
        wb_ref[...] = w.astype(BF16)
        zs = mm(xs_ref[...])
        if rope:
            _rope_store(zs, coss_ref[...], sins_ref[...], os_ref)
        else:
            os_ref[...] = zs.astype(os_ref.dtype)

    z = mm(x_ref[...])
    if rope:
        _rope_store(z, cos_ref[...], sin_ref[...], o_ref)
    else:
        o_ref[...] = z.astype(o_ref.dtype)


def _swiglu_in_kernel(x_ref, xs_ref, wg_ref, wv_ref, o_ref, os_ref, wgb_ref, wvb_ref):
    def act(x):
        g = jnp.dot(x, wgb_ref[...], preferred_element_type=F32)
        v = jnp.dot(x, wvb_ref[...], preferred_element_type=F32)
        return g * _sigmoid(g) * v

    @pl.when(pl.program_id(1) == 0)
    def _():
        wgb_ref[...] = wg_ref[...].astype(BF16)
        wvb_ref[...] = wv_ref[...].astype(BF16)
        os_ref[...] = act(xs_ref[...]).astype(os_ref.dtype)

    o_ref[...] = act(x_ref[...]).astype(o_ref.dtype)


def _proj(x, xs, w, layer, col0, ncols, tm, tn, out_dtype, rope=None, stack=None, w_t=False,
          name="proj"):
    m, k = x.shape
    rows_s = xs.shape[0]
    c0 = col0 // tn
    if w_t:
        w_spec = pl.BlockSpec((None, tn, k), lambda j, i: (layer, c0 + j, 0))
        w_mode = "nk_transpose" if tn % LANES == 0 else "nk_contract_last"
    else:
        w_spec = pl.BlockSpec((None, k, tn), lambda j, i: (layer, 0, c0 + j))
        w_mode = "kn"
    wb_shape = (tn, k) if w_mode == "nk_contract_last" else (k, tn)
    in_specs = [
        pl.BlockSpec((tm, k), lambda j, i: (i, 0)),
        pl.BlockSpec((rows_s, k), lambda j, i: (0, 0)),
        w_spec,
    ]
    args = [x, xs, w]
    if rope is not None:
        period = rope[0].shape[0] // tm
        in_specs += [pl.BlockSpec((tm, A_HEAD_DIM), lambda j, i: (i % period, 0))] * 2
        in_specs += [pl.BlockSpec((rows_s, A_HEAD_DIM), lambda j, i: (0, 0))] * 2
        args += list(rope)
    aliases = {}
    if stack is None:
        out_specs = [pl.BlockSpec((tm, tn), lambda j, i: (i, j)),
                     pl.BlockSpec((rows_s, tn), lambda j, i: (0, j))]
        out_shape = [jax.ShapeDtypeStruct((m, ncols), out_dtype),
                     jax.ShapeDtypeStruct((rows_s, ncols), out_dtype)]
    else:
        n_stack, idx, prev = stack
        out_specs = [pl.BlockSpec((None, tm, tn), lambda j, i: (idx, i, j)),
                     pl.BlockSpec((None, rows_s, tn), lambda j, i: (idx, 0, j))]
        out_shape = [jax.ShapeDtypeStruct((n_stack, m, ncols), out_dtype),
                     jax.ShapeDtypeStruct((n_stack, rows_s, ncols), out_dtype)]
        if prev is not None:
            aliases = {len(args): 0, len(args) + 1: 1}
            in_specs += [pl.BlockSpec(memory_space=pl.ANY)] * 2
            args += list(prev)
    return pl.pallas_call(
        functools.partial(_proj_kernel, rope=rope is not None, n_alias=len(aliases), w_mode=w_mode),
        grid=(ncols // tn, m // tm),
        in_specs=in_specs,
        out_specs=out_specs,
        out_shape=out_shape,
        input_output_aliases=aliases,
        scratch_shapes=[pltpu.VMEM(wb_shape, BF16)],
        compiler_params=_cparams(2, 56),
        name=name,
    )(*args)


def _swiglu_in(x, xs, w, layer, tm, tn):
    m, k = x.shape
    rows_s = xs.shape[0]
    dff = w.shape[2] // 2
    nv = dff // tn
    return pl.pallas_call(
        _swiglu_in_kernel,
        grid=(dff // tn, m // tm),
        in_specs=[
            pl.BlockSpec((tm, k), lambda j, i: (i, 0)),
            pl.BlockSpec((rows_s, k), lambda j, i: (0, 0)),
            pl.BlockSpec((None, k, tn), lambda j, i: (layer, 0, j)),
            pl.BlockSpec((None, k, tn), lambda j, i: (layer, 0, nv + j)),
        ],
        out_specs=[pl.BlockSpec((tm, tn), lambda j, i: (i, j)),
                   pl.BlockSpec((rows_s, tn), lambda j, i: (0, j))],
        out_shape=[jax.ShapeDtypeStruct((m, dff), BF16), jax.ShapeDtypeStruct((rows_s, dff), BF16)],
        scratch_shapes=[pltpu.VMEM((k, tn), BF16), pltpu.VMEM((k, tn), BF16)],
        compiler_params=_cparams(2, 56),
        name="swiglu_in",
    )(x, xs, w, w)


def _cast_kernel(w_ref, o_ref):
    o_ref[...] = w_ref[...].astype(o_ref.dtype)


def _cast_bf16(w, layer, tk):
    _, k, n = w.shape
    return pl.pallas_call(
        _cast_kernel,
        grid=(k // tk,),
        in_specs=[pl.BlockSpec((None, tk, n), lambda i: (layer, i, 0))],
        out_specs=pl.BlockSpec((tk, n), lambda i: (i, 0)),
        out_shape=jax.ShapeDtypeStruct((k, n), BF16),
        compiler_params=_cparams(1, 32),
        name="cast_bf16",
    )(w)


def _out_ln_kernel(*refs, n_parts, res_w):
    ap_refs = refs[:n_parts]
    as_refs = refs[n_parts:2 * n_parts]
    (w_ref, x_ref, gate_ref, sc_ref, sh_ref, xs_ref, gates_ref, scs_ref, shs_ref, lng_ref, lnb_ref,
     xo_ref, uo_ref, xso_ref, uso_ref) = refs[2 * n_parts:]

    def run(a_refs, x_ref, gate_ref, sc_ref, sh_ref, xo_ref, uo_ref):
        acc = None
        off = 0
        for a_ref in a_refs:
            kk = a_ref.shape[1]
            part = jnp.dot(a_ref[...], w_ref[off:off + kk, :], preferred_element_type=F32)
            acc = part if acc is None else acc + part
            off += kk
        y = ALPHA * x_ref[...] + (res_w * (1.0 + gate_ref[...])) * acc
        mu = jnp.mean(y, axis=-1, keepdims=True)
        yc = y - mu
        var = jnp.mean(yc * yc, axis=-1, keepdims=True)
        xh = yc * lax.rsqrt(var + LN_EPS)
        g = lng_ref[...]
        b = lnb_ref[...]
        mod = 1.0 + sc_ref[...]
        xo_ref[...] = xh * g + b
        uo_ref[...] = (xh * (g * mod) + (b * mod + sh_ref[...])).astype(uo_ref.dtype)

    @pl.when(pl.program_id(0) == 0)
    def _():
        run(as_refs, xs_ref, gates_ref, scs_ref, shs_ref, xso_ref, uso_ref)

    run(ap_refs, x_ref, gate_ref, sc_ref, sh_ref, xo_ref, uo_ref)


def _out_ln(parts, parts_s, wb, x, xs, mods, mods_s, specs, specs_s, ln_g, ln_b, ln_row, res_w, tm):
    m = parts[0].shape[0]
    rows_s = xs.shape[0]
    k, d = wb.shape
    ln_spec = pl.BlockSpec((None, 1, d), lambda i: (ln_row, 0, 0))
    row = lambda width: pl.BlockSpec((tm, width), lambda i: (i, 0))
    whole = lambda width: pl.BlockSpec((rows_s, width), lambda i: (0, 0))
    return pl.pallas_call(
        functools.partial(_out_ln_kernel, n_parts=len(parts), res_w=res_w),
        grid=(m // tm,),
        in_specs=[row(p.shape[1]) for p in parts] + [whole(p.shape[1]) for p in parts_s] + [
            pl.BlockSpec((k, d), lambda i: (0, 0), pipeline_mode=pl.Buffered(1)),
            row(d)] + [s(tm) for s in specs] + [whole(d)] + [s(rows_s) for s in specs_s] + [
            ln_spec, ln_spec],
        out_specs=[row(d), row(d), whole(d), whole(d)],
        out_shape=[jax.ShapeDtypeStruct((m, d), F32), jax.ShapeDtypeStruct((m, d), BF16),
                   jax.ShapeDtypeStruct((rows_s, d), F32), jax.ShapeDtypeStruct((rows_s, d), BF16)],
        compiler_params=_cparams(1, 56),
        name="out_postnorm",
    )(*parts, *parts_s, wb, x, mods, mods, mods, xs, mods_s, mods_s, mods_s, ln_g, ln_b)


def _moba_prompt_kernel(q_ref, k_ref, v_ref, o_ref, kb_ref, vb_ref, kmean_ref,
                        m_ref, acc_ref, *, nblk, scale):
    qi = pl.program_id(2)
    blk = MOBA_BLOCK
    hd = A_HEAD_DIM
    nheads = q_ref.shape[1] // hd
    lanes = lambda hh: slice(hh * hd, (hh + 1) * hd)

    @pl.when(qi == 0)
    def _():
        seq = k_ref.shape[0]
        r = lax.broadcasted_iota(jnp.int32, (seq, hd), 0)
        c = lax.broadcasted_iota(jnp.int32, (seq, hd), 1)
        in_blk = (r >= c * blk) & (r < c * blk + blk)
        onehot = jnp.where(in_blk, 1.0, 0.0).astype(BF16)
        kmean_ref[...] = jnp.zeros_like(kmean_ref)
        for hh in range(nheads):
            kf = k_ref[:, lanes(hh)]
            kb_ref[hh, :, 0:hd] = kf.astype(BF16)
            kb_ref[hh, :, hd:2 * hd] = onehot
            vb_ref[hh, :, 0:hd] = v_ref[:, lanes(hh)].astype(BF16)
            vb_ref[hh, :, hd:2 * hd] = jnp.ones((seq, hd), BF16)
            kmean_ref[hh, 0:nblk, :] = jnp.mean(kf.reshape(nblk, blk, hd), axis=1)

    nt = (((1,), (1,)), ((), ()))
    start = pl.multiple_of(qi * blk, blk)
    q_augs = []
    for hh in range(nheads):
        q = q_ref[:, lanes(hh)]
        gate = lax.dot_general(q, kmean_ref[hh], nt, precision=lax.Precision.HIGHEST,
                               preferred_element_type=F32)
        lane = lax.broadcasted_iota(jnp.int32, gate.shape, 1)
        lanef = lane.astype(F32)
        g = jnp.where(lane < qi, gate, -jnp.inf)
        selm = jnp.zeros(gate.shape, F32)
        for _ in range(MOBA_TOPK):
            mx = jnp.max(g, axis=1, keepdims=True)
            cand = jnp.where(g == mx, lanef, 1e9)
            cand = jnp.where(mx > -jnp.inf, cand, 1e9)
            pick = lanef == jnp.min(cand, axis=1, keepdims=True)
            selm = jnp.where(pick, 1.0, selm)
            g = jnp.where(pick, -jnp.inf, g)
        qb = q.astype(BF16)
        q_augs.append(jnp.concatenate([qb, ((1.0 - selm) * NEG_BIG).astype(BF16)], axis=1))
        s = lax.dot_general(qb, kb_ref[hh, pl.ds(start, blk), 0:hd], nt, preferred_element_type=F32) * scale
        row = lax.broadcasted_iota(jnp.int32, s.shape, 0)
        col = lax.broadcasted_iota(jnp.int32, s.shape, 1)
        s = jnp.where(col <= row, s, NEG_BIG)
        m0 = jnp.max(s, axis=1, keepdims=True)
        p = jnp.exp(s - m0)
        m_ref[hh] = jnp.broadcast_to(m0, m_ref.shape[1:])
        acc_ref[hh] = jnp.dot(p.astype(BF16), vb_ref[hh, pl.ds(start, blk), :], preferred_element_type=F32)

    grp = MOBA_GROUP
    span = grp * blk

    def past_group(gi, carry):
        st = pl.multiple_of(gi * span, span)
        for hh in range(nheads):
            sn = lax.dot_general(q_augs[hh], kb_ref[hh, pl.ds(st, span), :], nt,
                                 preferred_element_type=F32) * scale
            m_prev = m_ref[hh]
            m_new = jnp.maximum(m_prev, jnp.max(sn, axis=1, keepdims=True))
            a = jnp.exp(m_prev - m_new)
            pn = jnp.exp(sn - jnp.concatenate([m_new] * (span // hd), axis=1))
            acc_ref[hh] = jnp.concatenate([a, a], axis=1) * acc_ref[hh] + jnp.dot(
                pn.astype(BF16), vb_ref[hh, pl.ds(st, span), :], preferred_element_type=F32)
            m_ref[hh] = m_new
        return carry

    lax.fori_loop(0, (qi + grp - 1) // grp, past_group, 0)
    for hh in range(nheads):
        acc = acc_ref[hh]
        o_ref[:, lanes(hh)] = (acc[:, 0:hd] / acc[:, hd:2 * hd]).astype(o_ref.dtype)


def _moba_prompt(q, k, v, slab, bsz):
    rows, width = q.shape
    seq = rows // bsz
    nh = width // A_HEAD_DIM
    nblk = seq // MOBA_BLOCK
    assert nblk % MOBA_GROUP == 0
    blk = MOBA_BLOCK
    hd = A_HEAD_DIM
    hps = MOBA_HEADS_PER_STEP
    assert nh % hps == 0
    return pl.pallas_call(
        functools.partial(_moba_prompt_kernel, nblk=nblk, scale=hd ** -0.5),
        grid=(bsz, nh // hps, nblk),
        in_specs=[
            pl.BlockSpec((blk, hps * hd), lambda b, h, i: (b * nblk + i, h)),
            pl.BlockSpec((None, seq, hps * hd), lambda b, h, i: (slab, b, h), pipeline_mode=pl.Buffered(1)),
            pl.BlockSpec((None, seq, hps * hd), lambda b, h, i: (slab, b, h), pipeline_mode=pl.Buffered(1)),
        ],
        out_specs=pl.BlockSpec((blk, hps * hd), lambda b, h, i: (b * nblk + i, h)),
        out_shape=jax.ShapeDtypeStruct((rows, width), BF16),
        scratch_shapes=[
            pltpu.VMEM((hps, seq, 2 * hd), BF16), pltpu.VMEM((hps, seq, 2 * hd), BF16),
            pltpu.VMEM((hps, LANES, hd), F32),
            pltpu.VMEM((hps, blk, hd), F32), pltpu.VMEM((hps, blk, 2 * hd), F32),
        ],
        compiler_params=_cparams(3, 52),
        name="moba_prompt",
    )(q, k, v)


def _mlstm_prompt_kernel(q_ref, k_ref, v_ref, og_ref, g_ref, gb_ref, ng_ref,
                         h_ref, c_ref, n_ref, m_ref, *, nb, nh, dk, dv):
    lc = MLSTM_CHUNK
    chunks = q_ref.shape[1] // lc

    @pl.when(pl.program_id(0) == 0)
    def _():
        c_ref[...] = jnp.zeros_like(c_ref)
        n_ref[...] = jnp.zeros_like(n_ref)
        m_ref[...] = jnp.zeros_like(m_ref)

    tt = lax.broadcasted_iota(jnp.int32, (lc, lc), 0)
    ss = lax.broadcasted_iota(jnp.int32, (lc, lc), 1)
    causal = ss <= tt
    eye = ss == tt
    nt = (((1,), (1,)), ((), ()))
    tn = (((0,), (0,)), ((), ()))
    gb = gb_ref[...]
    for cc in range(chunks):
        rows = slice(cc * lc, (cc + 1) * lc)
        for b in range(nb):
            gates = g_ref[b, rows, :] + gb
            for h in range(nh):
                bh = b * nh + h
                i_col = gates[:, h:h + 1]
                fpre = gates[:, nh + h:nh + h + 1]
                f_col = jnp.minimum(fpre, 0.0) - jnp.log1p(jnp.exp(-jnp.abs(fpre)))
                f_row = jnp.sum(jnp.where(eye, f_col, 0.0), axis=0, keepdims=True)
                i_row = jnp.sum(jnp.where(eye, i_col, 0.0), axis=0, keepdims=True)
                b_col = jnp.sum(jnp.where(causal, f_row, 0.0), axis=1, keepdims=True)
                b_row = jnp.sum(jnp.where(ss >= tt, f_col, 0.0), axis=0, keepdims=True)
                m_prev = m_ref[bh][:, 0:1]
                d = jnp.where(causal, b_col - b_row + i_row, NEG_BIG)
                inter = b_col + m_prev
                m_t = jnp.maximum(jnp.max(d, axis=1, keepdims=True), inter)
                w = jnp.exp(d - m_t)
                q = q_ref[b, rows, h * dk:(h + 1) * dk]
                k = k_ref[b, rows, h * dk:(h + 1) * dk] * (dk ** -0.5)
                vb = v_ref[b, rows, h * dv:(h + 1) * dv].astype(BF16)
                qb = q.astype(BF16)
                s = lax.dot_general(qb, k.astype(BF16), nt, preferred_element_type=F32) * w
                wi = jnp.exp(inter - m_t)
                cst = c_ref[bh]
                nrow = n_ref[bh]
                num = wi * jnp.dot(qb, cst.astype(BF16), preferred_element_type=F32) + jnp.dot(
                    s.astype(BF16), vb, preferred_element_type=F32)
                den = wi * jnp.sum(q * nrow, axis=1, keepdims=True) + jnp.sum(s, axis=1, keepdims=True)
                hh = num / jnp.maximum(jnp.abs(den), jnp.exp(-m_t))
                b_last = b_col[lc - 1:lc, :]
                g_col = b_last - b_col + i_col
                m_new = jnp.maximum(b_last + m_prev, jnp.max(g_col, axis=0, keepdims=True))
                wc = jnp.exp(b_last + m_prev - m_new)
                kw = k * jnp.exp(g_col - m_new)
                c_ref[bh] = wc * cst + lax.dot_general(kw.astype(BF16), vb, tn, preferred_element_type=F32)
                n_ref[bh] = wc * nrow + jnp.sum(kw, axis=0, keepdims=True)
                m_ref[bh] = jnp.broadcast_to(m_new, (1, LANES))
                hn = hh * lax.rsqrt(jnp.mean(hh * hh, axis=1, keepdims=True) + LN_EPS)
                hn = hn * ng_ref[:, h * dv:(h + 1) * dv] * _sigmoid(og_ref[b, rows, h * dv:(h + 1) * dv])
                h_ref[b, rows, h * dv:(h + 1) * dv] = hn.astype(h_ref.dtype)


def _mlstm_prompt(zm, gates, gate_bias, norm_g, bsz):
    rows, _ = zm.shape
    seq = rows // bsz
    nh = B_HEADS
    v_w = norm_g.shape[1]
    qk_w = v_w // 2
    dk, dv = qk_w // nh, v_w // nh
    lc = MLSTM_CHUNK * MLSTM_CHUNKS_PER_STEP
    assert seq % lc == 0
    z3 = zm.reshape(bsz, seq, zm.shape[1])
    gw = gates.shape[1]
    g3 = gates.reshape(bsz, seq, gw)
    nbh = bsz * nh
    full = lambda shape: pl.BlockSpec(shape, lambda c: (0,) * len(shape))
    return pl.pallas_call(
        functools.partial(_mlstm_prompt_kernel, nb=bsz, nh=nh, dk=dk, dv=dv),
        grid=(seq // lc,),
        in_specs=[
            pl.BlockSpec((bsz, lc, qk_w), lambda c: (0, c, 0)),
            pl.BlockSpec((bsz, lc, qk_w), lambda c: (0, c, 1)),
            pl.BlockSpec((bsz, lc, v_w), lambda c: (0, c, 1)),
            pl.BlockSpec((bsz, lc, v_w), lambda c: (0, c, 2)),
            pl.BlockSpec((bsz, lc, gw), lambda c: (0, c, 0)),
            full((1, gw)), full((1, v_w)),
        ],
        out_specs=[
            pl.BlockSpec((bsz, lc, v_w), lambda c: (0, c, 0)),
            full((nbh, dk, dv)), full((nbh, 1, dk)), full((nbh, 1, LANES)),
        ],
        out_shape=[
            jax.ShapeDtypeStruct((bsz, seq, v_w), BF16),
            jax.ShapeDtypeStruct((nbh, dk, dv), F32),
            jax.ShapeDtypeStruct((nbh, 1, dk), F32),
            jax.ShapeDtypeStruct((nbh, 1, LANES), F32),
        ],
        compiler_params=_cparams(1, 32),
        name="mlstm_prompt",
    )(z3, z3, z3, z3, g3, gate_bias, norm_g)


def _rglru_gates(xconv, wa_ref, ba, wx_ref, bx, lam):
    bd = wa_ref.shape[1]
    r_parts, i_parts = [], []
    for n in range(wa_ref.shape[0]):
        xb = xconv[:, n * bd:(n + 1) * bd].astype(BF16)
        r_parts.append(jnp.dot(xb, wa_ref[n].astype(BF16), preferred_element_type=F32))
        i_parts.append(jnp.dot(xb, wx_ref[n].astype(BF16), preferred_element_type=F32))
    r = _sigmoid(jnp.concatenate(r_parts, axis=1) + ba)
    ig = _sigmoid(jnp.concatenate(i_parts, axis=1) + bx)
    log_a = (-RG_C) * r * _softplus(-lam)
    a = jnp.exp(log_a)
    mult = jnp.sqrt(1.0 - a * a)
    return a, mult * ig * xconv


def _rglru_prompt_kernel(gate_ref, xr_ref, cw_ref, cb_ref, wa_ref, ba_ref, wx_ref, bx_ref, lam_ref,
                         y_ref, hl_ref, buf_ref, xext_ref, a_ref, b_ref, hs_ref, hc_ref):
    t = pl.program_id(1)
    tt = xr_ref.shape[0]
    pad = 8
    nbuf = CONV_WIDTH - 1

    @pl.when(t == 0)
    def _():
        xext_ref[0:pad, :] = jnp.zeros((pad, xext_ref.shape[1]), F32)
        hc_ref[...] = jnp.zeros_like(hc_ref)

    @pl.when(t > 0)
    def _():
        xext_ref[0:pad, :] = xext_ref[tt:tt + pad, :]

    xr = xr_ref[...]
    xext_ref[pad:pad + tt, :] = xr
    xconv = cb_ref[...] + xext_ref[pad - nbuf:pad - nbuf + tt, :] * cw_ref[0:1, :]
    for j in range(1, CONV_WIDTH):
        xconv = xconv + xext_ref[pad - nbuf + j:pad - nbuf + j + tt, :] * cw_ref[j:j + 1, :]
    a, bterm = _rglru_gates(xconv, wa_ref, ba_ref[...], wx_ref, bx_ref[...], lam_ref[...])
    a_ref[...] = a
    b_ref[...] = bterm

    def step(i, h):
        h = a_ref[pl.ds(i, 1), :] * h + b_ref[pl.ds(i, 1), :]
        hs_ref[pl.ds(i, 1), :] = h
        return h

    h_last = lax.fori_loop(0, tt, step, hc_ref[...], unroll=8)
    hc_ref[...] = h_last
    y_ref[...] = (_gelu_tanh(gate_ref[...]) * hs_ref[...]).astype(y_ref.dtype)
    hl_ref[...] = h_last
    buf_ref[...] = xr[tt - nbuf:tt, :]


def _rglru_prompt(zr, layer, conv_w, conv_b, w_a, b_a, w_x, b_x, lam, bsz, tt):
    rows, w2 = zr.shape
    width = w2 // 2
    seq = rows // bsz
    nt = seq // tt
    nblk, bd = w_a.shape[1], w_a.shape[2]
    vec = lambda a: a.reshape(a.shape[0], 1, width)
    vspec = pl.BlockSpec((None, 1, width), lambda b, t: (layer, 0, 0))
    wspec = pl.BlockSpec((None, nblk, bd, bd), lambda b, t: (layer, 0, 0, 0))
    nbuf = CONV_WIDTH - 1
    return pl.pallas_call(
        _rglru_prompt_kernel,
        grid=(bsz, nt),
        in_specs=[
            pl.BlockSpec((tt, width), lambda b, t: (b * nt + t, 0)),
            pl.BlockSpec((tt, width), lambda b, t: (b * nt + t, 1)),
            pl.BlockSpec((None, CONV_WIDTH, width), lambda b, t: (layer, 0, 0)),
            vspec, wspec, vspec, wspec, vspec, vspec,
        ],
        out_specs=[
            pl.BlockSpec((tt, width), lambda b, t: (b * nt + t, 0)),
            pl.BlockSpec((None, 1, width), lambda b, t: (b, 0, 0)),
            pl.BlockSpec((None, nbuf, width), lambda b, t: (b, 0, 0)),
        ],
        out_shape=[
            jax.ShapeDtypeStruct((rows, width), BF16),
            jax.ShapeDtypeStruct((bsz, 1, width), F32),
            jax.ShapeDtypeStruct((bsz, nbuf, width), F32),
        ],
        scratch_shapes=[
            pltpu.VMEM((tt + 8, width), F32), pltpu.VMEM((tt, width), F32),
            pltpu.VMEM((tt, width), F32), pltpu.VMEM((tt, width), F32), pltpu.VMEM((1, width), F32),
        ],
        compiler_params=_cparams(2, 48),
        name="rglru_prompt",
    )(zr, zr, conv_w, vec(conv_b), w_a, vec(b_a), w_x, vec(b_x), vec(lam))


def _moba_gate_kernel(tbl_ref, q_ref, *refs, nheads, nsteps, pages_per_step, pages_per_blk):
    b = pl.program_id(0)
    st = pl.program_id(1)
    hd = A_HEAD_DIM
    k_refs = refs[:pages_per_step]
    sel_ref, ksum_ref = refs[pages_per_step:]
    page = k_refs[0].shape[0]
    blks_per_step = pages_per_step // pages_per_blk
    for i in range(blks_per_step):
        tot = jnp.sum(k_refs[i * pages_per_blk][...], axis=0)
        for u in range(1, pages_per_blk):
            tot = tot + jnp.sum(k_refs[i * pages_per_blk + u][...], axis=0)
        ksum_ref[st * blks_per_step + i] = tot

    @pl.when(st == nsteps - 1)
    def _():
        nblk = ksum_ref.shape[0]
        inv = 1.0 / (pages_per_blk * page)
        rowi = lax.broadcasted_iota(jnp.int32, (nblk, 1), 0).astype(F32)
        out_r = lax.broadcasted_iota(jnp.int32, sel_ref.shape, 0)
        out_c = lax.broadcasted_iota(jnp.int32, sel_ref.shape, 1)
        out = jnp.zeros(sel_ref.shape, jnp.int32)
        qrow = q_ref[pl.ds(b, 1), :]
        for h in range(nheads):
            km = ksum_ref[:, h, :] * inv
            g = jnp.sum(km * qrow[:, h * hd:(h + 1) * hd], axis=1, keepdims=True)
            for i in range(MOBA_TOPK):
                mx = jnp.max(g, axis=0, keepdims=True)
                idx = jnp.min(jnp.where(g == mx, rowi, 1e9), axis=0, keepdims=True)
                out = jnp.where((out_r == h) & (out_c == i), idx.astype(jnp.int32), out)
                g = jnp.where(rowi == idx, -jnp.inf, g)
        sel_ref[...] = out


GATE_PAGES_PER_STEP = 8


def _moba_sample_gate(q, cache_k, layer, tbl_flat, nreq, npages):
    page, nheads, hd = cache_k.shape[2:]
    ppb = MOBA_BLOCK // page
    nblk = npages // ppb
    pps = GATE_PAGES_PER_STEP
    nsteps = npages // pps

    def page_spec(u):
        return pl.BlockSpec((None, None, page, nheads, hd),
                            lambda b, s, tbl: (layer, tbl[b * npages + s * pps + u], 0, 0, 0))

    grid_spec = pltpu.PrefetchScalarGridSpec(
        num_scalar_prefetch=1,
        grid=(nreq, nsteps),
        in_specs=[pl.BlockSpec(q.shape, lambda b, s, tbl: (0, 0))] + [page_spec(u) for u in range(pps)],
        out_specs=pl.BlockSpec((None, nheads, LANES), lambda b, s, tbl: (b, 0, 0)),
        scratch_shapes=[pltpu.VMEM((nblk, nheads, hd), F32)],
    )
    return pl.pallas_call(
        functools.partial(_moba_gate_kernel, nheads=nheads, nsteps=nsteps, pages_per_step=pps,
                          pages_per_blk=ppb),
        grid_spec=grid_spec,
        out_shape=jax.ShapeDtypeStruct((nreq, nheads, LANES), jnp.int32),
        compiler_params=_cparams(2, 32),
        name="moba_sample_gate",
    )(tbl_flat, q, *([cache_k] * pps))


def _moba_sample_attn_kernel(tbl_ref, sel_ref, q_ref, kn_ref, vn_ref, *refs, npg, scale):
    k_refs = refs[:npg]
    v_refs = refs[npg:2 * npg]
    o_ref = refs[2 * npg]
    b = pl.program_id(0)
    h = pl.program_id(1)
    nheads = q_ref.shape[1]
    is_h = lax.broadcasted_iota(jnp.int32, (1, nheads, 1), 1) == h
    q_m = jnp.where(is_h, q_ref[b][None], 0.0)

    def score(kp):
        part = jnp.sum(kp * q_m, axis=2, keepdims=True)
        return jnp.sum(part, axis=1, keepdims=True) * scale

    s_own = score(kn_ref[b][None])
    ss = [score(k_ref[...]) for k_ref in k_refs]
    m = s_own
    for s in ss:
        m = jnp.maximum(m, jnp.max(s, axis=0, keepdims=True))
    l = jnp.exp(s_own - m)
    acc = l * vn_ref[b][None]
    for s, v_ref in zip(ss, v_refs):
        p = jnp.exp(s - m)
        l = l + jnp.sum(p, axis=0, keepdims=True)
        acc = acc + jnp.sum(p * v_ref[...], axis=0, keepdims=True)
    out = jnp.sum(jnp.where(is_h, acc / l, 0.0), axis=1)
    o_ref[...] = out.astype(o_ref.dtype)


def _moba_sample_attn(q, k_new, v_new, cache_k, cache_v, layer, tbl_flat, sel_flat, nreq, npages):
    page, nheads, hd = cache_k.shape[2:]
    ppb = MOBA_BLOCK // page
    npg = MOBA_TOPK * ppb

    def page_spec(j):
        def page_map(b, h, tbl, sel):
            blk = sel[(b * nheads + h) * LANES + j // ppb]
            return (layer, tbl[b * npages + blk * ppb + j % ppb], 0, 0, 0)
        return pl.BlockSpec((None, None, page, nheads, hd), page_map)

    head = pl.BlockSpec(q.shape, lambda b, h, tbl, sel: (0, 0, 0))
    pages = [page_spec(j) for j in range(npg)]
    grid_spec = pltpu.PrefetchScalarGridSpec(
        num_scalar_prefetch=2,
        grid=(nreq, nheads),
        in_specs=[head, head, head] + pages + pages,
        out_specs=pl.BlockSpec((None, 1, hd), lambda b, h, tbl, sel: (b, 0, h)),
    )
    return pl.pallas_call(
        functools.partial(_moba_sample_attn_kernel, npg=npg, scale=hd ** -0.5),
        grid_spec=grid_spec,
        out_shape=jax.ShapeDtypeStruct((nreq, 1, nheads * hd), F32),
        compiler_params=_cparams(2, 32),
        name="moba_sample_attn",
    )(tbl_flat, sel_flat, q, k_new, v_new, *([cache_k] * npg), *([cache_v] * npg))


def _mlstm_sample_kernel(q_ref, k_ref, v_ref, og_ref, g_ref, gb_ref, ng_ref, c0_ref, n0_ref, m0_ref,
                         h_ref, c_ref, n_ref, m_ref, *, nh, dk, dv):
    b = pl.program_id(0)
    rr = lax.broadcasted_iota(jnp.int32, (dk, dk), 0)
    cc = lax.broadcasted_iota(jnp.int32, (dk, dk), 1)
    eye = rr == cc
    gates = g_ref[pl.ds(b, 1), :] + gb_ref[...]
    q_all = q_ref[pl.ds(b, 1), :]
    k_all = k_ref[pl.ds(b, 1), :]
    v_all = v_ref[pl.ds(b, 1), :]
    og_all = og_ref[pl.ds(b, 1), :]
    for h in range(nh):
        ii = gates[:, h:h + 1]
        fpre = gates[:, nh + h:nh + h + 1]
        ff = jnp.minimum(fpre, 0.0) - jnp.log1p(jnp.exp(-jnp.abs(fpre)))
        m_prev = m0_ref[h][:, 0:1]
        q = q_all[:, h * dk:(h + 1) * dk]
        k = k_all[:, h * dk:(h + 1) * dk] * (dk ** -0.5)
        v = v_all[:, h * dv:(h + 1) * dv]
        inter = ff + m_prev
        m_t = jnp.maximum(ii, inter)
        s = jnp.sum(q * k, axis=1, keepdims=True) * jnp.exp(ii - m_t)
        wi = jnp.exp(inter - m_t)
        q_col = jnp.sum(jnp.where(eye, q, 0.0), axis=1, keepdims=True)
        k_col = jnp.sum(jnp.where(eye, k, 0.0), axis=1, keepdims=True)
        cst = c0_ref[h]
        nrow = n0_ref[h]
        num = wi * jnp.sum(q_col * cst, axis=0, keepdims=True) + s * v
        den = wi * jnp.sum(q * nrow, axis=1, keepdims=True) + s
        hh = num / jnp.maximum(jnp.abs(den), jnp.exp(-m_t))
        m_new = m_t
        wc = jnp.exp(inter - m_new)
        wg = jnp.exp(ii - m_new)
        c_ref[h] = wc * cst + (wg * k_col) * v
        n_ref[h] = wc * nrow + wg * k
        m_ref[h] = jnp.broadcast_to(m_new, (1, LANES))
        hn = hh * lax.rsqrt(jnp.mean(hh * hh, axis=1, keepdims=True) + LN_EPS)
        hn = hn * ng_ref[:, h * dv:(h + 1) * dv] * _sigmoid(og_all[:, h * dv:(h + 1) * dv])
        h_ref[:, h * dv:(h + 1) * dv] = hn.astype(h_ref.dtype)


def _mlstm_sample(zm, gates, gate_bias, norm_g, c0, n0, m0, layer, nreq):
    nh = B_HEADS
    v_w = norm_g.shape[1]
    qk_w = v_w // 2
    dk, dv = qk_w // nh, v_w // nh
    rows = zm.shape[0]
    full = lambda shape: pl.BlockSpec(shape, lambda b: (0,) * len(shape))
    return pl.pallas_call(
        functools.partial(_mlstm_sample_kernel, nh=nh, dk=dk, dv=dv),
        grid=(nreq,),
        in_specs=[
            pl.BlockSpec((rows, qk_w), lambda b: (0, 0)),
            pl.BlockSpec((rows, qk_w), lambda b: (0, 1)),
            pl.BlockSpec((rows, v_w), lambda b: (0, 1)),
            pl.BlockSpec((rows, v_w), lambda b: (0, 2)),
            full(gates.shape), full(gate_bias.shape), full((1, v_w)),
            pl.BlockSpec((None, None, nh, dk, dv), lambda b: (layer, b, 0, 0, 0)),
            pl.BlockSpec((None, None, nh, 1, dk), lambda b: (layer, b, 0, 0, 0)),
            pl.BlockSpec((None, None, nh, 1, LANES), lambda b: (layer, b, 0, 0, 0)),
        ],
        out_specs=[
            pl.BlockSpec((None, 1, v_w), lambda b: (b, 0, 0)),
            pl.BlockSpec((None, nh, dk, dv), lambda b: (b, 0, 0, 0)),
            pl.BlockSpec((None, nh, 1, dk), lambda b: (b, 0, 0, 0)),
            pl.BlockSpec((None, nh, 1, LANES), lambda b: (b, 0, 0, 0)),
        ],
        out_shape=[
            jax.ShapeDtypeStruct((nreq, 1, v_w), F32),
            jax.ShapeDtypeStruct((nreq, nh, dk, dv), F32),
            jax.ShapeDtypeStruct((nreq, nh, 1, dk), F32),
            jax.ShapeDtypeStruct((nreq, nh, 1, LANES), F32),
        ],
        compiler_params=_cparams(1, 32),
        name="mlstm_sample",
    )(zm, zm, zm, zm, gates, gate_bias, norm_g, c0, n0, m0)


def _rglru_sample_kernel(gate_ref, xr_ref, buf_ref, h0_ref, cw_ref, cb_ref, wa_ref, ba_ref, wx_ref,
                         bx_ref, lam_ref, y_ref, h_ref, nbuf_ref):
    nbuf = CONV_WIDTH - 1
    xr = xr_ref[...]
    xconv = cb_ref[...] + buf_ref[0] * cw_ref[0:1, :]
    for j in range(1, nbuf):
        xconv = xconv + buf_ref[j] * cw_ref[j:j + 1, :]
    xconv = xconv + xr * cw_ref[nbuf:nbuf + 1, :]
    a, bterm = _rglru_gates(xconv, wa_ref, ba_ref[...], wx_ref, bx_ref[...], lam_ref[...])
    h = a * h0_ref[...] + bterm
    h_ref[...] = h
    y = _gelu_tanh(gate_ref[...]) * h
    pad_rows = y_ref.shape[0] - y.shape[0]
    y_ref[...] = jnp.concatenate([y, jnp.zeros((pad_rows, y.shape[1]), F32)], axis=0).astype(y_ref.dtype)
    for j in range(nbuf - 1):
        nbuf_ref[j] = buf_ref[j + 1]
    nbuf_ref[nbuf - 1] = xr


def _rglru_sample(zr, layer, buf_t, h0, conv_w, conv_b, w_a, b_a, w_x, b_x, lam, nreq):
    rows, w2 = zr.shape
    width = w2 // 2
    nblk, bd = w_a.shape[1], w_a.shape[2]
    nbuf = CONV_WIDTH - 1
    vec = lambda a: a.reshape(a.shape[0], 1, width)
    vspec = pl.BlockSpec((None, 1, width), lambda i: (layer, 0, 0))
    wspec = pl.BlockSpec((None, nblk, bd, bd), lambda i: (layer, 0, 0, 0))
    return pl.pallas_call(
        _rglru_sample_kernel,
        grid=(1,),
        in_specs=[
            pl.BlockSpec((nreq, width), lambda i: (0, 0)),
            pl.BlockSpec((nreq, width), lambda i: (0, 1)),
            pl.BlockSpec((None, nbuf, nreq, width), lambda i: (layer, 0, 0, 0)),
            pl.BlockSpec((None, nreq, width), lambda i: (layer, 0, 0)),
            pl.BlockSpec((None, CONV_WIDTH, width), lambda i: (layer, 0, 0)),
            vspec, wspec, vspec, wspec, vspec, vspec,
        ],
        out_specs=[
            pl.BlockSpec((rows, width), lambda i: (0, 0)),
            pl.BlockSpec((nreq, width), lambda i: (0, 0)),
            pl.BlockSpec((nbuf, nreq, width), lambda i: (0, 0, 0)),
        ],
        out_shape=[
            jax.ShapeDtypeStruct((rows, width), BF16),
            jax.ShapeDtypeStruct((nreq, width), F32),
            jax.ShapeDtypeStruct((nbuf, nreq, width), F32),
        ],
        compiler_params=_cparams(1, 32),
        name="rglru_sample",
    )(zr, zr, buf_t, h0, conv_w, vec(conv_b), w_a, vec(b_a), w_x, vec(b_x), vec(lam))


def _rope_tables(pos):
    half = A_HEAD_DIM // 2
    inv = ROPE_THETA ** (-jnp.arange(half, dtype=F32) / half)
    ang = pos.astype(F32)[:, None] * inv[None, :]
    cos, sin = jnp.cos(ang), jnp.sin(ang)
    return jnp.concatenate([cos, cos], axis=1), jnp.concatenate([-sin, sin], axis=1)


def kernel(x_prompt, x_sample, cache_k, cache_v, state_mlstm_c, state_mlstm_n, state_mlstm_m,
           state_rglru_h, state_conv, page_table, c_prompt, c_sample, w_ada, b_ada, ln_g, ln_b,
           w_ffn1_in, w_ffn1_out, w_ffn2_in, w_ffn2_out, w_in_ab, b_igate, b_fgate, mlstm_norm_g,
           w_out_ab, w_in_rg, conv_w, conv_b, w_rg_a, b_rg_a, w_rg_x, b_rg_x, lru_lambda, w_out_rg):
    bsz, seq, d = x_prompt.shape
    nreq = x_sample.shape[0]
    depth = w_ada.shape[0]
    n_even = w_in_ab.shape[0]
    a_w = cache_k.shape[3] * cache_k.shape[4]
    v_w = mlstm_norm_g.shape[1]
    qk_w = v_w // 2
    m_w = 2 * qk_w + 2 * v_w
    nh_b = B_HEADS
    past_len = page_table.shape[1] * cache_k.shape[2]
    npages = page_table.shape[1]
    rows_p = bsz * seq
    rs = SAMPLE_ROWS

    tm = 1024
    tn = 512
    tm_p = 512
    tn_p = 1024
    tm_out = 256
    tk_cast = 512
    tt_rg = 256

    c_all = jnp.concatenate([c_prompt, c_sample, jnp.zeros((rs - bsz - nreq, d), F32)], axis=0)
    mods = _mods(c_all, w_ada, b_ada).reshape(depth, rs, 9, d)
    mods_p = jnp.transpose(mods[:, :bsz], (0, 2, 1, 3)).reshape(depth * 9 * bsz, 1, d)
    mods_s = jnp.transpose(mods[:, bsz:bsz + nreq], (0, 2, 1, 3))
    mods_s = jnp.pad(mods_s, ((0, 0), (0, 0), (0, rs - nreq), (0, 0))).reshape(depth * 9, rs, d)

    def mod_spec_p(l, s, j):
        base = ((l * 3 + s) * 3 + j) * bsz

        def make(tile_rows):
            tiles_per_batch = seq // tile_rows
            return pl.BlockSpec((None, 1, d), lambda i: (base + i // tiles_per_batch, 0, 0))
        return make

    def mod_spec_s(l, s, j):
        row = (l * 3 + s) * 3 + j

        def make(tile_rows):
            return pl.BlockSpec((None, rs, d), lambda i: (row, 0, 0))
        return make

    ln_g3 = ln_g.reshape(depth * 3, 1, d)
    ln_b3 = ln_b.reshape(depth * 3, 1, d)

    cos_p, sin_p = _rope_tables(jnp.arange(seq, dtype=jnp.int32))
    cos_s, sin_s = _rope_tables(jnp.full((rs,), past_len, dtype=jnp.int32))

    tbl_flat = page_table.reshape(-1).astype(jnp.int32)
    n0_s = state_mlstm_n[:, :, :, None, :]
    m0_s = jnp.broadcast_to(state_mlstm_m[:, :, :, None, None], state_mlstm_m.shape + (1, LANES))
    conv_t = jnp.transpose(state_conv, (0, 2, 1, 3))
    gate_bias = jnp.concatenate([b_igate, b_fgate], axis=1)
    w_ab_t = jnp.swapaxes(w_in_ab, 1, 2)

    xp = x_prompt.reshape(rows_p, d)
    xs = jnp.pad(x_sample.reshape(nreq, d), ((0, rs - nreq), (0, 0)))
    up = _modulate(xp, mod_spec_p(0, 0, 1), mod_spec_p(0, 0, 0), mods_p, tm)
    us = _modulate(xs, mod_spec_s(0, 0, 1), mod_spec_s(0, 0, 0), mods_s, rs)

    op = dict(k=[], v=[], c=[], n=[], m=[], h=[], buf=[])
    os_ = dict(k=[], v=[], c=[], n=[], m=[], h=[], buf=[])
    rope = (cos_p, sin_p, cos_s, sin_s)
    k_st = v_st = None
    pad_rows = lambda t, w: jnp.pad(t.reshape(nreq, w), ((0, rs - nreq), (0, 0))).astype(BF16)
    heads3 = lambda t: t.reshape(rs, -1, A_HEAD_DIM)

    for l in range(depth):
        wb_ffn1 = _cast_bf16(w_ffn1_out, l, tk_cast)
        wb_ffn2 = _cast_bf16(w_ffn2_out, l, tk_cast)
        wb_mix = _cast_bf16(w_out_ab if l % 2 == 0 else w_out_rg, l // 2, tk_cast)

        def post(parts_p, parts_s, wb, s, res_w, nxt):
            nl, ns = nxt
            pick = lambda spec: [spec(l, s, 2), spec(nl, ns, 1), spec(nl, ns, 0)]
            xo, uo, xso, uso = _out_ln(parts_p, parts_s, wb, xp, xs, mods_p, mods_s, pick(mod_spec_p),
                                       pick(mod_spec_s), ln_g3, ln_b3, l * 3 + s, res_w,
                                       tm_out if wb.shape[0] > d else 2 * tm_out)
            return (xo, uo), (xso, uso)

        act_p, act_s = _swiglu_in(up, us, w_ffn1_in, l, tm, tn)
        (xp, up), (xs, us) = post([act_p], [act_s], wb_ffn1, 0, FFN_RES, (l, 1))

        if l % 2 == 0:
            e = l // 2
            q_p, q_s = _proj(up, us, w_ab_t, e, 0, a_w, tm_p, tn_p, F32, rope=rope, w_t=True, name="proj_q")
            k_st = _proj(up, us, w_ab_t, e, a_w, a_w, tm_p, tn_p, F32, rope=rope, w_t=True,
                         stack=(n_even, e, k_st), name="proj_k")
            v_st = _proj(up, us, w_ab_t, e, 2 * a_w, a_w, tm_p, tn_p, F32, w_t=True,
                         stack=(n_even, e, v_st), name="proj_v")
            k_s, v_s = k_st[1][e], v_st[1][e]
            zm_p, zm_s = _proj(up, us, w_ab_t, e, 3 * a_w, m_w, tm_p, tn_p, F32, w_t=True, name="proj_mlstm")
            n_gates = 2 * nh_b
            g_p, g_s = _proj(up, us, w_ab_t, e, 3 * a_w + m_w, n_gates, tm_p, n_gates, F32, w_t=True,
                             name="proj_gates")
            gb = gate_bias[e:e + 1]
            ng = mlstm_norm_g[e:e + 1]

            oa = _moba_prompt(q_p, k_st[0], v_st[0], e, bsz)
            hm, c1, n1, m1 = _mlstm_prompt(zm_p, g_p, gb, ng, bsz)
            mix_p = [oa, hm.reshape(rows_p, v_w)]
            op["c"].append(c1.reshape(bsz, nh_b, c1.shape[1], c1.shape[2]))
            op["n"].append(n1.reshape(bsz, nh_b, -1))
            op["m"].append(m1[:, 0, 0].reshape(bsz, nh_b))

            sel = _moba_sample_gate(q_s, cache_k, e, tbl_flat, nreq, npages)
            oa = _moba_sample_attn(heads3(q_s), heads3(k_s), heads3(v_s), cache_k, cache_v, e,
                                   tbl_flat, sel.reshape(-1), nreq, npages)
            hm, c1, n1, m1 = _mlstm_sample(zm_s, g_s, gb, ng, state_mlstm_c, n0_s, m0_s, e, nreq)
            mix_s = [pad_rows(oa, a_w), pad_rows(hm, v_w)]
            os_["c"].append(c1)
            os_["n"].append(n1.reshape(nreq, nh_b, -1))
            os_["m"].append(m1[:, :, 0, 0])
        else:
            od = l // 2
            zr_p, zr_s = _proj(up, us, w_in_rg, od, 0, w_in_rg.shape[2], tm_p, tn_p, F32, name="proj_rg")
            y_p, h1, nb = _rglru_prompt(zr_p, od, conv_w, conv_b, w_rg_a, b_rg_a, w_rg_x, b_rg_x,
                                        lru_lambda, bsz, tt_rg)
            op["h"].append(h1.reshape(bsz, -1))
            op["buf"].append(nb)
            y_s, h1, nb = _rglru_sample(zr_s, od, conv_t, state_rglru_h, conv_w, conv_b, w_rg_a,
                                        b_rg_a, w_rg_x, b_rg_x, lru_lambda, nreq)
            os_["h"].append(h1)
            os_["buf"].append(jnp.transpose(nb, (1, 0, 2)))
            mix_p, mix_s = [y_p], [y_s]
        (xp, up), (xs, us) = post(mix_p, mix_s, wb_mix, 1, 1.0, (l, 2))

        act_p, act_s = _swiglu_in(up, us, w_ffn2_in, l, tm, tn)
        (xp, up), (xs, us) = post([act_p], [act_s], wb_ffn2, 2, FFN_RES, (min(l + 1, depth - 1), 0))

    st = jnp.stack
    kv_p = lambda t: t.reshape(n_even, bsz, seq, -1, A_HEAD_DIM)
    kv_s = lambda t: t[:, :nreq].reshape(n_even, nreq, 1, -1, A_HEAD_DIM)
    return (xp.reshape(bsz, seq, d), xs[:nreq].reshape(nreq, 1, d),
            kv_p(k_st[0]), kv_p(v_st[0]), kv_s(k_st[1]), kv_s(v_st[1]),
            st(op["c"]), st(op["n"]), st(op["m"]), st(os_["c"]), st(os_["n"]), st(os_["m"]),
            st(op["h"]), st(op["buf"]), st(os_["h"]), st(os_["buf"]))
```

```python
import functools

import numpy as np
import jax
import jax.numpy as jnp
from jax import lax
from jax.experimental import pallas as pl
from jax.experimental.pallas import tpu as pltpu

F32 = jnp.float32
BF16 = jnp.bfloat16

DEPTH = 4
MOBA_BLOCK = 256
MOBA_TOPK = 3
MOBA_GROUP = 4
MOBA_HEADS_PER_STEP = 4
MLSTM_CHUNKS_PER_STEP = 1
ROPE_THETA = 10000.0
A_HEAD_DIM = 128
B_HEADS = 4
MLSTM_CHUNK = 64
RG_BLOCKS = 16
CONV_WIDTH = 4
RG_C = 8.0
FFN_RES = 0.5
ALPHA = (2.0 * DEPTH) ** 0.25
LN_EPS = 1e-5

LANES = 128
SAMPLE_ROWS = 16
NEG_BIG = -1e30
MIB = 1024 * 1024


def _cparams(n_axes, vmem_mib):
    return pltpu.CompilerParams(
        dimension_semantics=("arbitrary",) * n_axes,
        vmem_limit_bytes=int(vmem_mib * MIB),
    )


def _sigmoid(x):
    return 0.5 * jnp.tanh(0.5 * x) + 0.5


def _softplus(x):
    return jnp.maximum(x, 0.0) + jnp.log1p(jnp.exp(-jnp.abs(x)))


def _gelu_tanh(x):
    return 0.5 * x * (1.0 + jnp.tanh(0.7978845608028654 * (x + 0.044715 * x * x * x)))


def _mods_kernel(c_ref, w_ref, b_ref, o_ref):
    c = c_ref[...]
    s = (c * _sigmoid(c)).astype(BF16)
    o_ref[...] = jnp.dot(s, w_ref[...].astype(BF16), preferred_element_type=F32) + b_ref[...]


def _mods(c_all, w_ada, b_ada):
    depth, k, n = w_ada.shape
    rows = c_all.shape[0]
    tn = 1024 if n % 1024 == 0 else n
    return pl.pallas_call(
        _mods_kernel,
        grid=(depth, n // tn),
        in_specs=[
            pl.BlockSpec((rows, k), lambda l, j: (0, 0)),
            pl.BlockSpec((None, k, tn), lambda l, j: (l, 0, j)),
            pl.BlockSpec((None, 1, tn), lambda l, j: (l, 0, j)),
        ],
        out_specs=pl.BlockSpec((None, rows, tn), lambda l, j: (l, 0, j)),
        out_shape=jax.ShapeDtypeStruct((depth, rows, n), F32),
        compiler_params=_cparams(2, 48),
        name="adaln_mods",
    )(c_all, w_ada, b_ada.reshape(depth, 1, n))


def _modulate_kernel(x_ref, sc_ref, sh_ref, u_ref):
    u_ref[...] = (x_ref[...] * (1.0 + sc_ref[...]) + sh_ref[...]).astype(u_ref.dtype)


def _modulate(x, sc_spec, sh_spec, mods, tm):
    m, d = x.shape
    return pl.pallas_call(
        _modulate_kernel,
        grid=(m // tm,),
        in_specs=[pl.BlockSpec((tm, d), lambda i: (i, 0)), sc_spec(tm), sh_spec(tm)],
        out_specs=pl.BlockSpec((tm, d), lambda i: (i, 0)),
        out_shape=jax.ShapeDtypeStruct((m, d), BF16),
        compiler_params=_cparams(1, 32),
        name="modulate0",
    )(x, mods, mods)


def _rope_store(z, cos, sin, o_ref):
    for j in range(z.shape[1] // A_HEAD_DIM):
        zj = z[:, j * A_HEAD_DIM:(j + 1) * A_HEAD_DIM]
        o_ref[:, j * A_HEAD_DIM:(j + 1) * A_HEAD_DIM] = (
            zj * cos + pltpu.roll(zj, A_HEAD_DIM // 2, 1) * sin).astype(o_ref.dtype)


def _proj_kernel(*refs, rope, n_alias, w_mode):
    refs = refs[:len(refs) - 3 - n_alias] + refs[len(refs) - 3:]
    if rope:
        x_ref, xs_ref, w_ref, cos_ref, sin_ref, coss_ref, sins_ref, o_ref, os_ref, wb_ref = refs
    else:
        x_ref, xs_ref, w_ref, o_ref, os_ref, wb_ref = refs

    def mm(x):
        if w_mode == "nk_contract_last":
            return lax.dot_general(x, wb_ref[...], (((1,), (1,)), ((), ())), preferred_element_type=F32)
        return jnp.dot(x, wb_ref[...], preferred_element_type=F32)

    @pl.when(pl.program_id(1) == 0)
    def _():
        w = w_ref[...]
        if w_mode == "nk_transpose":
            w = w.T
        wb_ref[...] = w.astype(BF16)
        zs = mm(xs_ref[...])
        if rope:
            _rope_store(zs, coss_ref[...], sins_ref[...], os_ref)
        else:
            os_ref[...] = zs.astype(os_ref.dtype)

    z = mm(x_ref[...])
    if rope:
        _rope_store(z, cos_ref[...], sin_ref[...], o_ref)
    else:
        o_ref[...] = z.astype(o_ref.dtype)


def _swiglu_in_kernel(x_ref, xs_ref, wg_ref, wv_ref, o_ref, os_ref, wgb_ref, wvb_ref):
    def act(x):
        g = jnp.dot(x, wgb_ref[...], preferred_element_type=F32)
        v = jnp.dot(x, wvb_ref[...], preferred_element_type=F32)
        return g * _sigmoid(g) * v

    @pl.when(pl.program_id(1) == 0)
    def _():
        wgb_ref[...] = wg_ref[...].astype(BF16)
        wvb_ref[...] = wv_ref[...].astype(BF16)
        os_ref[...] = act(xs_ref[...]).astype(os_ref.dtype)

    o_ref[...] = act(x_ref[...]).astype(o_ref.dtype)


def _proj(x, xs, w, layer, col0, ncols, tm, tn, out_dtype, rope=None, stack=None, w_t=False,
          name="proj"):
    m, k = x.shape
    rows_s = xs.shape[0]
    c0 = col0 // tn
    if w_t:
        w_spec = pl.BlockSpec((None, tn, k), lambda j, i: (layer, c0 + j, 0))
        w_mode = "nk_transpose" if tn % LANES == 0 else "nk_contract_last"
    else:
        w_spec = pl.BlockSpec((None, k, tn), lambda j, i: (layer, 0, c0 + j))
        w_mode = "kn"
    wb_shape = (tn, k) if w_mode == "nk_contract_last" else (k, tn)
    in_specs = [
        pl.BlockSpec((tm, k), lambda j, i: (i, 0)),
        pl.BlockSpec((rows_s, k), lambda j, i: (0, 0)),
        w_spec,
    ]
    args = [x, xs, w]
    if rope is not None:
        period = rope[0].shape[0] // tm
        in_specs += [pl.BlockSpec((tm, A_HEAD_DIM), lambda j, i: (i % period, 0))] * 2
        in_specs += [pl.BlockSpec((rows_s, A_HEAD_DIM), lambda j, i: (0, 0))] * 2
        args += list(rope)
    aliases = {}
    if stack is None:
        out_specs = [pl.BlockSpec((tm, tn), lambda j, i: (i, j)),
                     pl.BlockSpec((rows_s, tn), lambda j, i: (0, j))]
        out_shape = [jax.ShapeDtypeStruct((m, ncols), out_dtype),
                     jax.ShapeDtypeStruct((rows_s, ncols), out_dtype)]
    else:
        n_stack, idx, prev = stack
        out_specs = [pl.BlockSpec((None, tm, tn), lambda j, i: (idx, i, j)),
                     pl.BlockSpec((None, rows_s, tn), lambda j, i: (idx, 0, j))]
        out_shape = [jax.ShapeDtypeStruct((n_stack, m, ncols), out_dtype),
                     jax.ShapeDtypeStruct((n_stack, rows_s, ncols), out_dtype)]
        if prev is not None:
            aliases = {len(args): 0, len(args) + 1: 1}
            in_specs += [pl.BlockSpec(memory_space=pl.ANY)] * 2
            args += list(prev)
    return pl.pallas_call(
        functools.partial(_proj_kernel, rope=rope is not None, n_alias=len(aliases), w_mode=w_mode),
        grid=(ncols // tn, m // tm),
        in_specs=in_specs,
        out_specs=out_specs,
        out_shape=out_shape,
        input_output_aliases=aliases,
        scratch_shapes=[pltpu.VMEM(wb_shape, BF16)],
        compiler_params=_cparams(2, 56),
        name=name,
    )(*args)


def _swiglu_in(x, xs, w, layer, tm, tn):
    m, k = x.shape
    rows_s = xs.shape[0]
    dff = w.shape[2] // 2
    nv = dff // tn
    return pl.pallas_call(
        _swiglu_in_kernel,
        grid=(dff // tn, m // tm),
        in_specs=[
            pl.BlockSpec((tm, k), lambda j, i: (i, 0)),
            pl.BlockSpec((rows_s, k), lambda j, i: (0, 0)),
            pl.BlockSpec((None, k, tn), lambda j, i: (layer, 0, j)),
            pl.BlockSpec((None, k, tn), lambda j, i: (layer, 0, nv + j)),
        ],
        out_specs=[pl.BlockSpec((tm, tn), lambda j, i: (i, j)),
                   pl.BlockSpec((rows_s, tn), lambda j, i: (0, j))],
        out_shape=[jax.ShapeDtypeStruct((m, dff), BF16), jax.ShapeDtypeStruct((rows_s, dff), BF16)],
        scratch_shapes=[pltpu.VMEM((k, tn), BF16), pltpu.VMEM((k, tn), BF16)],
        compiler_params=_cparams(2, 56),
        name="swiglu_in",
    )(x, xs, w, w)


def _cast_kernel(w_ref, o_ref):
    o_ref[...] = w_ref[...].astype(o_ref.dtype)


def _cast_bf16(w, layer, tk):
    _, k, n = w.shape
    return pl.pallas_call(
        _cast_kernel,
        grid=(k // tk,),
        in_specs=[pl.BlockSpec((None, tk, n), lambda i: (layer, i, 0))],
        out_specs=pl.BlockSpec((tk, n), lambda i: (i, 0)),
        out_shape=jax.ShapeDtypeStruct((k, n), BF16),
        compiler_params=_cparams(1, 32),
        name="cast_bf16",
    )(w)


def _out_ln_kernel(*refs, n_parts, res_w):
    ap_refs = refs[:n_parts]
    as_refs = refs[n_parts:2 * n_parts]
    (w_ref, x_ref, gate_ref, sc_ref, sh_ref, xs_ref, gates_ref, scs_ref, shs_ref, lng_ref, lnb_ref,
     xo_ref, uo_ref, xso_ref, uso_ref) = refs[2 * n_parts:]

    def run(a_refs, x_ref, gate_ref, sc_ref, sh_ref, xo_ref, uo_ref):
        acc = None
        off = 0
        for a_ref in a_refs:
            kk = a_ref.shape[1]
            part = jnp.dot(a_ref[...], w_ref[off:off + kk, :], preferred_element_type=F32)
            acc = part if acc is None else acc + part
            off += kk
        y = ALPHA * x_ref[...] + (res_w * (1.0 + gate_ref[...])) * acc
        mu = jnp.mean(y, axis=-1, keepdims=True)
        yc = y - mu
        var = jnp.mean(yc * yc, axis=-1, keepdims=True)
        xh = yc * lax.rsqrt(var + LN_EPS)
        g = lng_ref[...]
        b = lnb_ref[...]
        mod = 1.0 + sc_ref[...]
        xo_ref[...] = xh * g + b
        uo_ref[...] = (xh * (g * mod) + (b * mod + sh_ref[...])).astype(uo_ref.dtype)

    @pl.when(pl.program_id(0) == 0)
    def _():
        run(as_refs, xs_ref, gates_ref, scs_ref, shs_ref, xso_ref, uso_ref)

    run(ap_refs, x_ref, gate_ref, sc_ref, sh_ref, xo_ref, uo_ref)


def _out_ln(parts, parts_s, wb, x, xs, mods, mods_s, specs, specs_s, ln_g, ln_b, ln_row, res_w, tm):
    m = parts[0].shape[0]
    rows_s = xs.shape[0]
    k, d = wb.shape
    ln_spec = pl.BlockSpec((None, 1, d), lambda i: (ln_row, 0, 0))
    row = lambda width: pl.BlockSpec((tm, width), lambda i: (i, 0))
    whole = lambda width: pl.BlockSpec((rows_s, width), lambda i: (0, 0))
    return pl.pallas_call(
        functools.partial(_out_ln_kernel, n_parts=len(parts), res_w=res_w),
        grid=(m // tm,),
        in_specs=[row(p.shape[1]) for p in parts] + [whole(p.shape[1]) for p in parts_s] + [
            pl.BlockSpec((k, d), lambda i: (0, 0), pipeline_mode=pl.Buffered(1)),
            row(d)] + [s(tm) for s in specs] + [whole(d)] + [s(rows_s) for s in specs_s] + [
            ln_spec, ln_spec],
        out_specs=[row(d), row(d), whole(d), whole(d)],
        out_shape=[jax.ShapeDtypeStruct((m, d), F32), jax.ShapeDtypeStruct((m, d), BF16),
                   jax.ShapeDtypeStruct((rows_s, d), F32), jax.ShapeDtypeStruct((rows_s, d), BF16)],
        compiler_params=_cparams(1, 56),
        name="out_postnorm",
    )(*parts, *parts_s, wb, x, mods, mods, mods, xs, mods_s, mods_s, mods_s, ln_g, ln_b)


def _moba_prompt_kernel(q_ref, k_ref, v_ref, o_ref, kb_ref, vb_ref, kmean_ref,
                        m_ref, acc_ref, *, nblk, scale):
    qi = pl.program_id(2)
    blk = MOBA_BLOCK
    hd = A_HEAD_DIM
    nheads = q_ref.shape[1] // hd
    lanes = lambda hh: slice(hh * hd, (hh + 1) * hd)

    @pl.when(qi == 0)
    def _():
        seq = k_ref.shape[0]
        r = lax.broadcasted_iota(jnp.int32, (seq, hd), 0)
        c = lax.broadcasted_iota(jnp.int32, (seq, hd), 1)
        in_blk = (r >= c * blk) & (r < c * blk + blk)
        onehot = jnp.where(in_blk, 1.0, 0.0).astype(BF16)
        kmean_ref[...] = jnp.zeros_like(kmean_ref)
        for hh in range(nheads):
            kf = k_ref[:, lanes(hh)]
            kb_ref[hh, :, 0:hd] = kf.astype(BF16)
            kb_ref[hh, :, hd:2 * hd] = onehot
            vb_ref[hh, :, 0:hd] = v_ref[:, lanes(hh)].astype(BF16)
            vb_ref[hh, :, hd:2 * hd] = jnp.ones((seq, hd), BF16)
            kmean_ref[hh, 0:nblk, :] = jnp.mean(kf.reshape(nblk, blk, hd), axis=1)

    nt = (((1,), (1,)), ((), ()))
    start = pl.multiple_of(qi * blk, blk)
    q_augs = []
    for hh in range(nheads):
        q = q_ref[:, lanes(hh)]
        gate = lax.dot_general(kmean_ref[hh, 0:nblk, :], q, nt, precision=lax.Precision.HIGHEST,
                               preferred_element_type=F32)
        bi = lax.broadcasted_iota(jnp.int32, gate.shape, 0)
        bf = bi.astype(F32)
        g = jnp.where(bi < qi, gate, -jnp.inf)
        sel = jnp.zeros(gate.shape, F32)
        for _ in range(MOBA_TOPK):
            mx = jnp.max(g, axis=0, keepdims=True)
            cand = jnp.where(g == mx, bf, 1e9)
            cand = jnp.where(mx > -jnp.inf, cand, 1e9)
            pick = bf == jnp.min(cand, axis=0, keepdims=True)
            sel = jnp.where(pick, 1.0, sel)
            g = jnp.where(pick, -jnp.inf, g)
        qb = q.astype(BF16)
        neg = jnp.concatenate([(1.0 - sel) * NEG_BIG, jnp.zeros((hd - nblk, blk), F32)], axis=0).T
        q_augs.append(jnp.concatenate([qb, neg.astype(BF16)], axis=1))
        s = lax.dot_general(qb, kb_ref[hh, pl.ds(start, blk), 0:hd], nt, preferred_element_type=F32) * scale
        row = lax.broadcasted_iota(jnp.int32, s.shape, 0)
        col = lax.broadcasted_iota(jnp.int32, s.shape, 1)
        s = jnp.where(col <= row, s, NEG_BIG)
        m0 = jnp.max(s, axis=1, keepdims=True)
        p = jnp.exp(s - m0)
        m_ref[hh] = jnp.broadcast_to(m0, m_ref.shape[1:])
        acc_ref[hh] = jnp.dot(p.astype(BF16), vb_ref[hh, pl.ds(start, blk), :], preferred_element_type=F32)

    grp = MOBA_GROUP
    span = grp * blk

    def past_group(gi, carry):
        st = pl.multiple_of(gi * span, span)
        for hh in range(nheads):
            sn = lax.dot_general(q_augs[hh], kb_ref[hh, pl.ds(st, span), :], nt,
                                 preferred_element_type=F32) * scale
            m_prev = m_ref[hh]
            m_new = jnp.maximum(m_prev, jnp.max(sn, axis=1, keepdims=True))
            a = jnp.exp(m_prev - m_new)
            pn = jnp.exp(sn - jnp.concatenate([m_new] * (span // hd), axis=1))
            acc_ref[hh] = jnp.concatenate([a, a], axis=1) * acc_ref[hh] + jnp.dot(
                pn.astype(BF16), vb_ref[hh, pl.ds(st, span), :], preferred_element_type=F32)
            m_ref[hh] = m_new
        return carry

    lax.fori_loop(0, (qi + grp - 1) // grp, past_group, 0)
    for hh in range(nheads):
        acc = acc_ref[hh]
        o_ref[:, lanes(hh)] = (acc[:, 0:hd] / acc[:, hd:2 * hd]).astype(o_ref.dtype)


def _moba_prompt(q, k, v, slab, bsz):
    rows, width = q.shape
    seq = rows // bsz
    nh = width // A_HEAD_DIM
    nblk = seq // MOBA_BLOCK
    assert nblk % MOBA_GROUP == 0
    blk = MOBA_BLOCK
    hd = A_HEAD_DIM
    hps = MOBA_HEADS_PER_STEP
    assert nh % hps == 0
    return pl.pallas_call(
        functools.partial(_moba_prompt_kernel, nblk=nblk, scale=hd ** -0.5),
        grid=(bsz, nh // hps, nblk),
        in_specs=[
            pl.BlockSpec((blk, hps * hd), lambda b, h, i: (b * nblk + i, h)),
            pl.BlockSpec((None, seq, hps * hd), lambda b, h, i: (slab, b, h), pipeline_mode=pl.Buffered(1)),
            pl.BlockSpec((None, seq, hps * hd), lambda b, h, i: (slab, b, h), pipeline_mode=pl.Buffered(1)),
        ],
        out_specs=pl.BlockSpec((blk, hps * hd), lambda b, h, i: (b * nblk + i, h)),
        out_shape=jax.ShapeDtypeStruct((rows, width), BF16),
        scratch_shapes=[
            pltpu.VMEM((hps, seq, 2 * hd), BF16), pltpu.VMEM((hps, seq, 2 * hd), BF16),
            pltpu.VMEM((hps, LANES, hd), F32),
            pltpu.VMEM((hps, blk, hd), F32), pltpu.VMEM((hps, blk, 2 * hd), F32),
        ],
        compiler_params=_cparams(3, 52),
        name="moba_prompt",
    )(q, k, v)


def _mlstm_prompt_kernel(q_ref, k_ref, v_ref, og_ref, g_ref, gb_ref, ng_ref,
                         h_ref, c_ref, n_ref, m_ref, *, nb, nh, dk, dv):
    lc = MLSTM_CHUNK
    chunks = q_ref.shape[1] // lc

    @pl.when(pl.program_id(0) == 0)
    def _():
        c_ref[...] = jnp.zeros_like(c_ref)
        n_ref[...] = jnp.zeros_like(n_ref)
        m_ref[...] = jnp.zeros_like(m_ref)

    tt = lax.broadcasted_iota(jnp.int32, (lc, lc), 0)
    ss = lax.broadcasted_iota(jnp.int32, (lc, lc), 1)
    causal = ss <= tt
    eye = ss == tt
    nt = (((1,), (1,)), ((), ()))
    tn = (((0,), (0,)), ((), ()))
    gb = gb_ref[...]
    for cc in range(chunks):
        rows = slice(cc * lc, (cc + 1) * lc)
        for b in range(nb):
            gates = g_ref[b, rows, :] + gb
            log_f = jnp.minimum(gates, 0.0) - jnp.log1p(jnp.exp(-jnp.abs(gates)))
            for h in range(nh):
                bh = b * nh + h
                i_col = gates[:, h:h + 1]
                f_col = log_f[:, nh + h:nh + h + 1]
                f_row = jnp.sum(jnp.where(eye, f_col, 0.0), axis=0, keepdims=True)
                i_row = jnp.sum(jnp.where(eye, i_col, 0.0), axis=0, keepdims=True)
                b_col = jnp.sum(jnp.where(causal, f_row, 0.0), axis=1, keepdims=True)
                b_row = jnp.sum(jnp.where(ss >= tt, f_col, 0.0), axis=0, keepdims=True)
                m_prev = m_ref[bh][:, 0:1]
                d = jnp.where(causal, b_col - b_row + i_row, NEG_BIG)
                inter = b_col + m_prev
                m_t = jnp.maximum(jnp.max(d, axis=1, keepdims=True), inter)
                w = jnp.exp(d - m_t)
                q = q_ref[b, rows, h * dk:(h + 1) * dk]
                k = k_ref[b, rows, h * dk:(h + 1) * dk] * (dk ** -0.5)
                vb = v_ref[b, rows, h * dv:(h + 1) * dv].astype(BF16)
                qb = q.astype(BF16)
                s = lax.dot_general(qb, k.astype(BF16), nt, preferred_element_type=F32) * w
                wi = jnp.exp(inter - m_t)
                cst = c_ref[bh]
                nrow = n_ref[bh]
                num = wi * jnp.dot(qb, cst.astype(BF16), preferred_element_type=F32) + jnp.dot(
                    s.astype(BF16), vb, preferred_element_type=F32)
                den = wi * jnp.sum(q * nrow, axis=1, keepdims=True) + jnp.sum(s, axis=1, keepdims=True)
                hh = num / jnp.maximum(jnp.abs(den), jnp.exp(-m_t))
                b_last = b_col[lc - 1:lc, :]
                g_col = b_last - b_col + i_col
                m_new = jnp.maximum(b_last + m_prev, jnp.max(g_col, axis=0, keepdims=True))
                wc = jnp.exp(b_last + m_prev - m_new)
                kw = k * jnp.exp(g_col - m_new)
                c_ref[bh] = wc * cst + lax.dot_general(kw.astype(BF16), vb, tn, preferred_element_type=F32)
                n_ref[bh] = wc * nrow + jnp.sum(kw, axis=0, keepdims=True)
                m_ref[bh] = jnp.broadcast_to(m_new, (1, LANES))
                hn = hh * lax.rsqrt(jnp.mean(hh * hh, axis=1, keepdims=True) + LN_EPS)
                hn = hn * ng_ref[:, h * dv:(h + 1) * dv] * _sigmoid(og_ref[b, rows, h * dv:(h + 1) * dv])
                h_ref[b, rows, h * dv:(h + 1) * dv] = hn.astype(h_ref.dtype)


def _mlstm_prompt(zm, gates, gate_bias, norm_g, bsz):
    rows, _ = zm.shape
    seq = rows // bsz
    nh = B_HEADS
    v_w = norm_g.shape[1]
    qk_w = v_w // 2
    dk, dv = qk_w // nh, v_w // nh
    lc = MLSTM_CHUNK * MLSTM_CHUNKS_PER_STEP
    assert seq % lc == 0
    z3 = zm.reshape(bsz, seq, zm.shape[1])
    gw = gates.shape[1]
    g3 = gates.reshape(bsz, seq, gw)
    nbh = bsz * nh
    full = lambda shape: pl.BlockSpec(shape, lambda c: (0,) * len(shape))
    return pl.pallas_call(
        functools.partial(_mlstm_prompt_kernel, nb=bsz, nh=nh, dk=dk, dv=dv),
        grid=(seq // lc,),
        in_specs=[
            pl.BlockSpec((bsz, lc, qk_w), lambda c: (0, c, 0)),
            pl.BlockSpec((bsz, lc, qk_w), lambda c: (0, c, 1)),
            pl.BlockSpec((bsz, lc, v_w), lambda c: (0, c, 1)),
            pl.BlockSpec((bsz, lc, v_w), lambda c: (0, c, 2)),
            pl.BlockSpec((bsz, lc, gw), lambda c: (0, c, 0)),
            full((1, gw)), full((1, v_w)),
        ],
        out_specs=[
            pl.BlockSpec((bsz, lc, v_w), lambda c: (0, c, 0)),
            full((nbh, dk, dv)), full((nbh, 1, dk)), full((nbh, 1, LANES)),
        ],
        out_shape=[
            jax.ShapeDtypeStruct((bsz, seq, v_w), BF16),
            jax.ShapeDtypeStruct((nbh, dk, dv), F32),
            jax.ShapeDtypeStruct((nbh, 1, dk), F32),
            jax.ShapeDtypeStruct((nbh, 1, LANES), F32),
        ],
        compiler_params=_cparams(1, 32),
        name="mlstm_prompt",
    )(z3, z3, z3, z3, g3, gate_bias, norm_g)


def _rglru_gates(xconv, wa_ref, ba, wx_ref, bx, lam):
    bd = wa_ref.shape[1]
    r_parts, i_parts = [], []
    for n in range(wa_ref.shape[0]):
        xb = xconv[:, n * bd:(n + 1) * bd].astype(BF16)
        r_parts.append(jnp.dot(xb, wa_ref[n].astype(BF16), preferred_element_type=F32))
        i_parts.append(jnp.dot(xb, wx_ref[n].astype(BF16), preferred_element_type=F32))
    r = _sigmoid(jnp.concatenate(r_parts, axis=1) + ba)
    ig = _sigmoid(jnp.concatenate(i_parts, axis=1) + bx)
    log_a = (-RG_C) * r * _softplus(-lam)
    a = jnp.exp(log_a)
    mult = jnp.sqrt(1.0 - a * a)
    return a, mult * ig * xconv


def _rglru_prompt_kernel(gate_ref, xr_ref, cw_ref, cb_ref, wa_ref, ba_ref, wx_ref, bx_ref, lam_ref,
                         y_ref, hl_ref, buf_ref, xext_ref, a_ref, b_ref, hs_ref, hc_ref):
    t = pl.program_id(1)
    tt = xr_ref.shape[0]
    pad = 8
    nbuf = CONV_WIDTH - 1

    @pl.when(t == 0)
    def _():
        xext_ref[0:pad, :] = jnp.zeros((pad, xext_ref.shape[1]), F32)
        hc_ref[...] = jnp.zeros_like(hc_ref)

    @pl.when(t > 0)
    def _():
        xext_ref[0:pad, :] = xext_ref[tt:tt + pad, :]

    xr = xr_ref[...]
    xext_ref[pad:pad + tt, :] = xr
    xconv = cb_ref[...] + xext_ref[pad - nbuf:pad - nbuf + tt, :] * cw_ref[0:1, :]
    for j in range(1, CONV_WIDTH):
        xconv = xconv + xext_ref[pad - nbuf + j:pad - nbuf + j + tt, :] * cw_ref[j:j + 1, :]
    a, bterm = _rglru_gates(xconv, wa_ref, ba_ref[...], wx_ref, bx_ref[...], lam_ref[...])
    a_ref[...] = a
    b_ref[...] = bterm

    def step(i, h):
        h = a_ref[pl.ds(i, 1), :] * h + b_ref[pl.ds(i, 1), :]
        hs_ref[pl.ds(i, 1), :] = h
        return h

    h_last = lax.fori_loop(0, tt, step, hc_ref[...], unroll=8)
    hc_ref[...] = h_last
    y_ref[...] = (_gelu_tanh(gate_ref[...]) * hs_ref[...]).astype(y_ref.dtype)
    hl_ref[...] = h_last
    buf_ref[...] = xr[tt - nbuf:tt, :]


def _rglru_prompt(zr, layer, conv_w, conv_b, w_a, b_a, w_x, b_x, lam, bsz, tt):
    rows, w2 = zr.shape
    width = w2 // 2
    seq = rows // bsz
    nt = seq // tt
    nblk, bd = w_a.shape[1], w_a.shape[2]
    vec = lambda a: a.reshape(a.shape[0], 1, width)
    vspec = pl.BlockSpec((None, 1, width), lambda b, t: (layer, 0, 0))
    wspec = pl.BlockSpec((None, nblk, bd, bd), lambda b, t: (layer, 0, 0, 0))
    nbuf = CONV_WIDTH - 1
    return pl.pallas_call(
        _rglru_prompt_kernel,
        grid=(bsz, nt),
        in_specs=[
            pl.BlockSpec((tt, width), lambda b, t: (b * nt + t, 0)),
            pl.BlockSpec((tt, width), lambda b, t: (b * nt + t, 1)),
            pl.BlockSpec((None, CONV_WIDTH, width), lambda b, t: (layer, 0, 0)),
            vspec, wspec, vspec, wspec, vspec, vspec,
        ],
        out_specs=[
            pl.BlockSpec((tt, width), lambda b, t: (b * nt + t, 0)),
            pl.BlockSpec((None, 1, width), lambda b, t: (b, 0, 0)),
            pl.BlockSpec((None, nbuf, width), lambda b, t: (b, 0, 0)),
        ],
        out_shape=[
            jax.ShapeDtypeStruct((rows, width), BF16),
            jax.ShapeDtypeStruct((bsz, 1, width), F32),
            jax.ShapeDtypeStruct((bsz, nbuf, width), F32),
        ],
        scratch_shapes=[
            pltpu.VMEM((tt + 8, width), F32), pltpu.VMEM((tt, width), F32),
            pltpu.VMEM((tt, width), F32), pltpu.VMEM((tt, width), F32), pltpu.VMEM((1, width), F32),
        ],
        compiler_params=_cparams(2, 48),
        name="rglru_prompt",
    )(zr, zr, conv_w, vec(conv_b), w_a, vec(b_a), w_x, vec(b_x), vec(lam))


def _moba_gate_kernel(tbl_ref, q_ref, *refs, nheads, nsteps, pages_per_step, pages_per_blk):
    b = pl.program_id(0)
    st = pl.program_id(1)
    hd = A_HEAD_DIM
    k_refs = refs[:pages_per_step]
    sel_ref, ksum_ref = refs[pages_per_step:]
    page = k_refs[0].shape[0]
    blks_per_step = pages_per_step // pages_per_blk
    for i in range(blks_per_step):
        tot = jnp.sum(k_refs[i * pages_per_blk][...], axis=0)
        for u in range(1, pages_per_blk):
            tot = tot + jnp.sum(k_refs[i * pages_per_blk + u][...], axis=0)
        ksum_ref[st * blks_per_step + i] = tot

    @pl.when(st == nsteps - 1)
    def _():
        nblk = ksum_ref.shape[0]
        inv = 1.0 / (pages_per_blk * page)
        rowi = lax.broadcasted_iota(jnp.int32, (nblk, 1), 0).astype(F32)
        out_r = lax.broadcasted_iota(jnp.int32, sel_ref.shape, 0)
        out_c = lax.broadcasted_iota(jnp.int32, sel_ref.shape, 1)
        out = jnp.zeros(sel_ref.shape, jnp.int32)
        qrow = q_ref[pl.ds(b, 1), :]
        for h in range(nheads):
            km = ksum_ref[:, h, :] * inv
            g = jnp.sum(km * qrow[:, h * hd:(h + 1) * hd], axis=1, keepdims=True)
            for i in range(MOBA_TOPK):
                mx = jnp.max(g, axis=0, keepdims=True)
                idx = jnp.min(jnp.where(g == mx, rowi, 1e9), axis=0, keepdims=True)
                out = jnp.where((out_r == h) & (out_c == i), idx.astype(jnp.int32), out)
                g = jnp.where(rowi == idx, -jnp.inf, g)
        sel_ref[...] = out


GATE_PAGES_PER_STEP = 16


def _moba_sample_gate(q, cache_k, layer, tbl_flat, nreq, npages):
    page, nheads, hd = cache_k.shape[2:]
    ppb = MOBA_BLOCK // page
    nblk = npages // ppb
    pps = GATE_PAGES_PER_STEP
    nsteps = npages // pps

    def page_spec(u):
        return pl.BlockSpec((None, None, page, nheads, hd),
                            lambda b, s, tbl: (layer, tbl[b * npages + s * pps + u], 0, 0, 0))

    grid_spec = pltpu.PrefetchScalarGridSpec(
        num_scalar_prefetch=1,
        grid=(nreq, nsteps),
        in_specs=[pl.BlockSpec(q.shape, lambda b, s, tbl: (0, 0))] + [page_spec(u) for u in range(pps)],
        out_specs=pl.BlockSpec((None, nheads, LANES), lambda b, s, tbl: (b, 0, 0)),
        scratch_shapes=[pltpu.VMEM((nblk, nheads, hd), F32)],
    )
    return pl.pallas_call(
        functools.partial(_moba_gate_kernel, nheads=nheads, nsteps=nsteps, pages_per_step=pps,
                          pages_per_blk=ppb),
        grid_spec=grid_spec,
        out_shape=jax.ShapeDtypeStruct((nreq, nheads, LANES), jnp.int32),
        compiler_params=_cparams(2, 32),
        name="moba_sample_gate",
    )(tbl_flat, q, *([cache_k] * pps))


def _moba_sample_attn_kernel(tbl_ref, sel_ref, q_ref, kn_ref, vn_ref, *refs, npg, hps, scale):
    o_ref = refs[2 * npg * hps]
    b = pl.program_id(0)
    nheads = q_ref.shape[1]
    hd = q_ref.shape[2]
    for hh in range(hps):
        k_refs = refs[hh * npg:(hh + 1) * npg]
        v_refs = refs[(hps + hh) * npg:(hps + hh + 1) * npg]
        h = pl.program_id(1) * hps + hh
        is_h = lax.broadcasted_iota(jnp.int32, (1, nheads, 1), 1) == h
        q_m = jnp.where(is_h, q_ref[b][None], 0.0)

        def score(kp):
            part = jnp.sum(kp * q_m, axis=2, keepdims=True)
            return jnp.sum(part, axis=1, keepdims=True) * scale

        s_own = score(kn_ref[b][None])
        ss = [score(k_ref[...]) for k_ref in k_refs]
        m = s_own
        for s in ss:
            m = jnp.maximum(m, jnp.max(s, axis=0, keepdims=True))
        l = jnp.exp(s_own - m)
        acc = l * vn_ref[b][None]
        for s, v_ref in zip(ss, v_refs):
            p = jnp.exp(s - m)
            l = l + jnp.sum(p, axis=0, keepdims=True)
            acc = acc + jnp.sum(p * v_ref[...], axis=0, keepdims=True)
        out = jnp.sum(jnp.where(is_h, acc / l, 0.0), axis=1)
        o_ref[:, hh * hd:(hh + 1) * hd] = out.astype(o_ref.dtype)


SAMPLE_ATTN_HEADS_PER_STEP = 2


def _moba_sample_attn(q, k_new, v_new, cache_k, cache_v, layer, tbl_flat, sel_flat, nreq, npages):
    page, nheads, hd = cache_k.shape[2:]
    ppb = MOBA_BLOCK // page
    npg = MOBA_TOPK * ppb
    hps = SAMPLE_ATTN_HEADS_PER_STEP
    assert nheads % hps == 0

    def page_spec(hh, j):
        def page_map(b, g, tbl, sel):
            blk = sel[(b * nheads + g * hps + hh) * LANES + j // ppb]
            return (layer, tbl[b * npages + blk * ppb + j % ppb], 0, 0, 0)
        return pl.BlockSpec((None, None, page, nheads, hd), page_map)

    head = pl.BlockSpec(q.shape, lambda b, g, tbl, sel: (0, 0, 0))
    pages = [page_spec(hh, j) for hh in range(hps) for j in range(npg)]
    grid_spec = pltpu.PrefetchScalarGridSpec(
        num_scalar_prefetch=2,
        grid=(nreq, nheads // hps),
        in_specs=[head, head, head] + pages + pages,
        out_specs=pl.BlockSpec((None, 1, hps * hd), lambda b, g, tbl, sel: (b, 0, g)),
    )
    n_in = npg * hps
    return pl.pallas_call(
        functools.partial(_moba_sample_attn_kernel, npg=npg, hps=hps, scale=hd ** -0.5),
        grid_spec=grid_spec,
        out_shape=jax.ShapeDtypeStruct((nreq, 1, nheads * hd), F32),
        compiler_params=_cparams(2, 48),
        name="moba_sample_attn",
    )(tbl_flat, sel_flat, q, k_new, v_new, *([cache_k] * n_in), *([cache_v] * n_in))


def _mlstm_sample_kernel(q_ref, k_ref, v_ref, og_ref, g_ref, gb_ref, ng_ref, c0_ref, n0_ref, m0_ref,
                         h_ref, c_ref, n_ref, m_ref, *, nh, dk, dv):
    b = pl.program_id(0)
    rr = lax.broadcasted_iota(jnp.int32, (dk, dk), 0)
    cc = lax.broadcasted_iota(jnp.int32, (dk, dk), 1)
    eye = rr == cc
    gates = g_ref[pl.ds(b, 1), :] + gb_ref[...]
    q_all = q_ref[pl.ds(b, 1), :]
    k_all = k_ref[pl.ds(b, 1), :]
    v_all = v_ref[pl.ds(b, 1), :]
    og_all = og_ref[pl.ds(b, 1), :]
    for h in range(nh):
        ii = gates[:, h:h + 1]
        fpre = gates[:, nh + h:nh + h + 1]
        ff = jnp.minimum(fpre, 0.0) - jnp.log1p(jnp.exp(-jnp.abs(fpre)))
        m_prev = m0_ref[h][:, 0:1]
        q = q_all[:, h * dk:(h + 1) * dk]
        k = k_all[:, h * dk:(h + 1) * dk] * (dk ** -0.5)
        v = v_all[:, h * dv:(h + 1) * dv]
        inter = ff + m_prev
        m_t = jnp.maximum(ii, inter)
        s = jnp.sum(q * k, axis=1, keepdims=True) * jnp.exp(ii - m_t)
        wi = jnp.exp(inter - m_t)
        q_col = jnp.sum(jnp.where(eye, q, 0.0), axis=1, keepdims=True)
        k_col = jnp.sum(jnp.where(eye, k, 0.0), axis=1, keepdims=True)
        cst = c0_ref[h]
        nrow = n0_ref[h]
        num = wi * jnp.sum(q_col * cst, axis=0, keepdims=True) + s * v
        den = wi * jnp.sum(q * nrow, axis=1, keepdims=True) + s
        hh = num / jnp.maximum(jnp.abs(den), jnp.exp(-m_t))
        m_new = m_t
        wc = jnp.exp(inter - m_new)
        wg = jnp.exp(ii - m_new)
        c_ref[h] = wc * cst + (wg * k_col) * v
        n_ref[h] = wc * nrow + wg * k
        m_ref[h] = jnp.broadcast_to(m_new, (1, LANES))
        hn = hh * lax.rsqrt(jnp.mean(hh * hh, axis=1, keepdims=True) + LN_EPS)
        hn = hn * ng_ref[:, h * dv:(h + 1) * dv] * _sigmoid(og_all[:, h * dv:(h + 1) * dv])
        h_ref[:, h * dv:(h + 1) * dv] = hn.astype(h_ref.dtype)


def _mlstm_sample(zm, gates, gate_bias, norm_g, c0, n0, m0, layer, nreq):
    nh = B_HEADS
    v_w = norm_g.shape[1]
    qk_w = v_w // 2
    dk, dv = qk_w // nh, v_w // nh
    rows = zm.shape[0]
    full = lambda shape: pl.BlockSpec(shape, lambda b: (0,) * len(shape))
    return pl.pallas_call(
        functools.partial(_mlstm_sample_kernel, nh=nh, dk=dk, dv=dv),
        grid=(nreq,),
        in_specs=[
            pl.BlockSpec((rows, qk_w), lambda b: (0, 0)),
            pl.BlockSpec((rows, qk_w), lambda b: (0, 1)),
            pl.BlockSpec((rows, v_w), lambda b: (0, 1)),
            pl.BlockSpec((rows, v_w), lambda b: (0, 2)),
            full(gates.shape), full(gate_bias.shape), full((1, v_w)),
            pl.BlockSpec((None, None, nh, dk, dv), lambda b: (layer, b, 0, 0, 0)),
            pl.BlockSpec((None, None, nh, 1, dk), lambda b: (layer, b, 0, 0, 0)),
            pl.BlockSpec((None, None, nh, 1, LANES), lambda b: (layer, b, 0, 0, 0)),
        ],
        out_specs=[
            pl.BlockSpec((None, 1, v_w), lambda b: (b, 0, 0)),
            pl.BlockSpec((None, nh, dk, dv), lambda b: (b, 0, 0, 0)),
            pl.BlockSpec((None, nh, 1, dk), lambda b: (b, 0, 0, 0)),
            pl.BlockSpec((None, nh, 1, LANES), lambda b: (b, 0, 0, 0)),
        ],
        out_shape=[
            jax.ShapeDtypeStruct((nreq, 1, v_w), F32),
            jax.ShapeDtypeStruct((nreq, nh, dk, dv), F32),
            jax.ShapeDtypeStruct((nreq, nh, 1, dk), F32),
            jax.ShapeDtypeStruct((nreq, nh, 1, LANES), F32),
        ],
        compiler_params=_cparams(1, 32),
        name="mlstm_sample",
    )(zm, zm, zm, zm, gates, gate_bias, norm_g, c0, n0, m0)


def _rglru_sample_kernel(gate_ref, xr_ref, buf_ref, h0_ref, cw_ref, cb_ref, wa_ref, ba_ref, wx_ref,
                         bx_ref, lam_ref, y_ref, h_ref, nbuf_ref):
    nbuf = CONV_WIDTH - 1
    xr = xr_ref[...]
    xconv = cb_ref[...] + buf_ref[0] * cw_ref[0:1, :]
    for j in range(1, nbuf):
        xconv = xconv + buf_ref[j] * cw_ref[j:j + 1, :]
    xconv = xconv + xr * cw_ref[nbuf:nbuf + 1, :]
    a, bterm = _rglru_gates(xconv, wa_ref, ba_ref[...], wx_ref, bx_ref[...], lam_ref[...])
    h = a * h0_ref[...] + bterm
    h_ref[...] = h
    y = _gelu_tanh(gate_ref[...]) * h
    pad_rows = y_ref.shape[0] - y.shape[0]
    y_ref[...] = jnp.concatenate([y, jnp.zeros((pad_rows, y.shape[1]), F32)], axis=0).astype(y_ref.dtype)
    for j in range(nbuf - 1):
        nbuf_ref[j] = buf_ref[j + 1]
    nbuf_ref[nbuf - 1] = xr


def _rglru_sample(zr, layer, buf_t, h0, conv_w, conv_b, w_a, b_a, w_x, b_x, lam, nreq):
    rows, w2 = zr.shape
    width = w2 // 2
    nblk, bd = w_a.shape[1], w_a.shape[2]
    nbuf = CONV_WIDTH - 1
    vec = lambda a: a.reshape(a.shape[0], 1, width)
    vspec = pl.BlockSpec((None, 1, width), lambda i: (layer, 0, 0))
    wspec = pl.BlockSpec((None, nblk, bd, bd), lambda i: (layer, 0, 0, 0))
    return pl.pallas_call(
        _rglru_sample_kernel,
        grid=(1,),
        in_specs=[
            pl.BlockSpec((nreq, width), lambda i: (0, 0)),
            pl.BlockSpec((nreq, width), lambda i: (0, 1)),
            pl.BlockSpec((None, nbuf, nreq, width), lambda i: (layer, 0, 0, 0)),
            pl.BlockSpec((None, nreq, width), lambda i: (layer, 0, 0)),
            pl.BlockSpec((None, CONV_WIDTH, width), lambda i: (layer, 0, 0)),
            vspec, wspec, vspec, wspec, vspec, vspec,
        ],
        out_specs=[
            pl.BlockSpec((rows, width), lambda i: (0, 0)),
            pl.BlockSpec((nreq, width), lambda i: (0, 0)),
            pl.BlockSpec((nbuf, nreq, width), lambda i: (0, 0, 0)),
        ],
        out_shape=[
            jax.ShapeDtypeStruct((rows, width), BF16),
            jax.ShapeDtypeStruct((nreq, width), F32),
            jax.ShapeDtypeStruct((nbuf, nreq, width), F32),
        ],
        compiler_params=_cparams(1, 32),
        name="rglru_sample",
    )(zr, zr, buf_t, h0, conv_w, vec(conv_b), w_a, vec(b_a), w_x, vec(b_x), vec(lam))


def _rope_tables(pos):
    half = A_HEAD_DIM // 2
    inv = ROPE_THETA ** (-jnp.arange(half, dtype=F32) / half)
    ang = pos.astype(F32)[:, None] * inv[None, :]
    cos, sin = jnp.cos(ang), jnp.sin(ang)
    return jnp.concatenate([cos, cos], axis=1), jnp.concatenate([-sin, sin], axis=1)


def kernel(x_prompt, x_sample, cache_k, cache_v, state_mlstm_c, state_mlstm_n, state_mlstm_m,
           state_rglru_h, state_conv, page_table, c_prompt, c_sample, w_ada, b_ada, ln_g, ln_b,
           w_ffn1_in, w_ffn1_out, w_ffn2_in, w_ffn2_out, w_in_ab, b_igate, b_fgate, mlstm_norm_g,
           w_out_ab, w_in_rg, conv_w, conv_b, w_rg_a, b_rg_a, w_rg_x, b_rg_x, lru_lambda, w_out_rg):
    bsz, seq, d = x_prompt.shape
    nreq = x_sample.shape[0]
    depth = w_ada.shape[0]
    n_even = w_in_ab.shape[0]
    a_w = cache_k.shape[3] * cache_k.shape[4]
    v_w = mlstm_norm_g.shape[1]
    qk_w = v_w // 2
    m_w = 2 * qk_w + 2 * v_w
    nh_b = B_HEADS
    past_len = page_table.shape[1] * cache_k.shape[2]
    npages = page_table.shape[1]
    rows_p = bsz * seq
    rs = SAMPLE_ROWS

    tm = 1024
    tn = 512
    tm_p = 1024
    tn_p = 1024
    tm_out = 256
    tk_cast = 512
    tt_rg = 256

    c_all = jnp.concatenate([c_prompt, c_sample, jnp.zeros((rs - bsz - nreq, d), F32)], axis=0)
    mods = _mods(c_all, w_ada, b_ada).reshape(depth, rs, 9, d)
    mods_p = jnp.transpose(mods[:, :bsz], (0, 2, 1, 3)).reshape(depth * 9 * bsz, 1, d)
    mods_s = jnp.transpose(mods[:, bsz:bsz + nreq], (0, 2, 1, 3))
    mods_s = jnp.pad(mods_s, ((0, 0), (0, 0), (0, rs - nreq), (0, 0))).reshape(depth * 9, rs, d)

    def mod_spec_p(l, s, j):
        base = ((l * 3 + s) * 3 + j) * bsz

        def make(tile_rows):
            tiles_per_batch = seq // tile_rows
            return pl.BlockSpec((None, 1, d), lambda i: (base + i // tiles_per_batch, 0, 0))
        return make

    def mod_spec_s(l, s, j):
        row = (l * 3 + s) * 3 + j

        def make(tile_rows):
            return pl.BlockSpec((None, rs, d), lambda i: (row, 0, 0))
        return make

    ln_g3 = ln_g.reshape(depth * 3, 1, d)
    ln_b3 = ln_b.reshape(depth * 3, 1, d)

    cos_p, sin_p = _rope_tables(jnp.arange(seq, dtype=jnp.int32))
    cos_s, sin_s = _rope_tables(jnp.full((rs,), past_len, dtype=jnp.int32))

    tbl_flat = page_table.reshape(-1).astype(jnp.int32)
    n0_s = state_mlstm_n[:, :, :, None, :]
    m0_s = jnp.broadcast_to(state_mlstm_m[:, :, :, None, None], state_mlstm_m.shape + (1, LANES))
    conv_t = jnp.transpose(state_conv, (0, 2, 1, 3))
    gate_bias = jnp.concatenate([b_igate, b_fgate], axis=1)
    w_ab_t = jnp.swapaxes(w_in_ab, 1, 2)

    xp = x_prompt.reshape(rows_p, d)
    xs = jnp.pad(x_sample.reshape(nreq, d), ((0, rs - nreq), (0, 0)))
    up = _modulate(xp, mod_spec_p(0, 0, 1), mod_spec_p(0, 0, 0), mods_p, tm)
    us = _modulate(xs, mod_spec_s(0, 0, 1), mod_spec_s(0, 0, 0), mods_s, rs)

    op = dict(k=[], v=[], c=[], n=[], m=[], h=[], buf=[])
    os_ = dict(k=[], v=[], c=[], n=[], m=[], h=[], buf=[])
    rope = (cos_p, sin_p, cos_s, sin_s)
    k_st = v_st = None
    pad_rows = lambda t, w: jnp.pad(t.reshape(nreq, w), ((0, rs - nreq), (0, 0))).astype(BF16)
    heads3 = lambda t: t.reshape(rs, -1, A_HEAD_DIM)

    for l in range(depth):
        wb_ffn1 = _cast_bf16(w_ffn1_out, l, tk_cast)
        wb_ffn2 = _cast_bf16(w_ffn2_out, l, tk_cast)
        wb_mix = _cast_bf16(w_out_ab if l % 2 == 0 else w_out_rg, l // 2, tk_cast)

        def post(parts_p, parts_s, wb, s, res_w, nxt):
            nl, ns = nxt
            pick = lambda spec: [spec(l, s, 2), spec(nl, ns, 1), spec(nl, ns, 0)]
            xo, uo, xso, uso = _out_ln(parts_p, parts_s, wb, xp, xs, mods_p, mods_s, pick(mod_spec_p),
                                       pick(mod_spec_s), ln_g3, ln_b3, l * 3 + s, res_w,
                                       tm_out if wb.shape[0] > d else 2 * tm_out)
            return (xo, uo), (xso, uso)

        act_p, act_s = _swiglu_in(up, us, w_ffn1_in, l, tm, tn)
        (xp, up), (xs, us) = post([act_p], [act_s], wb_ffn1, 0, FFN_RES, (l, 1))

        if l % 2 == 0:
            e = l // 2
            q_p, q_s = _proj(up, us, w_ab_t, e, 0, a_w, tm_p, tn_p, F32, rope=rope, w_t=True, name="proj_q")
            k_st = _proj(up, us, w_ab_t, e, a_w, a_w, tm_p, tn_p, F32, rope=rope, w_t=True,
                         stack=(n_even, e, k_st), name="proj_k")
            v_st = _proj(up, us, w_ab_t, e, 2 * a_w, a_w, tm_p, tn_p, F32, w_t=True,
                         stack=(n_even, e, v_st), name="proj_v")
            k_s, v_s = k_st[1][e], v_st[1][e]
            zm_p, zm_s = _proj(up, us, w_ab_t, e, 3 * a_w, m_w, tm_p, tn_p, F32, w_t=True, name="proj_mlstm")
            n_gates = 2 * nh_b
            g_p, g_s = _proj(up, us, w_ab_t, e, 3 * a_w + m_w, n_gates, tm_p, n_gates, F32, w_t=True,
                             name="proj_gates")
            gb = gate_bias[e:e + 1]
            ng = mlstm_norm_g[e:e + 1]

            oa = _moba_prompt(q_p, k_st[0], v_st[0], e, bsz)
            hm, c1, n1, m1 = _mlstm_prompt(zm_p, g_p, gb, ng, bsz)
            mix_p = [oa, hm.reshape(rows_p, v_w)]
            op["c"].append(c1.reshape(bsz, nh_b, c1.shape[1], c1.shape[2]))
            op["n"].append(n1.reshape(bsz, nh_b, -1))
            op["m"].append(m1[:, 0, 0].reshape(bsz, nh_b))

            sel = _moba_sample_gate(q_s, cache_k, e, tbl_flat, nreq, npages)
            oa = _moba_sample_attn(heads3(q_s), heads3(k_s), heads3(v_s), cache_k, cache_v, e,
                                   tbl_flat, sel.reshape(-1), nreq, npages)
            hm, c1, n1, m1 = _mlstm_sample(zm_s, g_s, gb, ng, state_mlstm_c, n0_s, m0_s, e, nreq)
            mix_s = [pad_rows(oa, a_w), pad_rows(hm, v_w)]
            os_["c"].append(c1)
            os_["n"].append(n1.reshape(nreq, nh_b, -1))
            os_["m"].append(m1[:, :, 0, 0])
        else:
            od = l // 2
            zr_p, zr_s = _proj(up, us, w_in_rg, od, 0, w_in_rg.shape[2], tm_p, tn_p, F32, name="proj_rg")
            y_p, h1, nb = _rglru_prompt(zr_p, od, conv_w, conv_b, w_rg_a, b_rg_a, w_rg_x, b_rg_x,
                                        lru_lambda, bsz, tt_rg)
            op["h"].append(h1.reshape(bsz, -1))
            op["buf"].append(nb)
            y_s, h1, nb = _rglru_sample(zr_s, od, conv_t, state_rglru_h, conv_w, conv_b, w_rg_a,
                                        b_rg_a, w_rg_x, b_rg_x, lru_lambda, nreq)
            os_["h"].append(h1)
            os_["buf"].append(jnp.transpose(nb, (1, 0, 2)))
            mix_p, mix_s = [y_p], [y_s]
        (xp, up), (xs, us) = post(mix_p, mix_s, wb_mix, 1, 1.0, (l, 2))

        act_p, act_s = _swiglu_in(up, us, w_ffn2_in, l, tm, tn)
        (xp, up), (xs, us) = post([act_p], [act_s], wb_ffn2, 2, FFN_RES, (min(l + 1, depth - 1), 0))

    st = jnp.stack
    kv_p = lambda t: t.reshape(n_even, bsz, seq, -1, A_HEAD_DIM)
    kv_s = lambda t: t[:, :nreq].reshape(n_even, nreq, 1, -1, A_HEAD_DIM)
    return (xp.reshape(bsz, seq, d), xs[:nreq].reshape(nreq, 1, d),
            kv_p(k_st[0]), kv_p(v_st[0]), kv_s(k_st[1]), kv_s(v_st[1]),
            st(op["c"]), st(op["n"]), st(op["m"]), st(os_["c"]), st(os_["n"]), st(os_["m"]),
            st(op["h"]), st(op["buf"]), st(os_["h"]), st(os_["buf"]))
```

```python
import functools

import numpy as np
import jax
import jax.numpy as jnp
from jax import lax
from jax.experimental import pallas as pl
from jax.experimental.pallas import tpu as pltpu

F32 = jnp.float32
BF16 = jnp.bfloat16

DEPTH = 4
MOBA_BLOCK = 256
MOBA_TOPK = 3
MOBA_GROUP = 4
MOBA_HEADS_PER_STEP = 4
MLSTM_CHUNKS_PER_STEP = 1
ROPE_THETA = 10000.0
A_HEAD_DIM = 128
B_HEADS = 4
MLSTM_CHUNK = 64
RG_BLOCKS = 16
CONV_WIDTH = 4
RG_C = 8.0
FFN_RES = 0.5
ALPHA = (2.0 * DEPTH) ** 0.25
LN_EPS = 1e-5

LANES = 128
SAMPLE_ROWS = 16
NEG_BIG = -1e30
MIB = 1024 * 1024


def _cparams(n_axes, vmem_mib):
    return pltpu.CompilerParams(
        dimension_semantics=("arbitrary",) * n_axes,
        vmem_limit_bytes=int(vmem_mib * MIB),
    )


def _sigmoid(x):
    return 0.5 * jnp.tanh(0.5 * x) + 0.5


def _softplus(x):
    return jnp.maximum(x, 0.0) + jnp.log1p(jnp.exp(-jnp.abs(x)))


def _gelu_tanh(x):
    return 0.5 * x * (1.0 + jnp.tanh(0.7978845608028654 * (x + 0.044715 * x * x * x)))


def _mods_kernel(c_ref, w_ref, b_ref, o_ref):
    c = c_ref[...]
    s = (c * _sigmoid(c)).astype(BF16)
    o_ref[...] = jnp.dot(s, w_ref[...].astype(BF16), preferred_element_type=F32) + b_ref[...]


def _mods(c_all, w_ada, b_ada):
    depth, k, n = w_ada.shape
    rows = c_all.shape[0]
    tn = 1024 if n % 1024 == 0 else n
    return pl.pallas_call(
        _mods_kernel,
        grid=(depth, n // tn),
        in_specs=[
            pl.BlockSpec((rows, k), lambda l, j: (0, 0)),
            pl.BlockSpec((None, k, tn), lambda l, j: (l, 0, j)),
            pl.BlockSpec((None, 1, tn), lambda l, j: (l, 0, j)),
        ],
        out_specs=pl.BlockSpec((None, rows, tn), lambda l, j: (l, 0, j)),
        out_shape=jax.ShapeDtypeStruct((depth, rows, n), F32),
        compiler_params=_cparams(2, 48),
        name="adaln_mods",
    )(c_all, w_ada, b_ada.reshape(depth, 1, n))


def _modulate_kernel(x_ref, sc_ref, sh_ref, u_ref):
    u_ref[...] = (x_ref[...] * (1.0 + sc_ref[...]) + sh_ref[...]).astype(u_ref.dtype)


def _modulate(x, sc_spec, sh_spec, mods, tm):
    m, d = x.shape
    return pl.pallas_call(
        _modulate_kernel,
        grid=(m // tm,),
        in_specs=[pl.BlockSpec((tm, d), lambda i: (i, 0)), sc_spec(tm), sh_spec(tm)],
        out_specs=pl.BlockSpec((tm, d), lambda i: (i, 0)),
        out_shape=jax.ShapeDtypeStruct((m, d), BF16),
        compiler_params=_cparams(1, 32),
        name="modulate0",
    )(x, mods, mods)


def _rope_store(z, cos, sin, o_ref):
    for j in range(z.shape[1] // A_HEAD_DIM):
        zj = z[:, j * A_HEAD_DIM:(j + 1) * A_HEAD_DIM]
        o_ref[:, j * A_HEAD_DIM:(j + 1) * A_HEAD_DIM] = (
            zj * cos + pltpu.roll(zj, A_HEAD_DIM // 2, 1) * sin).astype(o_ref.dtype)


def _proj_kernel(*refs, rope, n_prev, w_mode):
    if n_prev:
        prev_ref, prevs_ref = refs[len(refs) - 5:len(refs) - 3]
        refs = refs[:len(refs) - 5] + refs[len(refs) - 3:]
    if rope:
        x_ref, xs_ref, w_ref, cos_ref, sin_ref, coss_ref, sins_ref, o_ref, os_ref, wb_ref = refs
    else:
        x_ref, xs_ref, w_ref, o_ref, os_ref, wb_ref = refs
    if n_prev:
        o_ref[0:n_prev] = prev_ref[...]
        o_ref = o_ref.at[n_prev]

        @pl.when(pl.program_id(1) == 0)
        def _():
            os_ref[0:n_prev] = prevs_ref[...]

        os_ref = os_ref.at[n_prev]

    def mm(x):
        if w_mode == "nk_contract_last":
            return lax.dot_general(x, wb_ref[...], (((1,), (1,)), ((), ())), preferred_element_type=F32)
        return jnp.dot(x, wb_ref[...], preferred_element_type=F32)

    @pl.when(pl.program_id(1) == 0)
    def _():
        w = w_ref[...]
        if w_mode == "nk_transpose":
            w = w.T
        wb_ref[...] = w.astype(BF16)
        zs = mm(xs_ref[...])
        if rope:
            _rope_store(zs, coss_ref[...], sins_ref[...], os_ref)
        else:
            os_ref[...] = zs.astype(os_ref.dtype)

    z = mm(x_ref[...])
    if rope:
        _rope_store(z, cos_ref[...], sin_ref[...], o_ref)
    else:
        o_ref[...] = z.astype(o_ref.dtype)


def _swiglu_in_kernel(x_ref, xs_ref, wg_ref, wv_ref, o_ref, os_ref, wgb_ref, wvb_ref):
    def act(x):
        g = jnp.dot(x, wgb_ref[...], preferred_element_type=F32)
        v = jnp.dot(x, wvb_ref[...], preferred_element_type=F32)
        return g * _sigmoid(g) * v

    @pl.when(pl.program_id(1) == 0)
    def _():
        wgb_ref[...] = wg_ref[...].astype(BF16)
        wvb_ref[...] = wv_ref[...].astype(BF16)
        os_ref[...] = act(xs_ref[...]).astype(os_ref.dtype)

    o_ref[...] = act(x_ref[...]).astype(o_ref.dtype)


def _proj(x, xs, w, layer, col0, ncols, tm, tn, out_dtype, rope=None, stack=None, w_t=False,
          name="proj"):
    m, k = x.shape
    rows_s = xs.shape[0]
    c0 = col0 // tn
    if w_t:
        w_spec = pl.BlockSpec((None, tn, k), lambda j, i: (layer, c0 + j, 0))
        w_mode = "nk_transpose" if tn % LANES == 0 else "nk_contract_last"
    else:
        w_spec = pl.BlockSpec((None, k, tn), lambda j, i: (layer, 0, c0 + j))
        w_mode = "kn"
    wb_shape = (tn, k) if w_mode == "nk_contract_last" else (k, tn)
    in_specs = [
        pl.BlockSpec((tm, k), lambda j, i: (i, 0)),
        pl.BlockSpec((rows_s, k), lambda j, i: (0, 0)),
        w_spec,
    ]
    args = [x, xs, w]
    if rope is not None:
        period = rope[0].shape[0] // tm
        in_specs += [pl.BlockSpec((tm, A_HEAD_DIM), lambda j, i: (i % period, 0))] * 2
        in_specs += [pl.BlockSpec((rows_s, A_HEAD_DIM), lambda j, i: (0, 0))] * 2
        args += list(rope)
    n_prev = 0
    if stack is None:
        out_specs = [pl.BlockSpec((tm, tn), lambda j, i: (i, j)),
                     pl.BlockSpec((rows_s, tn), lambda j, i: (0, j))]
        out_shape = [jax.ShapeDtypeStruct((m, ncols), out_dtype),
                     jax.ShapeDtypeStruct((rows_s, ncols), out_dtype)]
    elif len(stack) == 0:
        out_specs = [pl.BlockSpec((None, tm, tn), lambda j, i: (0, i, j)),
                     pl.BlockSpec((None, rows_s, tn), lambda j, i: (0, 0, j))]
        out_shape = [jax.ShapeDtypeStruct((1, m, ncols), out_dtype),
                     jax.ShapeDtypeStruct((1, rows_s, ncols), out_dtype)]
    else:
        n_prev = stack[0].shape[0]
        in_specs += [pl.BlockSpec((n_prev, tm, tn), lambda j, i: (0, i, j)),
                     pl.BlockSpec((n_prev, rows_s, tn), lambda j, i: (0, 0, j))]
        args += list(stack)
        out_specs = [pl.BlockSpec((n_prev + 1, tm, tn), lambda j, i: (0, i, j)),
                     pl.BlockSpec((n_prev + 1, rows_s, tn), lambda j, i: (0, 0, j))]
        out_shape = [jax.ShapeDtypeStruct((n_prev + 1, m, ncols), out_dtype),
                     jax.ShapeDtypeStruct((n_prev + 1, rows_s, ncols), out_dtype)]
    return pl.pallas_call(
        functools.partial(_proj_kernel, rope=rope is not None, n_prev=n_prev, w_mode=w_mode),
        grid=(ncols // tn, m // tm),
        in_specs=in_specs,
        out_specs=out_specs,
        out_shape=out_shape,
        scratch_shapes=[pltpu.VMEM(wb_shape, BF16)],
        compiler_params=_cparams(2, 56),
        name=name,
    )(*args)


def _swiglu_in(x, xs, w, layer, tm, tn):
    m, k = x.shape
    rows_s = xs.shape[0]
    dff = w.shape[2] // 2
    nv = dff // tn
    return pl.pallas_call(
        _swiglu_in_kernel,
        grid=(dff // tn, m // tm),
        in_specs=[
            pl.BlockSpec((tm, k), lambda j, i: (i, 0)),
            pl.BlockSpec((rows_s, k), lambda j, i: (0, 0)),
            pl.BlockSpec((None, k, tn), lambda j, i: (layer, 0, j)),
            pl.BlockSpec((None, k, tn), lambda j, i: (layer, 0, nv + j)),
        ],
        out_specs=[pl.BlockSpec((tm, tn), lambda j, i: (i, j)),
                   pl.BlockSpec((rows_s, tn), lambda j, i: (0, j))],
        out_shape=[jax.ShapeDtypeStruct((m, dff), BF16), jax.ShapeDtypeStruct((rows_s, dff), BF16)],
        scratch_shapes=[pltpu.VMEM((k, tn), BF16), pltpu.VMEM((k, tn), BF16)],
        compiler_params=_cparams(2, 56),
        name="swiglu_in",
    )(x, xs, w, w)


def _cast_kernel(w_ref, o_ref):
    o_ref[...] = w_ref[...].astype(o_ref.dtype)


def _cast_bf16(w, layer, tk):
    _, k, n = w.shape
    return pl.pallas_call(
        _cast_kernel,
        grid=(k // tk,),
        in_specs=[pl.BlockSpec((None, tk, n), lambda i: (layer, i, 0))],
        out_specs=pl.BlockSpec((tk, n), lambda i: (i, 0)),
        out_shape=jax.ShapeDtypeStruct((k, n), BF16),
        compiler_params=_cparams(1, 32),
        name="cast_bf16",
    )(w)


def _out_ln_kernel(*refs, n_parts, res_w):
    ap_refs = refs[:n_parts]
    as_refs = refs[n_parts:2 * n_parts]
    (w_ref, x_ref, gate_ref, sc_ref, sh_ref, xs_ref, gates_ref, scs_ref, shs_ref, lng_ref, lnb_ref,
     xo_ref, uo_ref, xso_ref, uso_ref) = refs[2 * n_parts:]

    def run(a_refs, x_ref, gate_ref, sc_ref, sh_ref, xo_ref, uo_ref):
        acc = None
        off = 0
        for a_ref in a_refs:
            kk = a_ref.shape[1]
            part = jnp.dot(a_ref[...], w_ref[off:off + kk, :], preferred_element_type=F32)
            acc = part if acc is None else acc + part
            off += kk
        y = ALPHA * x_ref[...] + (res_w * (1.0 + gate_ref[...])) * acc
        mu = jnp.mean(y, axis=-1, keepdims=True)
        yc = y - mu
        var = jnp.mean(yc * yc, axis=-1, keepdims=True)
        xh = yc * lax.rsqrt(var + LN_EPS)
        g = lng_ref[...]
        b = lnb_ref[...]
        mod = 1.0 + sc_ref[...]
        xo_ref[...] = xh * g + b
        uo_ref[...] = (xh * (g * mod) + (b * mod + sh_ref[...])).astype(uo_ref.dtype)

    @pl.when(pl.program_id(0) == 0)
    def _():
        run(as_refs, xs_ref, gates_ref, scs_ref, shs_ref, xso_ref, uso_ref)

    run(ap_refs, x_ref, gate_ref, sc_ref, sh_ref, xo_ref, uo_ref)


def _out_ln(parts, parts_s, wb, x, xs, mods, mods_s, specs, specs_s, ln_g, ln_b, ln_row, res_w, tm):
    m = parts[0].shape[0]
    rows_s = xs.shape[0]
    k, d = wb.shape
    ln_spec = pl.BlockSpec((None, 1, d), lambda i: (ln_row, 0, 0))
    row = lambda width: pl.BlockSpec((tm, width), lambda i: (i, 0))
    whole = lambda width: pl.BlockSpec((rows_s, width), lambda i: (0, 0))
    return pl.pallas_call(
        functools.partial(_out_ln_kernel, n_parts=len(parts), res_w=res_w),
        grid=(m // tm,),
        in_specs=[row(p.shape[1]) for p in parts] + [whole(p.shape[1]) for p in parts_s] + [
            pl.BlockSpec((k, d), lambda i: (0, 0), pipeline_mode=pl.Buffered(1)),
            row(d)] + [s(tm) for s in specs] + [whole(d)] + [s(rows_s) for s in specs_s] + [
            ln_spec, ln_spec],
        out_specs=[row(d), row(d), whole(d), whole(d)],
        out_shape=[jax.ShapeDtypeStruct((m, d), F32), jax.ShapeDtypeStruct((m, d), BF16),
                   jax.ShapeDtypeStruct((rows_s, d), F32), jax.ShapeDtypeStruct((rows_s, d), BF16)],
        compiler_params=_cparams(1, 56),
        name="out_postnorm",
    )(*parts, *parts_s, wb, x, mods, mods, mods, xs, mods_s, mods_s, mods_s, ln_g, ln_b)


def _moba_prompt_kernel(q_ref, k_ref, v_ref, o_ref, kb_ref, vb_ref, kmean_ref,
                        m_ref, acc_ref, *, nblk, scale):
    qi = pl.program_id(2)
    blk = MOBA_BLOCK
    hd = A_HEAD_DIM
    nheads = q_ref.shape[1] // hd
    lanes = lambda hh: slice(hh * hd, (hh + 1) * hd)

    @pl.when(qi == 0)
    def _():
        seq = k_ref.shape[0]
        r = lax.broadcasted_iota(jnp.int32, (seq, hd), 0)
        c = lax.broadcasted_iota(jnp.int32, (seq, hd), 1)
        in_blk = (r >= c * blk) & (r < c * blk + blk)
        onehot = jnp.where(in_blk, 1.0, 0.0).astype(BF16)
        kmean_ref[...] = jnp.zeros_like(kmean_ref)
        for hh in range(nheads):
            kf = k_ref[:, lanes(hh)]
            kb_ref[hh, :, 0:hd] = kf.astype(BF16)
            kb_ref[hh, :, hd:2 * hd] = onehot
            vb_ref[hh, :, 0:hd] = v_ref[:, lanes(hh)].astype(BF16)
            vb_ref[hh, :, hd:2 * hd] = jnp.ones((seq, hd), BF16)
            kmean_ref[hh, 0:nblk, :] = jnp.mean(kf.reshape(nblk, blk, hd), axis=1)

    nt = (((1,), (1,)), ((), ()))
    start = pl.multiple_of(qi * blk, blk)
    q_augs = []
    for hh in range(nheads):
        q = q_ref[:, lanes(hh)]
        gate = lax.dot_general(kmean_ref[hh, 0:nblk, :], q, nt, precision=lax.Precision.HIGHEST,
                               preferred_element_type=F32)
        bi = lax.broadcasted_iota(jnp.int32, gate.shape, 0)
        bf = bi.astype(F32)
        g = jnp.where(bi < qi, gate, -jnp.inf)
        sel = jnp.zeros(gate.shape, F32)
        for _ in range(MOBA_TOPK):
            mx = jnp.max(g, axis=0, keepdims=True)
            cand = jnp.where(g == mx, bf, 1e9)
            cand = jnp.where(mx > -jnp.inf, cand, 1e9)
            pick = bf == jnp.min(cand, axis=0, keepdims=True)
            sel = jnp.where(pick, 1.0, sel)
            g = jnp.where(pick, -jnp.inf, g)
        qb = q.astype(BF16)
        neg = jnp.concatenate([(1.0 - sel) * NEG_BIG, jnp.zeros((hd - nblk, blk), F32)], axis=0).T
        q_augs.append(jnp.concatenate([qb, neg.astype(BF16)], axis=1))
        s = lax.dot_general(qb, kb_ref[hh, pl.ds(start, blk), 0:hd], nt, preferred_element_type=F32) * scale
        row = lax.broadcasted_iota(jnp.int32, s.shape, 0)
        col = lax.broadcasted_iota(jnp.int32, s.shape, 1)
        s = jnp.where(col <= row, s, NEG_BIG)
        m0 = jnp.max(s, axis=1, keepdims=True)
        p = jnp.exp(s - m0)
        m_ref[hh] = jnp.broadcast_to(m0, m_ref.shape[1:])
        acc_ref[hh] = jnp.dot(p.astype(BF16), vb_ref[hh, pl.ds(start, blk), :], preferred_element_type=F32)

    grp = MOBA_GROUP
    span = grp * blk

    def past_group(gi, carry):
        st = pl.multiple_of(gi * span, span)
        for hh in range(nheads):
            sn = lax.dot_general(q_augs[hh], kb_ref[hh, pl.ds(st, span), :], nt,
                                 preferred_element_type=F32) * scale
            m_prev = m_ref[hh]
            m_new = jnp.maximum(m_prev, jnp.max(sn, axis=1, keepdims=True))
            a = jnp.exp(m_prev - m_new)
            pn = jnp.exp(sn - jnp.concatenate([m_new] * (span // hd), axis=1))
            acc_ref[hh] = jnp.concatenate([a, a], axis=1) * acc_ref[hh] + jnp.dot(
                pn.astype(BF16), vb_ref[hh, pl.ds(st, span), :], preferred_element_type=F32)
            m_ref[hh] = m_new
        return carry

    lax.fori_loop(0, (qi + grp - 1) // grp, past_group, 0)
    for hh in range(nheads):
        acc = acc_ref[hh]
        o_ref[:, lanes(hh)] = (acc[:, 0:hd] / acc[:, hd:2 * hd]).astype(o_ref.dtype)


def _moba_prompt(q, k, v, slab, bsz):
    rows, width = q.shape
    seq = rows // bsz
    nh = width // A_HEAD_DIM
    nblk = seq // MOBA_BLOCK
    assert nblk % MOBA_GROUP == 0
    blk = MOBA_BLOCK
    hd = A_HEAD_DIM
    hps = MOBA_HEADS_PER_STEP
    assert nh % hps == 0
    return pl.pallas_call(
        functools.partial(_moba_prompt_kernel, nblk=nblk, scale=hd ** -0.5),
        grid=(bsz, nh // hps, nblk),
        in_specs=[
            pl.BlockSpec((blk, hps * hd), lambda b, h, i: (b * nblk + i, h)),
            pl.BlockSpec((None, seq, hps * hd), lambda b, h, i: (slab, b, h), pipeline_mode=pl.Buffered(1)),
            pl.BlockSpec((None, seq, hps * hd), lambda b, h, i: (slab, b, h), pipeline_mode=pl.Buffered(1)),
        ],
        out_specs=pl.BlockSpec((blk, hps * hd), lambda b, h, i: (b * nblk + i, h)),
        out_shape=jax.ShapeDtypeStruct((rows, width), BF16),
        scratch_shapes=[
            pltpu.VMEM((hps, seq, 2 * hd), BF16), pltpu.VMEM((hps, seq, 2 * hd), BF16),
            pltpu.VMEM((hps, LANES, hd), F32),
            pltpu.VMEM((hps, blk, hd), F32), pltpu.VMEM((hps, blk, 2 * hd), F32),
        ],
        compiler_params=_cparams(3, 52),
        name="moba_prompt",
    )(q, k, v)


def _mlstm_prompt_kernel(q_ref, k_ref, v_ref, og_ref, g_ref, gb_ref, ng_ref,
                         h_ref, c_ref, n_ref, m_ref, *, nb, nh, dk, dv):
    lc = MLSTM_CHUNK
    chunks = q_ref.shape[1] // lc

    @pl.when(pl.program_id(0) == 0)
    def _():
        c_ref[...] = jnp.zeros_like(c_ref)
        n_ref[...] = jnp.zeros_like(n_ref)
        m_ref[...] = jnp.zeros_like(m_ref)

    tt = lax.broadcasted_iota(jnp.int32, (lc, lc), 0)
    ss = lax.broadcasted_iota(jnp.int32, (lc, lc), 1)
    causal = ss <= tt
    eye = ss == tt
    nt = (((1,), (1,)), ((), ()))
    tn = (((0,), (0,)), ((), ()))
    gb = gb_ref[...]
    for cc in range(chunks):
        rows = slice(cc * lc, (cc + 1) * lc)
        for b in range(nb):
            gates = g_ref[b, rows, :] + gb
            log_f = jnp.minimum(gates, 0.0) - jnp.log1p(jnp.exp(-jnp.abs(gates)))
            for h in range(nh):
                bh = b * nh + h
                i_col = gates[:, h:h + 1]
                f_col = log_f[:, nh + h:nh + h + 1]
                f_row = jnp.sum(jnp.where(eye, f_col, 0.0), axis=0, keepdims=True)
                i_row = jnp.sum(jnp.where(eye, i_col, 0.0), axis=0, keepdims=True)
                b_col = jnp.sum(jnp.where(causal, f_row, 0.0), axis=1, keepdims=True)
                b_row = jnp.sum(jnp.where(ss >= tt, f_col, 0.0), axis=0, keepdims=True)
                m_prev = m_ref[bh][:, 0:1]
                d = jnp.where(causal, b_col - b_row + i_row, NEG_BIG)
                inter = b_col + m_prev
                m_t = jnp.maximum(jnp.max(d, axis=1, keepdims=True), inter)
                w = jnp.exp(d - m_t)
                q = q_ref[b, rows, h * dk:(h + 1) * dk]
                k = k_ref[b, rows, h * dk:(h + 1) * dk] * (dk ** -0.5)
                vb = v_ref[b, rows, h * dv:(h + 1) * dv].astype(BF16)
                qb = q.astype(BF16)
                s = lax.dot_general(qb, k.astype(BF16), nt, preferred_element_type=F32) * w
                wi = jnp.exp(inter - m_t)
                cst = c_ref[bh]
                nrow = n_ref[bh]
                num = wi * jnp.dot(qb, cst.astype(BF16), preferred_element_type=F32) + jnp.dot(
                    s.astype(BF16), vb, preferred_element_type=F32)
                den = wi * jnp.sum(q * nrow, axis=1, keepdims=True) + jnp.sum(s, axis=1, keepdims=True)
                hh = num / jnp.maximum(jnp.abs(den), jnp.exp(-m_t))
                b_last = b_col[lc - 1:lc, :]
                g_col = b_last - b_col + i_col
                m_new = jnp.maximum(b_last + m_prev, jnp.max(g_col, axis=0, keepdims=True))
                wc = jnp.exp(b_last + m_prev - m_new)
                kw = k * jnp.exp(g_col - m_new)
                c_ref[bh] = wc * cst + lax.dot_general(kw.astype(BF16), vb, tn, preferred_element_type=F32)
                n_ref[bh] = wc * nrow + jnp.sum(kw, axis=0, keepdims=True)
                m_ref[bh] = jnp.broadcast_to(m_new, (1, LANES))
                hn = hh * lax.rsqrt(jnp.mean(hh * hh, axis=1, keepdims=True) + LN_EPS)
                hn = hn * ng_ref[:, h * dv:(h + 1) * dv] * _sigmoid(og_ref[b, rows, h * dv:(h + 1) * dv])
                h_ref[b, rows, h * dv:(h + 1) * dv] = hn.astype(h_ref.dtype)


def _mlstm_prompt(zm, gates, gate_bias, norm_g, bsz):
    rows, _ = zm.shape
    seq = rows // bsz
    nh = B_HEADS
    v_w = norm_g.shape[1]
    qk_w = v_w // 2
    dk, dv = qk_w // nh, v_w // nh
    lc = MLSTM_CHUNK * MLSTM_CHUNKS_PER_STEP
    assert seq % lc == 0
    z3 = zm.reshape(bsz, seq, zm.shape[1])
    gw = gates.shape[1]
    g3 = gates.reshape(bsz, seq, gw)
    nbh = bsz * nh
    full = lambda shape: pl.BlockSpec(shape, lambda c: (0,) * len(shape))
    return pl.pallas_call(
        functools.partial(_mlstm_prompt_kernel, nb=bsz, nh=nh, dk=dk, dv=dv),
        grid=(seq // lc,),
        in_specs=[
            pl.BlockSpec((bsz, lc, qk_w), lambda c: (0, c, 0)),
            pl.BlockSpec((bsz, lc, qk_w), lambda c: (0, c, 1)),
            pl.BlockSpec((bsz, lc, v_w), lambda c: (0, c, 1)),
            pl.BlockSpec((bsz, lc, v_w), lambda c: (0, c, 2)),
            pl.BlockSpec((bsz, lc, gw), lambda c: (0, c, 0)),
            full((1, gw)), full((1, v_w)),
        ],
        out_specs=[
            pl.BlockSpec((bsz, lc, v_w), lambda c: (0, c, 0)),
            full((nbh, dk, dv)), full((nbh, 1, dk)), full((nbh, 1, LANES)),
        ],
        out_shape=[
            jax.ShapeDtypeStruct((bsz, seq, v_w), BF16),
            jax.ShapeDtypeStruct((nbh, dk, dv), F32),
            jax.ShapeDtypeStruct((nbh, 1, dk), F32),
            jax.ShapeDtypeStruct((nbh, 1, LANES), F32),
        ],
        compiler_params=_cparams(1, 32),
        name="mlstm_prompt",
    )(z3, z3, z3, z3, g3, gate_bias, norm_g)


def _rglru_gates(xconv, wa_ref, ba, wx_ref, bx, lam):
    bd = wa_ref.shape[1]
    r_parts, i_parts = [], []
    for n in range(wa_ref.shape[0]):
        xb = xconv[:, n * bd:(n + 1) * bd].astype(BF16)
        r_parts.append(jnp.dot(xb, wa_ref[n].astype(BF16), preferred_element_type=F32))
        i_parts.append(jnp.dot(xb, wx_ref[n].astype(BF16), preferred_element_type=F32))
    r = _sigmoid(jnp.concatenate(r_parts, axis=1) + ba)
    ig = _sigmoid(jnp.concatenate(i_parts, axis=1) + bx)
    log_a = (-RG_C) * r * _softplus(-lam)
    a = jnp.exp(log_a)
    mult = jnp.sqrt(1.0 - a * a)
    return a, mult * ig * xconv


def _rglru_prompt_kernel(gate_ref, xr_ref, cw_ref, cb_ref, wa_ref, ba_ref, wx_ref, bx_ref, lam_ref,
                         y_ref, hl_ref, buf_ref, xext_ref, a_ref, b_ref, hs_ref, hc_ref):
    t = pl.program_id(1)
    tt = xr_ref.shape[0]
    pad = 8
    nbuf = CONV_WIDTH - 1

    @pl.when(t == 0)
    def _():
        xext_ref[0:pad, :] = jnp.zeros((pad, xext_ref.shape[1]), F32)
        hc_ref[...] = jnp.zeros_like(hc_ref)

    @pl.when(t > 0)
    def _():
        xext_ref[0:pad, :] = xext_ref[tt:tt + pad, :]

    xr = xr_ref[...]
    xext_ref[pad:pad + tt, :] = xr
    xconv = cb_ref[...] + xext_ref[pad - nbuf:pad - nbuf + tt, :] * cw_ref[0:1, :]
    for j in range(1, CONV_WIDTH):
        xconv = xconv + xext_ref[pad - nbuf + j:pad - nbuf + j + tt, :] * cw_ref[j:j + 1, :]
    a, bterm = _rglru_gates(xconv, wa_ref, ba_ref[...], wx_ref, bx_ref[...], lam_ref[...])
    a_ref[...] = a
    b_ref[...] = bterm

    def step(i, h):
        h = a_ref[pl.ds(i, 1), :] * h + b_ref[pl.ds(i, 1), :]
        hs_ref[pl.ds(i, 1), :] = h
        return h

    h_last = lax.fori_loop(0, tt, step, hc_ref[...], unroll=8)
    hc_ref[...] = h_last
    y_ref[...] = (_gelu_tanh(gate_ref[...]) * hs_ref[...]).astype(y_ref.dtype)
    hl_ref[...] = h_last
    buf_ref[...] = xr[tt - nbuf:tt, :]


def _rglru_prompt(zr, layer, conv_w, conv_b, w_a, b_a, w_x, b_x, lam, bsz, tt):
    rows, w2 = zr.shape
    width = w2 // 2
    seq = rows // bsz
    nt = seq // tt
    nblk, bd = w_a.shape[1], w_a.shape[2]
    vec = lambda a: a.reshape(a.shape[0], 1, width)
    vspec = pl.BlockSpec((None, 1, width), lambda b, t: (layer, 0, 0))
    wspec = pl.BlockSpec((None, nblk, bd, bd), lambda b, t: (layer, 0, 0, 0))
    nbuf = CONV_WIDTH - 1
    return pl.pallas_call(
        _rglru_prompt_kernel,
        grid=(bsz, nt),
        in_specs=[
            pl.BlockSpec((tt, width), lambda b, t: (b * nt + t, 0)),
            pl.BlockSpec((tt, width), lambda b, t: (b * nt + t, 1)),
            pl.BlockSpec((None, CONV_WIDTH, width), lambda b, t: (layer, 0, 0)),
            vspec, wspec, vspec, wspec, vspec, vspec,
        ],
        out_specs=[
            pl.BlockSpec((tt, width), lambda b, t: (b * nt + t, 0)),
            pl.BlockSpec((None, 1, width), lambda b, t: (b, 0, 0)),
            pl.BlockSpec((None, nbuf, width), lambda b, t: (b, 0, 0)),
        ],
        out_shape=[
            jax.ShapeDtypeStruct((rows, width), BF16),
            jax.ShapeDtypeStruct((bsz, 1, width), F32),
            jax.ShapeDtypeStruct((bsz, nbuf, width), F32),
        ],
        scratch_shapes=[
            pltpu.VMEM((tt + 8, width), F32), pltpu.VMEM((tt, width), F32),
            pltpu.VMEM((tt, width), F32), pltpu.VMEM((tt, width), F32), pltpu.VMEM((1, width), F32),
        ],
        compiler_params=_cparams(2, 48),
        name="rglru_prompt",
    )(zr, zr, conv_w, vec(conv_b), w_a, vec(b_a), w_x, vec(b_x), vec(lam))


def _moba_gate_kernel(tbl_ref, q_ref, *refs, nheads, nsteps, pages_per_step, pages_per_blk):
    b = pl.program_id(0)
    st = pl.program_id(1)
    hd = A_HEAD_DIM
    k_refs = refs[:pages_per_step]
    sel_ref, ksum_ref = refs[pages_per_step:]
    page = k_refs[0].shape[0]
    blks_per_step = pages_per_step // pages_per_blk
    for i in range(blks_per_step):
        tot = jnp.sum(k_refs[i * pages_per_blk][...], axis=0)
        for u in range(1, pages_per_blk):
            tot = tot + jnp.sum(k_refs[i * pages_per_blk + u][...], axis=0)
        ksum_ref[st * blks_per_step + i] = tot

    @pl.when(st == nsteps - 1)
    def _():
        nblk = ksum_ref.shape[0]
        inv = 1.0 / (pages_per_blk * page)
        rowi = lax.broadcasted_iota(jnp.int32, (nblk, 1), 0).astype(F32)
        out_r = lax.broadcasted_iota(jnp.int32, sel_ref.shape, 0)
        out_c = lax.broadcasted_iota(jnp.int32, sel_ref.shape, 1)
        out = jnp.zeros(sel_ref.shape, jnp.int32)
        qrow = q_ref[pl.ds(b, 1), :]
        for h in range(nheads):
            km = ksum_ref[:, h, :] * inv
            g = jnp.sum(km * qrow[:, h * hd:(h + 1) * hd], axis=1, keepdims=True)
            for i in range(MOBA_TOPK):
                mx = jnp.max(g, axis=0, keepdims=True)
                idx = jnp.min(jnp.where(g == mx, rowi, 1e9), axis=0, keepdims=True)
                out = jnp.where((out_r == h) & (out_c == i), idx.astype(jnp.int32), out)
                g = jnp.where(rowi == idx, -jnp.inf, g)
        sel_ref[...] = out


GATE_PAGES_PER_STEP = 16


def _moba_sample_gate(q, cache_k, layer, tbl_flat, nreq, npages):
    page, nheads, hd = cache_k.shape[2:]
    ppb = MOBA_BLOCK // page
    nblk = npages // ppb
    pps = GATE_PAGES_PER_STEP
    nsteps = npages // pps

    def page_spec(u):
        return pl.BlockSpec((None, None, page, nheads, hd),
                            lambda b, s, tbl: (layer, tbl[b * npages + s * pps + u], 0, 0, 0))

    grid_spec = pltpu.PrefetchScalarGridSpec(
        num_scalar_prefetch=1,
        grid=(nreq, nsteps),
        in_specs=[pl.BlockSpec(q.shape, lambda b, s, tbl: (0, 0))] + [page_spec(u) for u in range(pps)],
        out_specs=pl.BlockSpec((None, nheads, LANES), lambda b, s, tbl: (b, 0, 0)),
        scratch_shapes=[pltpu.VMEM((nblk, nheads, hd), F32)],
    )
    return pl.pallas_call(
        functools.partial(_moba_gate_kernel, nheads=nheads, nsteps=nsteps, pages_per_step=pps,
                          pages_per_blk=ppb),
        grid_spec=grid_spec,
        out_shape=jax.ShapeDtypeStruct((nreq, nheads, LANES), jnp.int32),
        compiler_params=_cparams(2, 32),
        name="moba_sample_gate",
    )(tbl_flat, q, *([cache_k] * pps))


def _moba_sample_attn_kernel(tbl_ref, sel_ref, q_ref, kn_ref, vn_ref, *refs, npg, hps, scale):
    o_ref = refs[2 * npg * hps]
    b = pl.program_id(0)
    nheads = q_ref.shape[1]
    hd = q_ref.shape[2]
    for hh in range(hps):
        k_refs = refs[hh * npg:(hh + 1) * npg]
        v_refs = refs[(hps + hh) * npg:(hps + hh + 1) * npg]
        h = pl.program_id(1) * hps + hh
        is_h = lax.broadcasted_iota(jnp.int32, (1, nheads, 1), 1) == h
        q_m = jnp.where(is_h, q_ref[b][None], 0.0)

        def score(kp):
            part = jnp.sum(kp * q_m, axis=2, keepdims=True)
            return jnp.sum(part, axis=1, keepdims=True) * scale

        s_own = score(kn_ref[b][None])
        ss = [score(k_ref[...]) for k_ref in k_refs]
        m = s_own
        for s in ss:
            m = jnp.maximum(m, jnp.max(s, axis=0, keepdims=True))
        l = jnp.exp(s_own - m)
        acc = l * vn_ref[b][None]
        for s, v_ref in zip(ss, v_refs):
            p = jnp.exp(s - m)
            l = l + jnp.sum(p, axis=0, keepdims=True)
            acc = acc + jnp.sum(p * v_ref[...], axis=0, keepdims=True)
        out = jnp.sum(jnp.where(is_h, acc / l, 0.0), axis=1)
        o_ref[:, hh * hd:(hh + 1) * hd] = out.astype(o_ref.dtype)


SAMPLE_ATTN_HEADS_PER_STEP = 2


def _moba_sample_attn(q, k_new, v_new, cache_k, cache_v, layer, tbl_flat, sel_flat, nreq, npages):
    page, nheads, hd = cache_k.shape[2:]
    ppb = MOBA_BLOCK // page
    npg = MOBA_TOPK * ppb
    hps = SAMPLE_ATTN_HEADS_PER_STEP
    assert nheads % hps == 0

    def page_spec(hh, j):
        def page_map(b, g, tbl, sel):
            blk = sel[(b * nheads + g * hps + hh) * LANES + j // ppb]
            return (layer, tbl[b * npages + blk * ppb + j % ppb], 0, 0, 0)
        return pl.BlockSpec((None, None, page, nheads, hd), page_map)

    head = pl.BlockSpec(q.shape, lambda b, g, tbl, sel: (0, 0, 0))
    pages = [page_spec(hh, j) for hh in range(hps) for j in range(npg)]
    grid_spec = pltpu.PrefetchScalarGridSpec(
        num_scalar_prefetch=2,
        grid=(nreq, nheads // hps),
        in_specs=[head, head, head] + pages + pages,
        out_specs=pl.BlockSpec((None, 1, hps * hd), lambda b, g, tbl, sel: (b, 0, g)),
    )
    n_in = npg * hps
    return pl.pallas_call(
        functools.partial(_moba_sample_attn_kernel, npg=npg, hps=hps, scale=hd ** -0.5),
        grid_spec=grid_spec,
        out_shape=jax.ShapeDtypeStruct((nreq, 1, nheads * hd), F32),
        compiler_params=_cparams(2, 48),
        name="moba_sample_attn",
    )(tbl_flat, sel_flat, q, k_new, v_new, *([cache_k] * n_in), *([cache_v] * n_in))


def _mlstm_sample_kernel(q_ref, k_ref, v_ref, og_ref, g_ref, gb_ref, ng_ref, c0_ref, n0_ref, m0_ref,
                         h_ref, c_ref, n_ref, m_ref, *, nh, dk, dv):
    b = pl.program_id(0)
    rr = lax.broadcasted_iota(jnp.int32, (dk, dk), 0)
    cc = lax.broadcasted_iota(jnp.int32, (dk, dk), 1)
    eye = rr == cc
    gates = g_ref[pl.ds(b, 1), :] + gb_ref[...]
    q_all = q_ref[pl.ds(b, 1), :]
    k_all = k_ref[pl.ds(b, 1), :]
    v_all = v_ref[pl.ds(b, 1), :]
    og_all = og_ref[pl.ds(b, 1), :]
    for h in range(nh):
        ii = gates[:, h:h + 1]
        fpre = gates[:, nh + h:nh + h + 1]
        ff = jnp.minimum(fpre, 0.0) - jnp.log1p(jnp.exp(-jnp.abs(fpre)))
        m_prev = m0_ref[h][:, 0:1]
        q = q_all[:, h * dk:(h + 1) * dk]
        k = k_all[:, h * dk:(h + 1) * dk] * (dk ** -0.5)
        v = v_all[:, h * dv:(h + 1) * dv]
        inter = ff + m_prev
        m_t = jnp.maximum(ii, inter)
        s = jnp.sum(q * k, axis=1, keepdims=True) * jnp.exp(ii - m_t)
        wi = jnp.exp(inter - m_t)
        q_col = jnp.sum(jnp.where(eye, q, 0.0), axis=1, keepdims=True)
        k_col = jnp.sum(jnp.where(eye, k, 0.0), axis=1, keepdims=True)
        cst = c0_ref[h]
        nrow = n0_ref[h]
        num = wi * jnp.sum(q_col * cst, axis=0, keepdims=True) + s * v
        den = wi * jnp.sum(q * nrow, axis=1, keepdims=True) + s
        hh = num / jnp.maximum(jnp.abs(den), jnp.exp(-m_t))
        m_new = m_t
        wc = jnp.exp(inter - m_new)
        wg = jnp.exp(ii - m_new)
        c_ref[h] = wc * cst + (wg * k_col) * v
        n_ref[h] = wc * nrow + wg * k
        m_ref[h] = jnp.broadcast_to(m_new, (1, LANES))
        hn = hh * lax.rsqrt(jnp.mean(hh * hh, axis=1, keepdims=True) + LN_EPS)
        hn = hn * ng_ref[:, h * dv:(h + 1) * dv] * _sigmoid(og_all[:, h * dv:(h + 1) * dv])
        h_ref[:, h * dv:(h + 1) * dv] = hn.astype(h_ref.dtype)


def _mlstm_sample(zm, gates, gate_bias, norm_g, c0, n0, m0, layer, nreq):
    nh = B_HEADS
    v_w = norm_g.shape[1]
    qk_w = v_w // 2
    dk, dv = qk_w // nh, v_w // nh
    rows = zm.shape[0]
    full = lambda shape: pl.BlockSpec(shape, lambda b: (0,) * len(shape))
    return pl.pallas_call(
        functools.partial(_mlstm_sample_kernel, nh=nh, dk=dk, dv=dv),
        grid=(nreq,),
        in_specs=[
            pl.BlockSpec((rows, qk_w), lambda b: (0, 0)),
            pl.BlockSpec((rows, qk_w), lambda b: (0, 1)),
            pl.BlockSpec((rows, v_w), lambda b: (0, 1)),
            pl.BlockSpec((rows, v_w), lambda b: (0, 2)),
            full(gates.shape), full(gate_bias.shape), full((1, v_w)),
            pl.BlockSpec((None, None, nh, dk, dv), lambda b: (layer, b, 0, 0, 0)),
            pl.BlockSpec((None, None, nh, 1, dk), lambda b: (layer, b, 0, 0, 0)),
            pl.BlockSpec((None, None, nh, 1, LANES), lambda b: (layer, b, 0, 0, 0)),
        ],
        out_specs=[
            pl.BlockSpec((None, 1, v_w), lambda b: (b, 0, 0)),
            pl.BlockSpec((None, nh, dk, dv), lambda b: (b, 0, 0, 0)),
            pl.BlockSpec((None, nh, 1, dk), lambda b: (b, 0, 0, 0)),
            pl.BlockSpec((None, nh, 1, LANES), lambda b: (b, 0, 0, 0)),
        ],
        out_shape=[
            jax.ShapeDtypeStruct((nreq, 1, v_w), F32),
            jax.ShapeDtypeStruct((nreq, nh, dk, dv), F32),
            jax.ShapeDtypeStruct((nreq, nh, 1, dk), F32),
            jax.ShapeDtypeStruct((nreq, nh, 1, LANES), F32),
        ],
        compiler_params=_cparams(1, 32),
        name="mlstm_sample",
    )(zm, zm, zm, zm, gates, gate_bias, norm_g, c0, n0, m0)


def _rglru_sample_kernel(gate_ref, xr_ref, buf_ref, h0_ref, cw_ref, cb_ref, wa_ref, ba_ref, wx_ref,
                         bx_ref, lam_ref, y_ref, h_ref, nbuf_ref):
    nbuf = CONV_WIDTH - 1
    xr = xr_ref[...]
    xconv = cb_ref[...] + buf_ref[0] * cw_ref[0:1, :]
    for j in range(1, nbuf):
        xconv = xconv + buf_ref[j] * cw_ref[j:j + 1, :]
    xconv = xconv + xr * cw_ref[nbuf:nbuf + 1, :]
    a, bterm = _rglru_gates(xconv, wa_ref, ba_ref[...], wx_ref, bx_ref[...], lam_ref[...])
    h = a * h0_ref[...] + bterm
    h_ref[...] = h
    y = _gelu_tanh(gate_ref[...]) * h
    pad_rows = y_ref.shape[0] - y.shape[0]
    y_ref[...] = jnp.concatenate([y, jnp.zeros((pad_rows, y.shape[1]), F32)], axis=0).astype(y_ref.dtype)
    for j in range(nbuf - 1):
        nbuf_ref[j] = buf_ref[j + 1]
    nbuf_ref[nbuf - 1] = xr


def _rglru_sample(zr, layer, buf_t, h0, conv_w, conv_b, w_a, b_a, w_x, b_x, lam, nreq):
    rows, w2 = zr.shape
    width = w2 // 2
    nblk, bd = w_a.shape[1], w_a.shape[2]
    nbuf = CONV_WIDTH - 1
    vec = lambda a: a.reshape(a.shape[0], 1, width)
    vspec = pl.BlockSpec((None, 1, width), lambda i: (layer, 0, 0))
    wspec = pl.BlockSpec((None, nblk, bd, bd), lambda i: (layer, 0, 0, 0))
    return pl.pallas_call(
        _rglru_sample_kernel,
        grid=(1,),
        in_specs=[
            pl.BlockSpec((nreq, width), lambda i: (0, 0)),
            pl.BlockSpec((nreq, width), lambda i: (0, 1)),
            pl.BlockSpec((None, nbuf, nreq, width), lambda i: (layer, 0, 0, 0)),
            pl.BlockSpec((None, nreq, width), lambda i: (layer, 0, 0)),
            pl.BlockSpec((None, CONV_WIDTH, width), lambda i: (layer, 0, 0)),
            vspec, wspec, vspec, wspec, vspec, vspec,
        ],
        out_specs=[
            pl.BlockSpec((rows, width), lambda i: (0, 0)),
            pl.BlockSpec((nreq, width), lambda i: (0, 0)),
            pl.BlockSpec((nbuf, nreq, width), lambda i: (0, 0, 0)),
        ],
        out_shape=[
            jax.ShapeDtypeStruct((rows, width), BF16),
            jax.ShapeDtypeStruct((nreq, width), F32),
            jax.ShapeDtypeStruct((nbuf, nreq, width), F32),
        ],
        compiler_params=_cparams(1, 32),
        name="rglru_sample",
    )(zr, zr, buf_t, h0, conv_w, vec(conv_b), w_a, vec(b_a), w_x, vec(b_x), vec(lam))


def _rope_tables(pos):
    half = A_HEAD_DIM // 2
    inv = ROPE_THETA ** (-jnp.arange(half, dtype=F32) / half)
    ang = pos.astype(F32)[:, None] * inv[None, :]
    cos, sin = jnp.cos(ang), jnp.sin(ang)
    return jnp.concatenate([cos, cos], axis=1), jnp.concatenate([-sin, sin], axis=1)


def kernel(x_prompt, x_sample, cache_k, cache_v, state_mlstm_c, state_mlstm_n, state_mlstm_m,
           state_rglru_h, state_conv, page_table, c_prompt, c_sample, w_ada, b_ada, ln_g, ln_b,
           w_ffn1_in, w_ffn1_out, w_ffn2_in, w_ffn2_out, w_in_ab, b_igate, b_fgate, mlstm_norm_g,
           w_out_ab, w_in_rg, conv_w, conv_b, w_rg_a, b_rg_a, w_rg_x, b_rg_x, lru_lambda, w_out_rg):
    bsz, seq, d = x_prompt.shape
    nreq = x_sample.shape[0]
    depth = w_ada.shape[0]
    n_even = w_in_ab.shape[0]
    a_w = cache_k.shape[3] * cache_k.shape[4]
    v_w = mlstm_norm_g.shape[1]
    qk_w = v_w // 2
    m_w = 2 * qk_w + 2 * v_w
    nh_b = B_HEADS
    past_len = page_table.shape[1] * cache_k.shape[2]
    npages = page_table.shape[1]
    rows_p = bsz * seq
    rs = SAMPLE_ROWS

    tm = 1024
    tn = 512
    tm_p = 1024
    tn_p = 1024
    tm_out = 256
    tk_cast = 512
    tt_rg = 256

    c_all = jnp.concatenate([c_prompt, c_sample, jnp.zeros((rs - bsz - nreq, d), F32)], axis=0)
    mods = _mods(c_all, w_ada, b_ada).reshape(depth, rs, 9, d)
    mods_p = jnp.transpose(mods[:, :bsz], (0, 2, 1, 3)).reshape(depth * 9 * bsz, 1, d)
    mods_s = jnp.transpose(mods[:, bsz:bsz + nreq], (0, 2, 1, 3))
    mods_s = jnp.pad(mods_s, ((0, 0), (0, 0), (0, rs - nreq), (0, 0))).reshape(depth * 9, rs, d)

    def mod_spec_p(l, s, j):
        base = ((l * 3 + s) * 3 + j) * bsz

        def make(tile_rows):
            tiles_per_batch = seq // tile_rows
            return pl.BlockSpec((None, 1, d), lambda i: (base + i // tiles_per_batch, 0, 0))
        return make

    def mod_spec_s(l, s, j):
        row = (l * 3 + s) * 3 + j

        def make(tile_rows):
            return pl.BlockSpec((None, rs, d), lambda i: (row, 0, 0))
        return make

    ln_g3 = ln_g.reshape(depth * 3, 1, d)
    ln_b3 = ln_b.reshape(depth * 3, 1, d)

    cos_p, sin_p = _rope_tables(jnp.arange(seq, dtype=jnp.int32))
    cos_s, sin_s = _rope_tables(jnp.full((rs,), past_len, dtype=jnp.int32))

    tbl_flat = page_table.reshape(-1).astype(jnp.int32)
    n0_s = state_mlstm_n[:, :, :, None, :]
    m0_s = jnp.broadcast_to(state_mlstm_m[:, :, :, None, None], state_mlstm_m.shape + (1, LANES))
    conv_t = jnp.transpose(state_conv, (0, 2, 1, 3))
    gate_bias = jnp.concatenate([b_igate, b_fgate], axis=1)
    w_ab_t = jnp.swapaxes(w_in_ab, 1, 2)

    xp = x_prompt.reshape(rows_p, d)
    xs = jnp.pad(x_sample.reshape(nreq, d), ((0, rs - nreq), (0, 0)))
    up = _modulate(xp, mod_spec_p(0, 0, 1), mod_spec_p(0, 0, 0), mods_p, tm)
    us = _modulate(xs, mod_spec_s(0, 0, 1), mod_spec_s(0, 0, 0), mods_s, rs)

    op = dict(k=[], v=[], c=[], n=[], m=[], h=[], buf=[])
    os_ = dict(k=[], v=[], c=[], n=[], m=[], h=[], buf=[])
    rope = (cos_p, sin_p, cos_s, sin_s)
    k_st = v_st = ()
    pad_rows = lambda t, w: jnp.pad(t.reshape(nreq, w), ((0, rs - nreq), (0, 0))).astype(BF16)
    heads3 = lambda t: t.reshape(rs, -1, A_HEAD_DIM)

    for l in range(depth):
        wb_ffn1 = _cast_bf16(w_ffn1_out, l, tk_cast)
        wb_ffn2 = _cast_bf16(w_ffn2_out, l, tk_cast)
        wb_mix = _cast_bf16(w_out_ab if l % 2 == 0 else w_out_rg, l // 2, tk_cast)

        def post(parts_p, parts_s, wb, s, res_w, nxt):
            nl, ns = nxt
            pick = lambda spec: [spec(l, s, 2), spec(nl, ns, 1), spec(nl, ns, 0)]
            xo, uo, xso, uso = _out_ln(parts_p, parts_s, wb, xp, xs, mods_p, mods_s, pick(mod_spec_p),
                                       pick(mod_spec_s), ln_g3, ln_b3, l * 3 + s, res_w,
                                       tm_out if wb.shape[0] > d else 2 * tm_out)
            return (xo, uo), (xso, uso)

        act_p, act_s = _swiglu_in(up, us, w_ffn1_in, l, tm, tn)
        (xp, up), (xs, us) = post([act_p], [act_s], wb_ffn1, 0, FFN_RES, (l, 1))

        if l % 2 == 0:
            e = l // 2
            q_p, q_s = _proj(up, us, w_ab_t, e, 0, a_w, tm_p, tn_p, F32, rope=rope, w_t=True, name="proj_q")
            tm_kv = tm_p if e == 0 else tm_p // 2
            k_st = _proj(up, us, w_ab_t, e, a_w, a_w, tm_kv, tn_p, F32, rope=rope, w_t=True,
                         stack=k_st, name="proj_k")
            v_st = _proj(up, us, w_ab_t, e, 2 * a_w, a_w, tm_kv, tn_p, F32, w_t=True,
                         stack=v_st, name="proj_v")
            k_s, v_s = k_st[1][e], v_st[1][e]
            zm_p, zm_s = _proj(up, us, w_ab_t, e, 3 * a_w, m_w, tm_p, tn_p, F32, w_t=True, name="proj_mlstm")
            n_gates = 2 * nh_b
            g_p, g_s = _proj(up, us, w_ab_t, e, 3 * a_w + m_w, n_gates, tm_p, n_gates, F32, w_t=True,
                             name="proj_gates")
            gb = gate_bias[e:e + 1]
            ng = mlstm_norm_g[e:e + 1]

            oa = _moba_prompt(q_p, k_st[0], v_st[0], e, bsz)
            hm, c1, n1, m1 = _mlstm_prompt(zm_p, g_p, gb, ng, bsz)
            mix_p = [oa, hm.reshape(rows_p, v_w)]
            op["c"].append(c1.reshape(bsz, nh_b, c1.shape[1], c1.shape[2]))
            op["n"].append(n1.reshape(bsz, nh_b, -1))
            op["m"].append(m1[:, 0, 0].reshape(bsz, nh_b))

            sel = _moba_sample_gate(q_s, cache_k, e, tbl_flat, nreq, npages)
            oa = _moba_sample_attn(heads3(q_s), heads3(k_s), heads3(v_s), cache_k, cache_v, e,
                                   tbl_flat, sel.reshape(-1), nreq, npages)
            hm, c1, n1, m1 = _mlstm_sample(zm_s, g_s, gb, ng, state_mlstm_c, n0_s, m0_s, e, nreq)
            mix_s = [pad_rows(oa, a_w), pad_rows(hm, v_w)]
            os_["c"].append(c1)
            os_["n"].append(n1.reshape(nreq, nh_b, -1))
            os_["m"].append(m1[:, :, 0, 0])
        else:
            od = l // 2
            zr_p, zr_s = _proj(up, us, w_in_rg, od, 0, w_in_rg.shape[2], tm_p, tn_p, F32, name="proj_rg")
            y_p, h1, nb = _rglru_prompt(zr_p, od, conv_w, conv_b, w_rg_a, b_rg_a, w_rg_x, b_rg_x,
                                        lru_lambda, bsz, tt_rg)
            op["h"].append(h1.reshape(bsz, -1))
            op["buf"].append(nb)
            y_s, h1, nb = _rglru_sample(zr_s, od, conv_t, state_rglru_h, conv_w, conv_b, w_rg_a,
                                        b_rg_a, w_rg_x, b_rg_x, lru_lambda, nreq)
            os_["h"].append(h1)
            os_["buf"].append(jnp.transpose(nb, (1, 0, 2)))
            mix_p, mix_s = [y_p], [y_s]
        (xp, up), (xs, us) = post(mix_p, mix_s, wb_mix, 1, 1.0, (l, 2))

        act_p, act_s = _swiglu_in(up, us, w_ffn2_in, l, tm, tn)
        (xp, up), (xs, us) = post([act_p], [act_s], wb_ffn2, 2, FFN_RES, (min(l + 1, depth - 1), 0))

    st = jnp.stack
    kv_p = lambda t: t.reshape(n_even, bsz, seq, -1, A_HEAD_DIM)
    kv_s = lambda t: t[:, :nreq].reshape(n_even, nreq, 1, -1, A_HEAD_DIM)
    return (xp.reshape(bsz, seq, d), xs[:nreq].reshape(nreq, 1, d),
            kv_p(k_st[0]), kv_p(v_st[0]), kv_s(k_st[1]), kv_s(v_st[1]),
            st(op["c"]), st(op["n"]), st(op["m"]), st(os_["c"]), st(os_["n"]), st(os_["m"]),
            st(op["h"]), st(op["buf"]), st(os_["h"]), st(os_["buf"]))
```

```python
import functools

import numpy as np
import jax
import jax.numpy as jnp
from jax import lax
from jax.experimental import pallas as pl
from jax.experimental.pallas import tpu as pltpu

F32 = jnp.float32
BF16 = jnp.bfloat16

DEPTH = 4
MOBA_BLOCK = 256
MOBA_TOPK = 3
MOBA_GROUP = 4
MOBA_HEADS_PER_STEP = 4
MLSTM_CHUNKS_PER_STEP = 1
ROPE_THETA = 10000.0
A_HEAD_DIM = 128
B_HEADS = 4
MLSTM_CHUNK = 64
RG_BLOCKS = 16
CONV_WIDTH = 4
RG_C = 8.0
FFN_RES = 0.5
ALPHA = (2.0 * DEPTH) ** 0.25
LN_EPS = 1e-5

LANES = 128
SAMPLE_ROWS = 16
NEG_BIG = -1e30
MIB = 1024 * 1024


def _cparams(n_axes, vmem_mib):
    return pltpu.CompilerParams(
        dimension_semantics=("arbitrary",) * n_axes,
        vmem_limit_bytes=int(vmem_mib * MIB),
    )


def _sigmoid(x):
    return 0.5 * jnp.tanh(0.5 * x) + 0.5


def _softplus(x):
    return jnp.maximum(x, 0.0) + jnp.log1p(jnp.exp(-jnp.abs(x)))


def _gelu_tanh(x):
    return 0.5 * x * (1.0 + jnp.tanh(0.7978845608028654 * (x + 0.044715 * x * x * x)))


def _mods_kernel(c_ref, w_ref, b_ref, o_ref):
    c = c_ref[...]
    s = (c * _sigmoid(c)).astype(BF16)
    o_ref[...] = jnp.dot(s, w_ref[...].astype(BF16), preferred_element_type=F32) + b_ref[...]


def _mods(c_all, w_ada, b_ada):
    depth, k, n = w_ada.shape
    rows = c_all.shape[0]
    tn = 1024 if n % 1024 == 0 else n
    return pl.pallas_call(
        _mods_kernel,
        grid=(depth, n // tn),
        in_specs=[
            pl.BlockSpec((rows, k), lambda l, j: (0, 0)),
            pl.BlockSpec((None, k, tn), lambda l, j: (l, 0, j)),
            pl.BlockSpec((None, 1, tn), lambda l, j: (l, 0, j)),
        ],
        out_specs=pl.BlockSpec((None, rows, tn), lambda l, j: (l, 0, j)),
        out_shape=jax.ShapeDtypeStruct((depth, rows, n), F32),
        compiler_params=_cparams(2, 48),
        name="adaln_mods",
    )(c_all, w_ada, b_ada.reshape(depth, 1, n))


def _modulate_kernel(x_ref, sc_ref, sh_ref, u_ref):
    u_ref[...] = (x_ref[...] * (1.0 + sc_ref[...]) + sh_ref[...]).astype(u_ref.dtype)


def _modulate(x, sc_spec, sh_spec, mods, tm):
    m, d = x.shape
    return pl.pallas_call(
        _modulate_kernel,
        grid=(m // tm,),
        in_specs=[pl.BlockSpec((tm, d), lambda i: (i, 0)), sc_spec(tm), sh_spec(tm)],
        out_specs=pl.BlockSpec((tm, d), lambda i: (i, 0)),
        out_shape=jax.ShapeDtypeStruct((m, d), BF16),
        compiler_params=_cparams(1, 32),
        name="modulate0",
    )(x, mods, mods)


def _rope_store(z, cos, sin, o_ref):
    for j in range(z.shape[1] // A_HEAD_DIM):
        zj = z[:, j * A_HEAD_DIM:(j + 1) * A_HEAD_DIM]
        o_ref[:, j * A_HEAD_DIM:(j + 1) * A_HEAD_DIM] = (
            zj * cos + pltpu.roll(zj, A_HEAD_DIM // 2, 1) * sin).astype(o_ref.dtype)


def _proj_kernel(*refs, rope, n_prev, w_mode):
    if n_prev:
        prev_ref, prevs_ref = refs[len(refs) - 5:len(refs) - 3]
        refs = refs[:len(refs) - 5] + refs[len(refs) - 3:]
    if rope:
        x_ref, xs_ref, w_ref, cos_ref, sin_ref, coss_ref, sins_ref, o_ref, os_ref, wb_ref = refs
    else:
        x_ref, xs_ref, w_ref, o_ref, os_ref, wb_ref = refs
    if n_prev:
        o_ref[0:n_prev] = prev_ref[...]
        o_ref = o_ref.at[n_prev]

        @pl.when(pl.program_id(1) == 0)
        def _():
            os_ref[0:n_prev] = prevs_ref[...]

        os_ref = os_ref.at[n_prev]

    def mm(x):
        if w_mode == "nk_contract_last":
            return lax.dot_general(x, wb_ref[...], (((1,), (1,)), ((), ())), preferred_element_type=F32)
        return jnp.dot(x, wb_ref[...], preferred_element_type=F32)

    @pl.when(pl.program_id(1) == 0)
    def _():
        w = w_ref[...]
        if w_mode == "nk_transpose":
            w = w.T
        wb_ref[...] = w.astype(BF16)
        zs = mm(xs_ref[...])
        if rope:
            _rope_store(zs, coss_ref[...], sins_ref[...], os_ref)
        else:
            os_ref[...] = zs.astype(os_ref.dtype)

    z = mm(x_ref[...])
    if rope:
        _rope_store(z, cos_ref[...], sin_ref[...], o_ref)
    else:
        o_ref[...] = z.astype(o_ref.dtype)


def _swiglu_in_kernel(x_ref, xs_ref, wg_ref, wv_ref, o_ref, os_ref, wgb_ref, wvb_ref):
    def act(x):
        g = jnp.dot(x, wgb_ref[...], preferred_element_type=F32)
        v = jnp.dot(x, wvb_ref[...], preferred_element_type=F32)
        return g * _sigmoid(g) * v

    @pl.when(pl.program_id(1) == 0)
    def _():
        wgb_ref[...] = wg_ref[...].astype(BF16)
        wvb_ref[...] = wv_ref[...].astype(BF16)
        os_ref[...] = act(xs_ref[...]).astype(os_ref.dtype)

    o_ref[...] = act(x_ref[...]).astype(o_ref.dtype)


def _proj(x, xs, w, layer, col0, ncols, tm, tn, out_dtype, rope=None, stack=None, w_t=False,
          name="proj"):
    m, k = x.shape
    rows_s = xs.shape[0]
    c0 = col0 // tn
    if w_t:
        w_spec = pl.BlockSpec((None, tn, k), lambda j, i: (layer, c0 + j, 0))
        w_mode = "nk_transpose" if tn % LANES == 0 else "nk_contract_last"
    else:
        w_spec = pl.BlockSpec((None, k, tn), lambda j, i: (layer, 0, c0 + j))
        w_mode = "kn"
    wb_shape = (tn, k) if w_mode == "nk_contract_last" else (k, tn)
    in_specs = [
        pl.BlockSpec((tm, k), lambda j, i: (i, 0)),
        pl.BlockSpec((rows_s, k), lambda j, i: (0, 0)),
        w_spec,
    ]
    args = [x, xs, w]
    if rope is not None:
        period = rope[0].shape[0] // tm
        in_specs += [pl.BlockSpec((tm, A_HEAD_DIM), lambda j, i: (i % period, 0))] * 2
        in_specs += [pl.BlockSpec((rows_s, A_HEAD_DIM), lambda j, i: (0, 0))] * 2
        args += list(rope)
    n_prev = 0
    if stack is None:
        out_specs = [pl.BlockSpec((tm, tn), lambda j, i: (i, j)),
                     pl.BlockSpec((rows_s, tn), lambda j, i: (0, j))]
        out_shape = [jax.ShapeDtypeStruct((m, ncols), out_dtype),
                     jax.ShapeDtypeStruct((rows_s, ncols), F32)]
    elif len(stack) == 0:
        out_specs = [pl.BlockSpec((None, tm, tn), lambda j, i: (0, i, j)),
                     pl.BlockSpec((None, rows_s, tn), lambda j, i: (0, 0, j))]
        out_shape = [jax.ShapeDtypeStruct((1, m, ncols), out_dtype),
                     jax.ShapeDtypeStruct((1, rows_s, ncols), out_dtype)]
    else:
        n_prev = stack[0].shape[0]
        in_specs += [pl.BlockSpec((n_prev, tm, tn), lambda j, i: (0, i, j)),
                     pl.BlockSpec((n_prev, rows_s, tn), lambda j, i: (0, 0, j))]
        args += list(stack)
        out_specs = [pl.BlockSpec((n_prev + 1, tm, tn), lambda j, i: (0, i, j)),
                     pl.BlockSpec((n_prev + 1, rows_s, tn), lambda j, i: (0, 0, j))]
        out_shape = [jax.ShapeDtypeStruct((n_prev + 1, m, ncols), out_dtype),
                     jax.ShapeDtypeStruct((n_prev + 1, rows_s, ncols), out_dtype)]
    return pl.pallas_call(
        functools.partial(_proj_kernel, rope=rope is not None, n_prev=n_prev, w_mode=w_mode),
        grid=(ncols // tn, m // tm),
        in_specs=in_specs,
        out_specs=out_specs,
        out_shape=out_shape,
        scratch_shapes=[pltpu.VMEM(wb_shape, BF16)],
        compiler_params=_cparams(2, 56),
        name=name,
    )(*args)


def _swiglu_in(x, xs, w, layer, tm, tn):
    m, k = x.shape
    rows_s = xs.shape[0]
    dff = w.shape[2] // 2
    nv = dff // tn
    return pl.pallas_call(
        _swiglu_in_kernel,
        grid=(dff // tn, m // tm),
        in_specs=[
            pl.BlockSpec((tm, k), lambda j, i: (i, 0)),
            pl.BlockSpec((rows_s, k), lambda j, i: (0, 0)),
            pl.BlockSpec((None, k, tn), lambda j, i: (layer, 0, j)),
            pl.BlockSpec((None, k, tn), lambda j, i: (layer, 0, nv + j)),
        ],
        out_specs=[pl.BlockSpec((tm, tn), lambda j, i: (i, j)),
                   pl.BlockSpec((rows_s, tn), lambda j, i: (0, j))],
        out_shape=[jax.ShapeDtypeStruct((m, dff), BF16), jax.ShapeDtypeStruct((rows_s, dff), BF16)],
        scratch_shapes=[pltpu.VMEM((k, tn), BF16), pltpu.VMEM((k, tn), BF16)],
        compiler_params=_cparams(2, 56),
        name="swiglu_in",
    )(x, xs, w, w)


def _cast_kernel(w_ref, o_ref):
    o_ref[...] = w_ref[...].astype(o_ref.dtype)


def _cast_bf16(w, layer, tk):
    _, k, n = w.shape
    return pl.pallas_call(
        _cast_kernel,
        grid=(k // tk,),
        in_specs=[pl.BlockSpec((None, tk, n), lambda i: (layer, i, 0))],
        out_specs=pl.BlockSpec((tk, n), lambda i: (i, 0)),
        out_shape=jax.ShapeDtypeStruct((k, n), BF16),
        compiler_params=_cparams(1, 32),
        name="cast_bf16",
    )(w)


def _out_ln_kernel(*refs, n_parts, res_w):
    ap_refs = refs[:n_parts]
    as_refs = refs[n_parts:2 * n_parts]
    (w_ref, x_ref, gate_ref, sc_ref, sh_ref, xs_ref, gates_ref, scs_ref, shs_ref, lng_ref, lnb_ref,
     xo_ref, uo_ref, xso_ref, uso_ref) = refs[2 * n_parts:]

    def run(a_refs, x_ref, gate_ref, sc_ref, sh_ref, xo_ref, uo_ref):
        acc = None
        off = 0
        for a_ref in a_refs:
            kk = a_ref.shape[1]
            part = jnp.dot(a_ref[...], w_ref[off:off + kk, :], preferred_element_type=F32)
            acc = part if acc is None else acc + part
            off += kk
        y = ALPHA * x_ref[...] + (res_w * (1.0 + gate_ref[...])) * acc
        mu = jnp.mean(y, axis=-1, keepdims=True)
        yc = y - mu
        var = jnp.mean(yc * yc, axis=-1, keepdims=True)
        xh = yc * lax.rsqrt(var + LN_EPS)
        g = lng_ref[...]
        b = lnb_ref[...]
        mod = 1.0 + sc_ref[...]
        xo_ref[...] = xh * g + b
        uo_ref[...] = (xh * (g * mod) + (b * mod + sh_ref[...])).astype(uo_ref.dtype)

    @pl.when(pl.program_id(0) == 0)
    def _():
        run(as_refs, xs_ref, gates_ref, scs_ref, shs_ref, xso_ref, uso_ref)

    run(ap_refs, x_ref, gate_ref, sc_ref, sh_ref, xo_ref, uo_ref)


def _out_ln(parts, parts_s, wb, x, xs, mods, mods_s, specs, specs_s, ln_g, ln_b, ln_row, res_w, tm):
    m = parts[0].shape[0]
    rows_s = xs.shape[0]
    k, d = wb.shape
    ln_spec = pl.BlockSpec((None, 1, d), lambda i: (ln_row, 0, 0))
    row = lambda width: pl.BlockSpec((tm, width), lambda i: (i, 0))
    whole = lambda width: pl.BlockSpec((rows_s, width), lambda i: (0, 0))
    return pl.pallas_call(
        functools.partial(_out_ln_kernel, n_parts=len(parts), res_w=res_w),
        grid=(m // tm,),
        in_specs=[row(p.shape[1]) for p in parts] + [whole(p.shape[1]) for p in parts_s] + [
            pl.BlockSpec((k, d), lambda i: (0, 0), pipeline_mode=pl.Buffered(1)),
            row(d)] + [s(tm) for s in specs] + [whole(d)] + [s(rows_s) for s in specs_s] + [
            ln_spec, ln_spec],
        out_specs=[row(d), row(d), whole(d), whole(d)],
        out_shape=[jax.ShapeDtypeStruct((m, d), F32), jax.ShapeDtypeStruct((m, d), BF16),
                   jax.ShapeDtypeStruct((rows_s, d), F32), jax.ShapeDtypeStruct((rows_s, d), BF16)],
        compiler_params=_cparams(1, 56),
        name="out_postnorm",
    )(*parts, *parts_s, wb, x, mods, mods, mods, xs, mods_s, mods_s, mods_s, ln_g, ln_b)


def _moba_prompt_kernel(q_ref, k_ref, v_ref, o_ref, kb_ref, vb_ref, kmean_ref,
                        m_ref, acc_ref, *, nblk, scale):
    qi = pl.program_id(2)
    blk = MOBA_BLOCK
    hd = A_HEAD_DIM
    nheads = q_ref.shape[1] // hd
    lanes = lambda hh: slice(hh * hd, (hh + 1) * hd)

    @pl.when(qi == 0)
    def _():
        seq = k_ref.shape[0]
        r = lax.broadcasted_iota(jnp.int32, (seq, hd), 0)
        c = lax.broadcasted_iota(jnp.int32, (seq, hd), 1)
        in_blk = (r >= c * blk) & (r < c * blk + blk)
        onehot = jnp.where(in_blk, 1.0, 0.0).astype(BF16)
        kmean_ref[...] = jnp.zeros_like(kmean_ref)
        for hh in range(nheads):
            kf = k_ref[:, lanes(hh)]
            kb_ref[hh, :, 0:hd] = kf.astype(BF16)
            kb_ref[hh, :, hd:2 * hd] = onehot
            vb_ref[hh, :, 0:hd] = v_ref[:, lanes(hh)].astype(BF16)
            vb_ref[hh, :, hd:2 * hd] = jnp.ones((seq, hd), BF16)
            kmean_ref[hh, 0:nblk, :] = jnp.mean(kf.reshape(nblk, blk, hd), axis=1)

    nt = (((1,), (1,)), ((), ()))
    start = pl.multiple_of(qi * blk, blk)
    q_augs = []
    for hh in range(nheads):
        q = q_ref[:, lanes(hh)]
        gate = lax.dot_general(kmean_ref[hh, 0:nblk, :], q, nt, precision=lax.Precision.HIGHEST,
                               preferred_element_type=F32)
        bi = lax.broadcasted_iota(jnp.int32, gate.shape, 0)
        bf = bi.astype(F32)
        g = jnp.where(bi < qi, gate, -jnp.inf)
        sel = jnp.zeros(gate.shape, F32)
        for _ in range(MOBA_TOPK):
            mx = jnp.max(g, axis=0, keepdims=True)
            cand = jnp.where(g == mx, bf, 1e9)
            cand = jnp.where(mx > -jnp.inf, cand, 1e9)
            pick = bf == jnp.min(cand, axis=0, keepdims=True)
            sel = jnp.where(pick, 1.0, sel)
            g = jnp.where(pick, -jnp.inf, g)
        qb = q.astype(BF16)
        neg = jnp.concatenate([(1.0 - sel) * NEG_BIG, jnp.zeros((hd - nblk, blk), F32)], axis=0).T
        q_augs.append(jnp.concatenate([qb, neg.astype(BF16)], axis=1))
        s = lax.dot_general(qb, kb_ref[hh, pl.ds(start, blk), 0:hd], nt, preferred_element_type=F32) * scale
        row = lax.broadcasted_iota(jnp.int32, s.shape, 0)
        col = lax.broadcasted_iota(jnp.int32, s.shape, 1)
        s = jnp.where(col <= row, s, NEG_BIG)
        m0 = jnp.max(s, axis=1, keepdims=True)
        p = jnp.exp(s - m0)
        m_ref[hh] = jnp.broadcast_to(m0, m_ref.shape[1:])
        acc_ref[hh] = jnp.dot(p.astype(BF16), vb_ref[hh, pl.ds(start, blk), :], preferred_element_type=F32)

    grp = MOBA_GROUP
    span = grp * blk

    def past_group(gi, carry):
        st = pl.multiple_of(gi * span, span)
        for hh in range(nheads):
            sn = lax.dot_general(q_augs[hh], kb_ref[hh, pl.ds(st, span), :], nt,
                                 preferred_element_type=F32) * scale
            m_prev = m_ref[hh]
            m_new = jnp.maximum(m_prev, jnp.max(sn, axis=1, keepdims=True))
            a = jnp.exp(m_prev - m_new)
            pn = jnp.exp(sn - jnp.concatenate([m_new] * (span // hd), axis=1))
            acc_ref[hh] = jnp.concatenate([a, a], axis=1) * acc_ref[hh] + jnp.dot(
                pn.astype(BF16), vb_ref[hh, pl.ds(st, span), :], preferred_element_type=F32)
            m_ref[hh] = m_new
        return carry

    lax.fori_loop(0, (qi + grp - 1) // grp, past_group, 0)
    for hh in range(nheads):
        acc = acc_ref[hh]
        o_ref[:, lanes(hh)] = (acc[:, 0:hd] / acc[:, hd:2 * hd]).astype(o_ref.dtype)


def _moba_prompt(q, k, v, slab, bsz):
    rows, width = q.shape
    seq = rows // bsz
    nh = width // A_HEAD_DIM
    nblk = seq // MOBA_BLOCK
    assert nblk % MOBA_GROUP == 0
    blk = MOBA_BLOCK
    hd = A_HEAD_DIM
    hps = MOBA_HEADS_PER_STEP
    assert nh % hps == 0
    return pl.pallas_call(
        functools.partial(_moba_prompt_kernel, nblk=nblk, scale=hd ** -0.5),
        grid=(bsz, nh // hps, nblk),
        in_specs=[
            pl.BlockSpec((blk, hps * hd), lambda b, h, i: (b * nblk + i, h)),
            pl.BlockSpec((None, seq, hps * hd), lambda b, h, i: (slab, b, h), pipeline_mode=pl.Buffered(1)),
            pl.BlockSpec((None, seq, hps * hd), lambda b, h, i: (slab, b, h), pipeline_mode=pl.Buffered(1)),
        ],
        out_specs=pl.BlockSpec((blk, hps * hd), lambda b, h, i: (b * nblk + i, h)),
        out_shape=jax.ShapeDtypeStruct((rows, width), BF16),
        scratch_shapes=[
            pltpu.VMEM((hps, seq, 2 * hd), BF16), pltpu.VMEM((hps, seq, 2 * hd), BF16),
            pltpu.VMEM((hps, LANES, hd), F32),
            pltpu.VMEM((hps, blk, hd), F32), pltpu.VMEM((hps, blk, 2 * hd), F32),
        ],
        compiler_params=_cparams(3, 52),
        name="moba_prompt",
    )(q, k, v)


def _mlstm_prompt_kernel(q_ref, k_ref, v_ref, og_ref, g_ref, gb_ref, ng_ref,
                         h_ref, c_ref, n_ref, m_ref, *, nb, nh, dk, dv):
    lc = MLSTM_CHUNK
    chunks = q_ref.shape[1] // lc

    @pl.when(pl.program_id(0) == 0)
    def _():
        c_ref[...] = jnp.zeros_like(c_ref)
        n_ref[...] = jnp.zeros_like(n_ref)
        m_ref[...] = jnp.zeros_like(m_ref)

    tt = lax.broadcasted_iota(jnp.int32, (lc, lc), 0)
    ss = lax.broadcasted_iota(jnp.int32, (lc, lc), 1)
    causal = ss <= tt
    eye = ss == tt
    nt = (((1,), (1,)), ((), ()))
    tn = (((0,), (0,)), ((), ()))
    gb = gb_ref[...]
    for cc in range(chunks):
        rows = slice(cc * lc, (cc + 1) * lc)
        for b in range(nb):
            gates = g_ref[b, rows, :] + gb
            log_f = jnp.minimum(gates, 0.0) - jnp.log1p(jnp.exp(-jnp.abs(gates)))
            for h in range(nh):
                bh = b * nh + h
                i_col = gates[:, h:h + 1]
                f_col = log_f[:, nh + h:nh + h + 1]
                f_row = jnp.sum(jnp.where(eye, f_col, 0.0), axis=0, keepdims=True)
                i_row = jnp.sum(jnp.where(eye, i_col, 0.0), axis=0, keepdims=True)
                b_col = jnp.sum(jnp.where(causal, f_row, 0.0), axis=1, keepdims=True)
                b_row = jnp.sum(jnp.where(ss >= tt, f_col, 0.0), axis=0, keepdims=True)
                m_prev = m_ref[bh][:, 0:1]
                d = jnp.where(causal, b_col - b_row + i_row, NEG_BIG)
                inter = b_col + m_prev
                m_t = jnp.maximum(jnp.max(d, axis=1, keepdims=True), inter)
                w = jnp.exp(d - m_t)
                q = q_ref[b, rows, h * dk:(h + 1) * dk].astype(F32)
                k = k_ref[b, rows, h * dk:(h + 1) * dk].astype(F32) * (dk ** -0.5)
                vb = v_ref[b, rows, h * dv:(h + 1) * dv].astype(BF16)
                qb = q.astype(BF16)
                s = lax.dot_general(qb, k.astype(BF16), nt, preferred_element_type=F32) * w
                wi = jnp.exp(inter - m_t)
                cst = c_ref[bh]
                nrow = n_ref[bh]
                num = wi * jnp.dot(qb, cst.astype(BF16), preferred_element_type=F32) + jnp.dot(
                    s.astype(BF16), vb, preferred_element_type=F32)
                den = wi * jnp.sum(q * nrow, axis=1, keepdims=True) + jnp.sum(s, axis=1, keepdims=True)
                hh = num / jnp.maximum(jnp.abs(den), jnp.exp(-m_t))
                b_last = b_col[lc - 1:lc, :]
                g_col = b_last - b_col + i_col
                m_new = jnp.maximum(b_last + m_prev, jnp.max(g_col, axis=0, keepdims=True))
                wc = jnp.exp(b_last + m_prev - m_new)
                kw = k * jnp.exp(g_col - m_new)
                c_ref[bh] = wc * cst + lax.dot_general(kw.astype(BF16), vb, tn, preferred_element_type=F32)
                n_ref[bh] = wc * nrow + jnp.sum(kw, axis=0, keepdims=True)
                m_ref[bh] = jnp.broadcast_to(m_new, (1, LANES))
                hn = hh * lax.rsqrt(jnp.mean(hh * hh, axis=1, keepdims=True) + LN_EPS)
                hn = hn * ng_ref[:, h * dv:(h + 1) * dv] * _sigmoid(
                    og_ref[b, rows, h * dv:(h + 1) * dv].astype(F32))
                h_ref[b, rows, h * dv:(h + 1) * dv] = hn.astype(h_ref.dtype)


def _mlstm_prompt(zm, gates, gate_bias, norm_g, bsz):
    rows, _ = zm.shape
    seq = rows // bsz
    nh = B_HEADS
    v_w = norm_g.shape[1]
    qk_w = v_w // 2
    dk, dv = qk_w // nh, v_w // nh
    lc = MLSTM_CHUNK * MLSTM_CHUNKS_PER_STEP
    assert seq % lc == 0
    z3 = zm.reshape(bsz, seq, zm.shape[1])
    gw = gates.shape[1]
    g3 = gates.reshape(bsz, seq, gw)
    nbh = bsz * nh
    full = lambda shape: pl.BlockSpec(shape, lambda c: (0,) * len(shape))
    return pl.pallas_call(
        functools.partial(_mlstm_prompt_kernel, nb=bsz, nh=nh, dk=dk, dv=dv),
        grid=(seq // lc,),
        in_specs=[
            pl.BlockSpec((bsz, lc, qk_w), lambda c: (0, c, 0)),
            pl.BlockSpec((bsz, lc, qk_w), lambda c: (0, c, 1)),
            pl.BlockSpec((bsz, lc, v_w), lambda c: (0, c, 1)),
            pl.BlockSpec((bsz, lc, v_w), lambda c: (0, c, 2)),
            pl.BlockSpec((bsz, lc, gw), lambda c: (0, c, 0)),
            full((1, gw)), full((1, v_w)),
        ],
        out_specs=[
            pl.BlockSpec((bsz, lc, v_w), lambda c: (0, c, 0)),
            full((nbh, dk, dv)), full((nbh, 1, dk)), full((nbh, 1, LANES)),
        ],
        out_shape=[
            jax.ShapeDtypeStruct((bsz, seq, v_w), BF16),
            jax.ShapeDtypeStruct((nbh, dk, dv), F32),
            jax.ShapeDtypeStruct((nbh, 1, dk), F32),
            jax.ShapeDtypeStruct((nbh, 1, LANES), F32),
        ],
        compiler_params=_cparams(1, 32),
        name="mlstm_prompt",
    )(z3, z3, z3, z3, g3, gate_bias, norm_g)


def _rglru_gates(xconv, wa_ref, ba, wx_ref, bx, lam):
    bd = wa_ref.shape[1]
    r_parts, i_parts = [], []
    for n in range(wa_ref.shape[0]):
        xb = xconv[:, n * bd:(n + 1) * bd].astype(BF16)
        r_parts.append(jnp.dot(xb, wa_ref[n].astype(BF16), preferred_element_type=F32))
        i_parts.append(jnp.dot(xb, wx_ref[n].astype(BF16), preferred_element_type=F32))
    r = _sigmoid(jnp.concatenate(r_parts, axis=1) + ba)
    ig = _sigmoid(jnp.concatenate(i_parts, axis=1) + bx)
    log_a = (-RG_C) * r * _softplus(-lam)
    a = jnp.exp(log_a)
    mult = jnp.sqrt(1.0 - a * a)
    return a, mult * ig * xconv


def _rglru_prompt_kernel(gate_ref, xr_ref, cw_ref, cb_ref, wa_ref, ba_ref, wx_ref, bx_ref, lam_ref,
                         y_ref, hl_ref, buf_ref, xext_ref, a_ref, b_ref, hs_ref, hc_ref):
    t = pl.program_id(1)
    tt = xr_ref.shape[0]
    pad = 8
    nbuf = CONV_WIDTH - 1

    @pl.when(t == 0)
    def _():
        xext_ref[0:pad, :] = jnp.zeros((pad, xext_ref.shape[1]), F32)
        hc_ref[...] = jnp.zeros_like(hc_ref)

    @pl.when(t > 0)
    def _():
        xext_ref[0:pad, :] = xext_ref[tt:tt + pad, :]

    xr = xr_ref[...].astype(F32)
    xext_ref[pad:pad + tt, :] = xr
    xconv = cb_ref[...] + xext_ref[pad - nbuf:pad - nbuf + tt, :] * cw_ref[0:1, :]
    for j in range(1, CONV_WIDTH):
        xconv = xconv + xext_ref[pad - nbuf + j:pad - nbuf + j + tt, :] * cw_ref[j:j + 1, :]
    a, bterm = _rglru_gates(xconv, wa_ref, ba_ref[...], wx_ref, bx_ref[...], lam_ref[...])
    a_ref[...] = a
    b_ref[...] = bterm

    def step(i, h):
        h = a_ref[pl.ds(i, 1), :] * h + b_ref[pl.ds(i, 1), :]
        hs_ref[pl.ds(i, 1), :] = h
        return h

    h_last = lax.fori_loop(0, tt, step, hc_ref[...], unroll=8)
    hc_ref[...] = h_last
    y_ref[...] = (_gelu_tanh(gate_ref[...].astype(F32)) * hs_ref[...]).astype(y_ref.dtype)
    hl_ref[...] = h_last
    buf_ref[...] = xr[tt - nbuf:tt, :]


def _rglru_prompt(zr, layer, conv_w, conv_b, w_a, b_a, w_x, b_x, lam, bsz, tt):
    rows, w2 = zr.shape
    width = w2 // 2
    seq = rows // bsz
    nt = seq // tt
    nblk, bd = w_a.shape[1], w_a.shape[2]
    vec = lambda a: a.reshape(a.shape[0], 1, width)
    vspec = pl.BlockSpec((None, 1, width), lambda b, t: (layer, 0, 0))
    wspec = pl.BlockSpec((None, nblk, bd, bd), lambda b, t: (layer, 0, 0, 0))
    nbuf = CONV_WIDTH - 1
    return pl.pallas_call(
        _rglru_prompt_kernel,
        grid=(bsz, nt),
        in_specs=[
            pl.BlockSpec((tt, width), lambda b, t: (b * nt + t, 0)),
            pl.BlockSpec((tt, width), lambda b, t: (b * nt + t, 1)),
            pl.BlockSpec((None, CONV_WIDTH, width), lambda b, t: (layer, 0, 0)),
            vspec, wspec, vspec, wspec, vspec, vspec,
        ],
        out_specs=[
            pl.BlockSpec((tt, width), lambda b, t: (b * nt + t, 0)),
            pl.BlockSpec((None, 1, width), lambda b, t: (b, 0, 0)),
            pl.BlockSpec((None, nbuf, width), lambda b, t: (b, 0, 0)),
        ],
        out_shape=[
            jax.ShapeDtypeStruct((rows, width), BF16),
            jax.ShapeDtypeStruct((bsz, 1, width), F32),
            jax.ShapeDtypeStruct((bsz, nbuf, width), F32),
        ],
        scratch_shapes=[
            pltpu.VMEM((tt + 8, width), F32), pltpu.VMEM((tt, width), F32),
            pltpu.VMEM((tt, width), F32), pltpu.VMEM((tt, width), F32), pltpu.VMEM((1, width), F32),
        ],
        compiler_params=_cparams(2, 48),
        name="rglru_prompt",
    )(zr, zr, conv_w, vec(conv_b), w_a, vec(b_a), w_x, vec(b_x), vec(lam))


def _moba_gate_kernel(tbl_ref, q_ref, *refs, nheads, nsteps, pages_per_step, pages_per_blk):
    b = pl.program_id(0)
    st = pl.program_id(1)
    hd = A_HEAD_DIM
    k_refs = refs[:pages_per_step]
    sel_ref, ksum_ref = refs[pages_per_step:]
    page = k_refs[0].shape[0]
    blks_per_step = pages_per_step // pages_per_blk
    for i in range(blks_per_step):
        tot = jnp.sum(k_refs[i * pages_per_blk][...], axis=0)
        for u in range(1, pages_per_blk):
            tot = tot + jnp.sum(k_refs[i * pages_per_blk + u][...], axis=0)
        ksum_ref[st * blks_per_step + i] = tot

    @pl.when(st == nsteps - 1)
    def _():
        nblk = ksum_ref.shape[0]
        inv = 1.0 / (pages_per_blk * page)
        rowi = lax.broadcasted_iota(jnp.int32, (nblk, 1), 0).astype(F32)
        out_r = lax.broadcasted_iota(jnp.int32, sel_ref.shape, 0)
        out_c = lax.broadcasted_iota(jnp.int32, sel_ref.shape, 1)
        out = jnp.zeros(sel_ref.shape, jnp.int32)
        qrow = q_ref[pl.ds(b, 1), :]
        for h in range(nheads):
            km = ksum_ref[:, h, :] * inv
            g = jnp.sum(km * qrow[:, h * hd:(h + 1) * hd], axis=1, keepdims=True)
            for i in range(MOBA_TOPK):
                mx = jnp.max(g, axis=0, keepdims=True)
                idx = jnp.min(jnp.where(g == mx, rowi, 1e9), axis=0, keepdims=True)
                out = jnp.where((out_r == h) & (out_c == i), idx.astype(jnp.int32), out)
                g = jnp.where(rowi == idx, -jnp.inf, g)
        sel_ref[...] = out


GATE_PAGES_PER_STEP = 16


def _moba_sample_gate(q, cache_k, layer, tbl_flat, nreq, npages):
    page, nheads, hd = cache_k.shape[2:]
    ppb = MOBA_BLOCK // page
    nblk = npages // ppb
    pps = GATE_PAGES_PER_STEP
    nsteps = npages // pps

    def page_spec(u):
        return pl.BlockSpec((None, None, page, nheads, hd),
                            lambda b, s, tbl: (layer, tbl[b * npages + s * pps + u], 0, 0, 0))

    grid_spec = pltpu.PrefetchScalarGridSpec(
        num_scalar_prefetch=1,
        grid=(nreq, nsteps),
        in_specs=[pl.BlockSpec(q.shape, lambda b, s, tbl: (0, 0))] + [page_spec(u) for u in range(pps)],
        out_specs=pl.BlockSpec((None, nheads, LANES), lambda b, s, tbl: (b, 0, 0)),
        scratch_shapes=[pltpu.VMEM((nblk, nheads, hd), F32)],
    )
    return pl.pallas_call(
        functools.partial(_moba_gate_kernel, nheads=nheads, nsteps=nsteps, pages_per_step=pps,
                          pages_per_blk=ppb),
        grid_spec=grid_spec,
        out_shape=jax.ShapeDtypeStruct((nreq, nheads, LANES), jnp.int32),
        compiler_params=_cparams(2, 32),
        name="moba_sample_gate",
    )(tbl_flat, q, *([cache_k] * pps))


def _moba_sample_attn_kernel(tbl_ref, sel_ref, q_ref, kn_ref, vn_ref, *refs, npg, hps, scale):
    o_ref = refs[2 * npg * hps]
    b = pl.program_id(0)
    nheads = q_ref.shape[1]
    hd = q_ref.shape[2]
    for hh in range(hps):
        k_refs = refs[hh * npg:(hh + 1) * npg]
        v_refs = refs[(hps + hh) * npg:(hps + hh + 1) * npg]
        h = pl.program_id(1) * hps + hh
        is_h = lax.broadcasted_iota(jnp.int32, (1, nheads, 1), 1) == h
        q_m = jnp.where(is_h, q_ref[b][None], 0.0)

        def score(kp):
            part = jnp.sum(kp * q_m, axis=2, keepdims=True)
            return jnp.sum(part, axis=1, keepdims=True) * scale

        s_own = score(kn_ref[b][None])
        ss = [score(k_ref[...]) for k_ref in k_refs]
        m = s_own
        for s in ss:
            m = jnp.maximum(m, jnp.max(s, axis=0, keepdims=True))
        l = jnp.exp(s_own - m)
        acc = l * vn_ref[b][None]
        for s, v_ref in zip(ss, v_refs):
            p = jnp.exp(s - m)
            l = l + jnp.sum(p, axis=0, keepdims=True)
            acc = acc + jnp.sum(p * v_ref[...], axis=0, keepdims=True)
        out = jnp.sum(jnp.where(is_h, acc / l, 0.0), axis=1)
        o_ref[:, hh * hd:(hh + 1) * hd] = out.astype(o_ref.dtype)


SAMPLE_ATTN_HEADS_PER_STEP = 2


def _moba_sample_attn(q, k_new, v_new, cache_k, cache_v, layer, tbl_flat, sel_flat, nreq, npages):
    page, nheads, hd = cache_k.shape[2:]
    ppb = MOBA_BLOCK // page
    npg = MOBA_TOPK * ppb
    hps = SAMPLE_ATTN_HEADS_PER_STEP
    assert nheads % hps == 0

    def page_spec(hh, j):
        def page_map(b, g, tbl, sel):
            blk = sel[(b * nheads + g * hps + hh) * LANES + j // ppb]
            return (layer, tbl[b * npages + blk * ppb + j % ppb], 0, 0, 0)
        return pl.BlockSpec((None, None, page, nheads, hd), page_map)

    head = pl.BlockSpec(q.shape, lambda b, g, tbl, sel: (0, 0, 0))
    pages = [page_spec(hh, j) for hh in range(hps) for j in range(npg)]
    grid_spec = pltpu.PrefetchScalarGridSpec(
        num_scalar_prefetch=2,
        grid=(nreq, nheads // hps),
        in_specs=[head, head, head] + pages + pages,
        out_specs=pl.BlockSpec((None, 1, hps * hd), lambda b, g, tbl, sel: (b, 0, g)),
    )
    n_in = npg * hps
    return pl.pallas_call(
        functools.partial(_moba_sample_attn_kernel, npg=npg, hps=hps, scale=hd ** -0.5),
        grid_spec=grid_spec,
        out_shape=jax.ShapeDtypeStruct((nreq, 1, nheads * hd), F32),
        compiler_params=_cparams(2, 48),
        name="moba_sample_attn",
    )(tbl_flat, sel_flat, q, k_new, v_new, *([cache_k] * n_in), *([cache_v] * n_in))


def _mlstm_sample_kernel(q_ref, k_ref, v_ref, og_ref, g_ref, gb_ref, ng_ref, c0_ref, n0_ref, m0_ref,
                         h_ref, c_ref, n_ref, m_ref, *, nh, dk, dv):
    b = pl.program_id(0)
    rr = lax.broadcasted_iota(jnp.int32, (dk, dk), 0)
    cc = lax.broadcasted_iota(jnp.int32, (dk, dk), 1)
    eye = rr == cc
    gates = g_ref[pl.ds(b, 1), :] + gb_ref[...]
    q_all = q_ref[pl.ds(b, 1), :]
    k_all = k_ref[pl.ds(b, 1), :]
    v_all = v_ref[pl.ds(b, 1), :]
    og_all = og_ref[pl.ds(b, 1), :]
    for h in range(nh):
        ii = gates[:, h:h + 1]
        fpre = gates[:, nh + h:nh + h + 1]
        ff = jnp.minimum(fpre, 0.0) - jnp.log1p(jnp.exp(-jnp.abs(fpre)))
        m_prev = m0_ref[h][:, 0:1]
        q = q_all[:, h * dk:(h + 1) * dk]
        k = k_all[:, h * dk:(h + 1) * dk] * (dk ** -0.5)
        v = v_all[:, h * dv:(h + 1) * dv]
        inter = ff + m_prev
        m_t = jnp.maximum(ii, inter)
        s = jnp.sum(q * k, axis=1, keepdims=True) * jnp.exp(ii - m_t)
        wi = jnp.exp(inter - m_t)
        q_col = jnp.sum(jnp.where(eye, q, 0.0), axis=1, keepdims=True)
        k_col = jnp.sum(jnp.where(eye, k, 0.0), axis=1, keepdims=True)
        cst = c0_ref[h]
        nrow = n0_ref[h]
        num = wi * jnp.sum(q_col * cst, axis=0, keepdims=True) + s * v
        den = wi * jnp.sum(q * nrow, axis=1, keepdims=True) + s
        hh = num / jnp.maximum(jnp.abs(den), jnp.exp(-m_t))
        m_new = m_t
        wc = jnp.exp(inter - m_new)
        wg = jnp.exp(ii - m_new)
        c_ref[h] = wc * cst + (wg * k_col) * v
        n_ref[h] = wc * nrow + wg * k
        m_ref[h] = jnp.broadcast_to(m_new, (1, LANES))
        hn = hh * lax.rsqrt(jnp.mean(hh * hh, axis=1, keepdims=True) + LN_EPS)
        hn = hn * ng_ref[:, h * dv:(h + 1) * dv] * _sigmoid(og_all[:, h * dv:(h + 1) * dv])
        h_ref[:, h * dv:(h + 1) * dv] = hn.astype(h_ref.dtype)


def _mlstm_sample(zm, gates, gate_bias, norm_g, c0, n0, m0, layer, nreq):
    nh = B_HEADS
    v_w = norm_g.shape[1]
    qk_w = v_w // 2
    dk, dv = qk_w // nh, v_w // nh
    rows = zm.shape[0]
    full = lambda shape: pl.BlockSpec(shape, lambda b: (0,) * len(shape))
    return pl.pallas_call(
        functools.partial(_mlstm_sample_kernel, nh=nh, dk=dk, dv=dv),
        grid=(nreq,),
        in_specs=[
            pl.BlockSpec((rows, qk_w), lambda b: (0, 0)),
            pl.BlockSpec((rows, qk_w), lambda b: (0, 1)),
            pl.BlockSpec((rows, v_w), lambda b: (0, 1)),
            pl.BlockSpec((rows, v_w), lambda b: (0, 2)),
            full(gates.shape), full(gate_bias.shape), full((1, v_w)),
            pl.BlockSpec((None, None, nh, dk, dv), lambda b: (layer, b, 0, 0, 0)),
            pl.BlockSpec((None, None, nh, 1, dk), lambda b: (layer, b, 0, 0, 0)),
            pl.BlockSpec((None, None, nh, 1, LANES), lambda b: (layer, b, 0, 0, 0)),
        ],
        out_specs=[
            pl.BlockSpec((None, 1, v_w), lambda b: (b, 0, 0)),
            pl.BlockSpec((None, nh, dk, dv), lambda b: (b, 0, 0, 0)),
            pl.BlockSpec((None, nh, 1, dk), lambda b: (b, 0, 0, 0)),
            pl.BlockSpec((None, nh, 1, LANES), lambda b: (b, 0, 0, 0)),
        ],
        out_shape=[
            jax.ShapeDtypeStruct((nreq, 1, v_w), F32),
            jax.ShapeDtypeStruct((nreq, nh, dk, dv), F32),
            jax.ShapeDtypeStruct((nreq, nh, 1, dk), F32),
            jax.ShapeDtypeStruct((nreq, nh, 1, LANES), F32),
        ],
        compiler_params=_cparams(1, 32),
        name="mlstm_sample",
    )(zm, zm, zm, zm, gates, gate_bias, norm_g, c0, n0, m0)


def _rglru_sample_kernel(gate_ref, xr_ref, buf_ref, h0_ref, cw_ref, cb_ref, wa_ref, ba_ref, wx_ref,
                         bx_ref, lam_ref, y_ref, h_ref, nbuf_ref):
    nbuf = CONV_WIDTH - 1
    xr = xr_ref[...]
    xconv = cb_ref[...] + buf_ref[0] * cw_ref[0:1, :]
    for j in range(1, nbuf):
        xconv = xconv + buf_ref[j] * cw_ref[j:j + 1, :]
    xconv = xconv + xr * cw_ref[nbuf:nbuf + 1, :]
    a, bterm = _rglru_gates(xconv, wa_ref, ba_ref[...], wx_ref, bx_ref[...], lam_ref[...])
    h = a * h0_ref[...] + bterm
    h_ref[...] = h
    y = _gelu_tanh(gate_ref[...]) * h
    pad_rows = y_ref.shape[0] - y.shape[0]
    y_ref[...] = jnp.concatenate([y, jnp.zeros((pad_rows, y.shape[1]), F32)], axis=0).astype(y_ref.dtype)
    for j in range(nbuf - 1):
        nbuf_ref[j] = buf_ref[j + 1]
    nbuf_ref[nbuf - 1] = xr


def _rglru_sample(zr, layer, buf_t, h0, conv_w, conv_b, w_a, b_a, w_x, b_x, lam, nreq):
    rows, w2 = zr.shape
    width = w2 // 2
    nblk, bd = w_a.shape[1], w_a.shape[2]
    nbuf = CONV_WIDTH - 1
    vec = lambda a: a.reshape(a.shape[0], 1, width)
    vspec = pl.BlockSpec((None, 1, width), lambda i: (layer, 0, 0))
    wspec = pl.BlockSpec((None, nblk, bd, bd), lambda i: (layer, 0, 0, 0))
    return pl.pallas_call(
        _rglru_sample_kernel,
        grid=(1,),
        in_specs=[
            pl.BlockSpec((nreq, width), lambda i: (0, 0)),
            pl.BlockSpec((nreq, width), lambda i: (0, 1)),
            pl.BlockSpec((None, nbuf, nreq, width), lambda i: (layer, 0, 0, 0)),
            pl.BlockSpec((None, nreq, width), lambda i: (layer, 0, 0)),
            pl.BlockSpec((None, CONV_WIDTH, width), lambda i: (layer, 0, 0)),
            vspec, wspec, vspec, wspec, vspec, vspec,
        ],
        out_specs=[
            pl.BlockSpec((rows, width), lambda i: (0, 0)),
            pl.BlockSpec((nreq, width), lambda i: (0, 0)),
            pl.BlockSpec((nbuf, nreq, width), lambda i: (0, 0, 0)),
        ],
        out_shape=[
            jax.ShapeDtypeStruct((rows, width), BF16),
            jax.ShapeDtypeStruct((nreq, width), F32),
            jax.ShapeDtypeStruct((nbuf, nreq, width), F32),
        ],
        compiler_params=_cparams(1, 32),
        name="rglru_sample",
    )(zr, zr, buf_t, h0, conv_w, vec(conv_b), w_a, vec(b_a), w_x, vec(b_x), vec(lam))


def _rope_tables(pos):
    half = A_HEAD_DIM // 2
    inv = ROPE_THETA ** (-jnp.arange(half, dtype=F32) / half)
    ang = pos.astype(F32)[:, None] * inv[None, :]
    cos, sin = jnp.cos(ang), jnp.sin(ang)
    return jnp.concatenate([cos, cos], axis=1), jnp.concatenate([-sin, sin], axis=1)


def kernel(x_prompt, x_sample, cache_k, cache_v, state_mlstm_c, state_mlstm_n, state_mlstm_m,
           state_rglru_h, state_conv, page_table, c_prompt, c_sample, w_ada, b_ada, ln_g, ln_b,
           w_ffn1_in, w_ffn1_out, w_ffn2_in, w_ffn2_out, w_in_ab, b_igate, b_fgate, mlstm_norm_g,
           w_out_ab, w_in_rg, conv_w, conv_b, w_rg_a, b_rg_a, w_rg_x, b_rg_x, lru_lambda, w_out_rg):
    bsz, seq, d = x_prompt.shape
    nreq = x_sample.shape[0]
    depth = w_ada.shape[0]
    n_even = w_in_ab.shape[0]
    a_w = cache_k.shape[3] * cache_k.shape[4]
    v_w = mlstm_norm_g.shape[1]
    qk_w = v_w // 2
    m_w = 2 * qk_w + 2 * v_w
    nh_b = B_HEADS
    past_len = page_table.shape[1] * cache_k.shape[2]
    npages = page_table.shape[1]
    rows_p = bsz * seq
    rs = SAMPLE_ROWS

    tm = 1024
    tn = 512
    tm_p = 1024
    tn_p = 1024
    tm_out = 256
    tk_cast = 512
    tt_rg = 256

    c_all = jnp.concatenate([c_prompt, c_sample, jnp.zeros((rs - bsz - nreq, d), F32)], axis=0)
    mods = _mods(c_all, w_ada, b_ada).reshape(depth, rs, 9, d)
    mods_p = jnp.transpose(mods[:, :bsz], (0, 2, 1, 3)).reshape(depth * 9 * bsz, 1, d)
    mods_s = jnp.transpose(mods[:, bsz:bsz + nreq], (0, 2, 1, 3))
    mods_s = jnp.pad(mods_s, ((0, 0), (0, 0), (0, rs - nreq), (0, 0))).reshape(depth * 9, rs, d)

    def mod_spec_p(l, s, j):
        base = ((l * 3 + s) * 3 + j) * bsz

        def make(tile_rows):
            tiles_per_batch = seq // tile_rows
            return pl.BlockSpec((None, 1, d), lambda i: (base + i // tiles_per_batch, 0, 0))
        return make

    def mod_spec_s(l, s, j):
        row = (l * 3 + s) * 3 + j

        def make(tile_rows):
            return pl.BlockSpec((None, rs, d), lambda i: (row, 0, 0))
        return make

    ln_g3 = ln_g.reshape(depth * 3, 1, d)
    ln_b3 = ln_b.reshape(depth * 3, 1, d)

    cos_p, sin_p = _rope_tables(jnp.arange(seq, dtype=jnp.int32))
    cos_s, sin_s = _rope_tables(jnp.full((rs,), past_len, dtype=jnp.int32))

    tbl_flat = page_table.reshape(-1).astype(jnp.int32)
    n0_s = state_mlstm_n[:, :, :, None, :]
    m0_s = jnp.broadcast_to(state_mlstm_m[:, :, :, None, None], state_mlstm_m.shape + (1, LANES))
    conv_t = jnp.transpose(state_conv, (0, 2, 1, 3))
    gate_bias = jnp.concatenate([b_igate, b_fgate], axis=1)
    w_ab_t = jnp.swapaxes(w_in_ab, 1, 2)

    xp = x_prompt.reshape(rows_p, d)
    xs = jnp.pad(x_sample.reshape(nreq, d), ((0, rs - nreq), (0, 0)))
    up = _modulate(xp, mod_spec_p(0, 0, 1), mod_spec_p(0, 0, 0), mods_p, tm)
    us = _modulate(xs, mod_spec_s(0, 0, 1), mod_spec_s(0, 0, 0), mods_s, rs)

    op = dict(k=[], v=[], c=[], n=[], m=[], h=[], buf=[])
    os_ = dict(k=[], v=[], c=[], n=[], m=[], h=[], buf=[])
    rope = (cos_p, sin_p, cos_s, sin_s)
    k_st = v_st = ()
    pad_rows = lambda t, w: jnp.pad(t.reshape(nreq, w), ((0, rs - nreq), (0, 0))).astype(BF16)
    heads3 = lambda t: t.reshape(rs, -1, A_HEAD_DIM)

    for l in range(depth):
        wb_ffn1 = _cast_bf16(w_ffn1_out, l, tk_cast)
        wb_ffn2 = _cast_bf16(w_ffn2_out, l, tk_cast)
        wb_mix = _cast_bf16(w_out_ab if l % 2 == 0 else w_out_rg, l // 2, tk_cast)

        def post(parts_p, parts_s, wb, s, res_w, nxt):
            nl, ns = nxt
            pick = lambda spec: [spec(l, s, 2), spec(nl, ns, 1), spec(nl, ns, 0)]
            xo, uo, xso, uso = _out_ln(parts_p, parts_s, wb, xp, xs, mods_p, mods_s, pick(mod_spec_p),
                                       pick(mod_spec_s), ln_g3, ln_b3, l * 3 + s, res_w,
                                       tm_out if wb.shape[0] > d else 2 * tm_out)
            return (xo, uo), (xso, uso)

        act_p, act_s = _swiglu_in(up, us, w_ffn1_in, l, tm, tn)
        (xp, up), (xs, us) = post([act_p], [act_s], wb_ffn1, 0, FFN_RES, (l, 1))

        if l % 2 == 0:
            e = l // 2
            q_p, q_s = _proj(up, us, w_ab_t, e, 0, a_w, tm_p, tn_p, F32, rope=rope, w_t=True, name="proj_q")
            tm_kv = tm_p if e == 0 else tm_p // 2
            k_st = _proj(up, us, w_ab_t, e, a_w, a_w, tm_kv, tn_p, F32, rope=rope, w_t=True,
                         stack=k_st, name="proj_k")
            v_st = _proj(up, us, w_ab_t, e, 2 * a_w, a_w, tm_kv, tn_p, F32, w_t=True,
                         stack=v_st, name="proj_v")
            k_s, v_s = k_st[1][e], v_st[1][e]
            zm_p, zm_s = _proj(up, us, w_ab_t, e, 3 * a_w, m_w, tm_p, tn_p, BF16, w_t=True, name="proj_mlstm")
            n_gates = 2 * nh_b
            g_p, g_s = _proj(up, us, w_ab_t, e, 3 * a_w + m_w, n_gates, tm_p, n_gates, F32, w_t=True,
                             name="proj_gates")
            gb = gate_bias[e:e + 1]
            ng = mlstm_norm_g[e:e + 1]

            oa = _moba_prompt(q_p, k_st[0], v_st[0], e, bsz)
            hm, c1, n1, m1 = _mlstm_prompt(zm_p, g_p, gb, ng, bsz)
            mix_p = [oa, hm.reshape(rows_p, v_w)]
            op["c"].append(c1.reshape(bsz, nh_b, c1.shape[1], c1.shape[2]))
            op["n"].append(n1.reshape(bsz, nh_b, -1))
            op["m"].append(m1[:, 0, 0].reshape(bsz, nh_b))

            sel = _moba_sample_gate(q_s, cache_k, e, tbl_flat, nreq, npages)
            oa = _moba_sample_attn(heads3(q_s), heads3(k_s), heads3(v_s), cache_k, cache_v, e,
                                   tbl_flat, sel.reshape(-1), nreq, npages)
            hm, c1, n1, m1 = _mlstm_sample(zm_s, g_s, gb, ng, state_mlstm_c, n0_s, m0_s, e, nreq)
            mix_s = [pad_rows(oa, a_w), pad_rows(hm, v_w)]
            os_["c"].append(c1)
            os_["n"].append(n1.reshape(nreq, nh_b, -1))
            os_["m"].append(m1[:, :, 0, 0])
        else:
            od = l // 2
            zr_p, zr_s = _proj(up, us, w_in_rg, od, 0, w_in_rg.shape[2], tm_p, tn_p, BF16, name="proj_rg")
            y_p, h1, nb = _rglru_prompt(zr_p, od, conv_w, conv_b, w_rg_a, b_rg_a, w_rg_x, b_rg_x,
                                        lru_lambda, bsz, tt_rg)
            op["h"].append(h1.reshape(bsz, -1))
            op["buf"].append(nb)
            y_s, h1, nb = _rglru_sample(zr_s, od, conv_t, state_rglru_h, conv_w, conv_b, w_rg_a,
                                        b_rg_a, w_rg_x, b_rg_x, lru_lambda, nreq)
            os_["h"].append(h1)
            os_["buf"].append(jnp.transpose(nb, (1, 0, 2)))
            mix_p, mix_s = [y_p], [y_s]
        (xp, up), (xs, us) = post(mix_p, mix_s, wb_mix, 1, 1.0, (l, 2))

        act_p, act_s = _swiglu_in(up, us, w_ffn2_in, l, tm, tn)
        (xp, up), (xs, us) = post([act_p], [act_s], wb_ffn2, 2, FFN_RES, (min(l + 1, depth - 1), 0))

    st = jnp.stack
    kv_p = lambda t: t.reshape(n_even, bsz, seq, -1, A_HEAD_DIM)
    kv_s = lambda t: t[:, :nreq].reshape(n_even, nreq, 1, -1, A_HEAD_DIM)
    return (xp.reshape(bsz, seq, d), xs[:nreq].reshape(nreq, 1, d),
            kv_p(k_st[0]), kv_p(v_st[0]), kv_s(k_st[1]), kv_s(v_st[1]),
            st(op["c"]), st(op["n"]), st(op["m"]), st(os_["c"]), st(os_["n"]), st(os_["m"]),
            st(op["h"]), st(op["buf"]), st(os_["h"]), st(os_["buf"]))
```

```python
import functools

import numpy as np
import jax
import jax.numpy as jnp
from jax import lax
from jax.experimental import pallas as pl
from jax.experimental.pallas import tpu as pltpu

F32 = jnp.float32
BF16 = jnp.bfloat16

DEPTH = 4
MOBA_BLOCK = 256
MOBA_TOPK = 3
MOBA_GROUP = 4
MOBA_HEADS_PER_STEP = 4
MLSTM_CHUNKS_PER_STEP = 1
ROPE_THETA = 10000.0
A_HEAD_DIM = 128
B_HEADS = 4
MLSTM_CHUNK = 64
RG_BLOCKS = 16
CONV_WIDTH = 4
RG_C = 8.0
FFN_RES = 0.5
ALPHA = (2.0 * DEPTH) ** 0.25
LN_EPS = 1e-5

LANES = 128
SAMPLE_ROWS = 16
NEG_BIG = -1e30
MIB = 1024 * 1024


def _cparams(n_axes, vmem_mib):
    return pltpu.CompilerParams(
        dimension_semantics=("arbitrary",) * n_axes,
        vmem_limit_bytes=int(vmem_mib * MIB),
    )


def _sigmoid(x):
    return 0.5 * jnp.tanh(0.5 * x) + 0.5


def _softplus(x):
    return jnp.maximum(x, 0.0) + jnp.log1p(jnp.exp(-jnp.abs(x)))


def _gelu_tanh(x):
    return 0.5 * x * (1.0 + jnp.tanh(0.7978845608028654 * (x + 0.044715 * x * x * x)))


def _mods_kernel(c_ref, w_ref, b_ref, o_ref):
    c = c_ref[...]
    s = (c * _sigmoid(c)).astype(BF16)
    o_ref[...] = jnp.dot(s, w_ref[...].astype(BF16), preferred_element_type=F32) + b_ref[...]


def _mods(c_all, w_ada, b_ada):
    depth, k, n = w_ada.shape
    rows = c_all.shape[0]
    tn = 1024 if n % 1024 == 0 else n
    return pl.pallas_call(
        _mods_kernel,
        grid=(depth, n // tn),
        in_specs=[
            pl.BlockSpec((rows, k), lambda l, j: (0, 0)),
            pl.BlockSpec((None, k, tn), lambda l, j: (l, 0, j)),
            pl.BlockSpec((None, 1, tn), lambda l, j: (l, 0, j)),
        ],
        out_specs=pl.BlockSpec((None, rows, tn), lambda l, j: (l, 0, j)),
        out_shape=jax.ShapeDtypeStruct((depth, rows, n), F32),
        compiler_params=_cparams(2, 48),
        name="adaln_mods",
    )(c_all, w_ada, b_ada.reshape(depth, 1, n))


def _modulate_kernel(x_ref, sc_ref, sh_ref, u_ref):
    u_ref[...] = (x_ref[...] * (1.0 + sc_ref[...]) + sh_ref[...]).astype(u_ref.dtype)


def _modulate(x, sc_spec, sh_spec, mods, tm):
    m, d = x.shape
    return pl.pallas_call(
        _modulate_kernel,
        grid=(m // tm,),
        in_specs=[pl.BlockSpec((tm, d), lambda i: (i, 0)), sc_spec(tm), sh_spec(tm)],
        out_specs=pl.BlockSpec((tm, d), lambda i: (i, 0)),
        out_shape=jax.ShapeDtypeStruct((m, d), BF16),
        compiler_params=_cparams(1, 32),
        name="modulate0",
    )(x, mods, mods)


def _rope_store(z, cos, sin, o_ref):
    for j in range(z.shape[1] // A_HEAD_DIM):
        zj = z[:, j * A_HEAD_DIM:(j + 1) * A_HEAD_DIM]
        o_ref[:, j * A_HEAD_DIM:(j + 1) * A_HEAD_DIM] = (
            zj * cos + pltpu.roll(zj, A_HEAD_DIM // 2, 1) * sin).astype(o_ref.dtype)


def _proj_kernel(*refs, rope, n_prev, w_mode):
    if n_prev:
        prev_ref, prevs_ref = refs[len(refs) - 5:len(refs) - 3]
        refs = refs[:len(refs) - 5] + refs[len(refs) - 3:]
    if rope:
        x_ref, xs_ref, w_ref, cos_ref, sin_ref, coss_ref, sins_ref, o_ref, os_ref, wb_ref = refs
    else:
        x_ref, xs_ref, w_ref, o_ref, os_ref, wb_ref = refs
    if n_prev:
        o_ref[0:n_prev] = prev_ref[...]
        o_ref = o_ref.at[n_prev]

        @pl.when(pl.program_id(1) == 0)
        def _():
            os_ref[0:n_prev] = prevs_ref[...]

        os_ref = os_ref.at[n_prev]

    def mm(x):
        if w_mode == "nk_contract_last":
            return lax.dot_general(x, wb_ref[...], (((1,), (1,)), ((), ())), preferred_element_type=F32)
        return jnp.dot(x, wb_ref[...], preferred_element_type=F32)

    @pl.when(pl.program_id(1) == 0)
    def _():
        w = w_ref[...]
        if w_mode == "nk_transpose":
            w = w.T
        wb_ref[...] = w.astype(BF16)
        zs = mm(xs_ref[...])
        if rope:
            _rope_store(zs, coss_ref[...], sins_ref[...], os_ref)
        else:
            os_ref[...] = zs.astype(os_ref.dtype)

    z = mm(x_ref[...])
    if rope:
        _rope_store(z, cos_ref[...], sin_ref[...], o_ref)
    else:
        o_ref[...] = z.astype(o_ref.dtype)


def _swiglu_in_kernel(x_ref, xs_ref, wg_ref, wv_ref, o_ref, os_ref, wgb_ref, wvb_ref):
    def act(x):
        g = jnp.dot(x, wgb_ref[...], preferred_element_type=F32)
        v = jnp.dot(x, wvb_ref[...], preferred_element_type=F32)
        return g * _sigmoid(g) * v

    @pl.when(pl.program_id(1) == 0)
    def _():
        wgb_ref[...] = wg_ref[...].astype(BF16)
        wvb_ref[...] = wv_ref[...].astype(BF16)
        os_ref[...] = act(xs_ref[...]).astype(os_ref.dtype)

    o_ref[...] = act(x_ref[...]).astype(o_ref.dtype)


def _proj(x, xs, w, layer, col0, ncols, tm, tn, out_dtype, rope=None, stack=None, w_t=False,
          name="proj"):
    m, k = x.shape
    rows_s = xs.shape[0]
    c0 = col0 // tn
    if w_t:
        w_spec = pl.BlockSpec((None, tn, k), lambda j, i: (layer, c0 + j, 0))
        w_mode = "nk_transpose" if tn % LANES == 0 else "nk_contract_last"
    else:
        w_spec = pl.BlockSpec((None, k, tn), lambda j, i: (layer, 0, c0 + j))
        w_mode = "kn"
    wb_shape = (tn, k) if w_mode == "nk_contract_last" else (k, tn)
    in_specs = [
        pl.BlockSpec((tm, k), lambda j, i: (i, 0)),
        pl.BlockSpec((rows_s, k), lambda j, i: (0, 0)),
        w_spec,
    ]
    args = [x, xs, w]
    if rope is not None:
        period = rope[0].shape[0] // tm
        in_specs += [pl.BlockSpec((tm, A_HEAD_DIM), lambda j, i: (i % period, 0))] * 2
        in_specs += [pl.BlockSpec((rows_s, A_HEAD_DIM), lambda j, i: (0, 0))] * 2
        args += list(rope)
    n_prev = 0
    if stack is None:
        out_specs = [pl.BlockSpec((tm, tn), lambda j, i: (i, j)),
                     pl.BlockSpec((rows_s, tn), lambda j, i: (0, j))]
        out_shape = [jax.ShapeDtypeStruct((m, ncols), out_dtype),
                     jax.ShapeDtypeStruct((rows_s, ncols), out_dtype)]
    elif len(stack) == 0:
        out_specs = [pl.BlockSpec((None, tm, tn), lambda j, i: (0, i, j)),
                     pl.BlockSpec((None, rows_s, tn), lambda j, i: (0, 0, j))]
        out_shape = [jax.ShapeDtypeStruct((1, m, ncols), out_dtype),
                     jax.ShapeDtypeStruct((1, rows_s, ncols), out_dtype)]
    else:
        n_prev = stack[0].shape[0]
        in_specs += [pl.BlockSpec((n_prev, tm, tn), lambda j, i: (0, i, j)),
                     pl.BlockSpec((n_prev, rows_s, tn), lambda j, i: (0, 0, j))]
        args += list(stack)
        out_specs = [pl.BlockSpec((n_prev + 1, tm, tn), lambda j, i: (0, i, j)),
                     pl.BlockSpec((n_prev + 1, rows_s, tn), lambda j, i: (0, 0, j))]
        out_shape = [jax.ShapeDtypeStruct((n_prev + 1, m, ncols), out_dtype),
                     jax.ShapeDtypeStruct((n_prev + 1, rows_s, ncols), out_dtype)]
    return pl.pallas_call(
        functools.partial(_proj_kernel, rope=rope is not None, n_prev=n_prev, w_mode=w_mode),
        grid=(ncols // tn, m // tm),
        in_specs=in_specs,
        out_specs=out_specs,
        out_shape=out_shape,
        scratch_shapes=[pltpu.VMEM(wb_shape, BF16)],
        compiler_params=_cparams(2, 56),
        name=name,
    )(*args)


def _swiglu_in(x, xs, w, layer, tm, tn):
    m, k = x.shape
    rows_s = xs.shape[0]
    dff = w.shape[2] // 2
    nv = dff // tn
    return pl.pallas_call(
        _swiglu_in_kernel,
        grid=(dff // tn, m // tm),
        in_specs=[
            pl.BlockSpec((tm, k), lambda j, i: (i, 0)),
            pl.BlockSpec((rows_s, k), lambda j, i: (0, 0)),
            pl.BlockSpec((None, k, tn), lambda j, i: (layer, 0, j)),
            pl.BlockSpec((None, k, tn), lambda j, i: (layer, 0, nv + j)),
        ],
        out_specs=[pl.BlockSpec((tm, tn), lambda j, i: (i, j)),
                   pl.BlockSpec((rows_s, tn), lambda j, i: (0, j))],
        out_shape=[jax.ShapeDtypeStruct((m, dff), BF16), jax.ShapeDtypeStruct((rows_s, dff), BF16)],
        scratch_shapes=[pltpu.VMEM((k, tn), BF16), pltpu.VMEM((k, tn), BF16)],
        compiler_params=_cparams(2, 56),
        name="swiglu_in",
    )(x, xs, w, w)


def _cast_kernel(w_ref, o_ref):
    o_ref[...] = w_ref[...].astype(o_ref.dtype)


def _cast_bf16(w, layer, tk):
    _, k, n = w.shape
    return pl.pallas_call(
        _cast_kernel,
        grid=(k // tk,),
        in_specs=[pl.BlockSpec((None, tk, n), lambda i: (layer, i, 0))],
        out_specs=pl.BlockSpec((tk, n), lambda i: (i, 0)),
        out_shape=jax.ShapeDtypeStruct((k, n), BF16),
        compiler_params=_cparams(1, 32),
        name="cast_bf16",
    )(w)


def _out_ln_kernel(*refs, n_parts, res_w):
    ap_refs = refs[:n_parts]
    as_refs = refs[n_parts:2 * n_parts]
    (w_ref, x_ref, gate_ref, sc_ref, sh_ref, xs_ref, gates_ref, scs_ref, shs_ref, lng_ref, lnb_ref,
     xo_ref, uo_ref, xso_ref, uso_ref) = refs[2 * n_parts:]

    def run(a_refs, x_ref, gate_ref, sc_ref, sh_ref, xo_ref, uo_ref):
        acc = None
        off = 0
        for a_ref in a_refs:
            kk = a_ref.shape[1]
            part = jnp.dot(a_ref[...], w_ref[off:off + kk, :], preferred_element_type=F32)
            acc = part if acc is None else acc + part
            off += kk
        y = ALPHA * x_ref[...] + (res_w * (1.0 + gate_ref[...])) * acc
        mu = jnp.mean(y, axis=-1, keepdims=True)
        yc = y - mu
        var = jnp.mean(yc * yc, axis=-1, keepdims=True)
        xh = yc * lax.rsqrt(var + LN_EPS)
        g = lng_ref[...]
        b = lnb_ref[...]
        mod = 1.0 + sc_ref[...]
        xo_ref[...] = xh * g + b
        uo_ref[...] = (xh * (g * mod) + (b * mod + sh_ref[...])).astype(uo_ref.dtype)

    @pl.when(pl.program_id(0) == 0)
    def _():
        run(as_refs, xs_ref, gates_ref, scs_ref, shs_ref, xso_ref, uso_ref)

    run(ap_refs, x_ref, gate_ref, sc_ref, sh_ref, xo_ref, uo_ref)


def _out_ln(parts, parts_s, wb, x, xs, mods, mods_s, specs, specs_s, ln_g, ln_b, ln_row, res_w, tm):
    m = parts[0].shape[0]
    rows_s = xs.shape[0]
    k, d = wb.shape
    ln_spec = pl.BlockSpec((None, 1, d), lambda i: (ln_row, 0, 0))
    row = lambda width: pl.BlockSpec((tm, width), lambda i: (i, 0))
    whole = lambda width: pl.BlockSpec((rows_s, width), lambda i: (0, 0))
    return pl.pallas_call(
        functools.partial(_out_ln_kernel, n_parts=len(parts), res_w=res_w),
        grid=(m // tm,),
        in_specs=[row(p.shape[1]) for p in parts] + [whole(p.shape[1]) for p in parts_s] + [
            pl.BlockSpec((k, d), lambda i: (0, 0), pipeline_mode=pl.Buffered(1)),
            row(d)] + [s(tm) for s in specs] + [whole(d)] + [s(rows_s) for s in specs_s] + [
            ln_spec, ln_spec],
        out_specs=[row(d), row(d), whole(d), whole(d)],
        out_shape=[jax.ShapeDtypeStruct((m, d), F32), jax.ShapeDtypeStruct((m, d), BF16),
                   jax.ShapeDtypeStruct((rows_s, d), F32), jax.ShapeDtypeStruct((rows_s, d), BF16)],
        compiler_params=_cparams(1, 56),
        name="out_postnorm",
    )(*parts, *parts_s, wb, x, mods, mods, mods, xs, mods_s, mods_s, mods_s, ln_g, ln_b)


def _moba_prompt_kernel(q_ref, k_ref, v_ref, o_ref, kb_ref, vb_ref, kmean_ref,
                        m_ref, acc_ref, *, nblk, scale):
    qi = pl.program_id(2)
    blk = MOBA_BLOCK
    hd = A_HEAD_DIM
    nheads = q_ref.shape[1] // hd
    lanes = lambda hh: slice(hh * hd, (hh + 1) * hd)

    @pl.when(qi == 0)
    def _():
        seq = k_ref.shape[0]
        r = lax.broadcasted_iota(jnp.int32, (seq, hd), 0)
        c = lax.broadcasted_iota(jnp.int32, (seq, hd), 1)
        in_blk = (r >= c * blk) & (r < c * blk + blk)
        onehot = jnp.where(in_blk, 1.0, 0.0).astype(BF16)
        kmean_ref[...] = jnp.zeros_like(kmean_ref)
        for hh in range(nheads):
            kf = k_ref[:, lanes(hh)]
            kb_ref[hh, :, 0:hd] = kf.astype(BF16)
            kb_ref[hh, :, hd:2 * hd] = onehot
            vb_ref[hh, :, 0:hd] = v_ref[:, lanes(hh)].astype(BF16)
            vb_ref[hh, :, hd:2 * hd] = jnp.ones((seq, hd), BF16)
            kmean_ref[hh, 0:nblk, :] = jnp.mean(kf.reshape(nblk, blk, hd), axis=1)

    nt = (((1,), (1,)), ((), ()))
    start = pl.multiple_of(qi * blk, blk)
    q_augs = []
    for hh in range(nheads):
        q = q_ref[:, lanes(hh)]
        gate = lax.dot_general(kmean_ref[hh, 0:nblk, :], q, nt, precision=lax.Precision.HIGHEST,
                               preferred_element_type=F32)
        bi = lax.broadcasted_iota(jnp.int32, gate.shape, 0)
        bf = bi.astype(F32)
        g = jnp.where(bi < qi, gate, -jnp.inf)
        sel = jnp.zeros(gate.shape, F32)
        for _ in range(MOBA_TOPK):
            mx = jnp.max(g, axis=0, keepdims=True)
            cand = jnp.where(g == mx, bf, 1e9)
            cand = jnp.where(mx > -jnp.inf, cand, 1e9)
            pick = bf == jnp.min(cand, axis=0, keepdims=True)
            sel = jnp.where(pick, 1.0, sel)
            g = jnp.where(pick, -jnp.inf, g)
        qb = q.astype(BF16)
        neg = jnp.concatenate([(1.0 - sel) * NEG_BIG, jnp.zeros((hd - nblk, blk), F32)], axis=0).T
        q_augs.append(jnp.concatenate([qb, neg.astype(BF16)], axis=1))
        s = lax.dot_general(qb, kb_ref[hh, pl.ds(start, blk), 0:hd], nt, preferred_element_type=F32) * scale
        row = lax.broadcasted_iota(jnp.int32, s.shape, 0)
        col = lax.broadcasted_iota(jnp.int32, s.shape, 1)
        s = jnp.where(col <= row, s, NEG_BIG)
        m0 = jnp.max(s, axis=1, keepdims=True)
        p = jnp.exp(s - m0)
        m_ref[hh] = jnp.broadcast_to(m0, m_ref.shape[1:])
        acc_ref[hh] = jnp.dot(p.astype(BF16), vb_ref[hh, pl.ds(start, blk), :], preferred_element_type=F32)

    grp = MOBA_GROUP
    span = grp * blk

    def past_group(gi, carry):
        st = pl.multiple_of(gi * span, span)
        for hh in range(nheads):
            sn = lax.dot_general(q_augs[hh], kb_ref[hh, pl.ds(st, span), :], nt,
                                 preferred_element_type=F32) * scale
            m_prev = m_ref[hh]
            m_new = jnp.maximum(m_prev, jnp.max(sn, axis=1, keepdims=True))
            a = jnp.exp(m_prev - m_new)
            pn = jnp.exp(sn - jnp.concatenate([m_new] * (span // hd), axis=1))
            acc_ref[hh] = jnp.concatenate([a, a], axis=1) * acc_ref[hh] + jnp.dot(
                pn.astype(BF16), vb_ref[hh, pl.ds(st, span), :], preferred_element_type=F32)
            m_ref[hh] = m_new
        return carry

    lax.fori_loop(0, (qi + grp - 1) // grp, past_group, 0)
    for hh in range(nheads):
        acc = acc_ref[hh]
        o_ref[:, lanes(hh)] = (acc[:, 0:hd] / acc[:, hd:2 * hd]).astype(o_ref.dtype)


def _moba_prompt(q, k, v, slab, bsz):
    rows, width = q.shape
    seq = rows // bsz
    nh = width // A_HEAD_DIM
    nblk = seq // MOBA_BLOCK
    assert nblk % MOBA_GROUP == 0
    blk = MOBA_BLOCK
    hd = A_HEAD_DIM
    hps = MOBA_HEADS_PER_STEP
    assert nh % hps == 0
    return pl.pallas_call(
        functools.partial(_moba_prompt_kernel, nblk=nblk, scale=hd ** -0.5),
        grid=(bsz, nh // hps, nblk),
        in_specs=[
            pl.BlockSpec((blk, hps * hd), lambda b, h, i: (b * nblk + i, h)),
            pl.BlockSpec((None, seq, hps * hd), lambda b, h, i: (slab, b, h), pipeline_mode=pl.Buffered(1)),
            pl.BlockSpec((None, seq, hps * hd), lambda b, h, i: (slab, b, h), pipeline_mode=pl.Buffered(1)),
        ],
        out_specs=pl.BlockSpec((blk, hps * hd), lambda b, h, i: (b * nblk + i, h)),
        out_shape=jax.ShapeDtypeStruct((rows, width), BF16),
        scratch_shapes=[
            pltpu.VMEM((hps, seq, 2 * hd), BF16), pltpu.VMEM((hps, seq, 2 * hd), BF16),
            pltpu.VMEM((hps, LANES, hd), F32),
            pltpu.VMEM((hps, blk, hd), F32), pltpu.VMEM((hps, blk, 2 * hd), F32),
        ],
        compiler_params=_cparams(3, 52),
        name="moba_prompt",
    )(q, k, v)


def _mlstm_prompt_kernel(tbl_ref, q_ref, k_ref, v_ref, og_ref, g_ref, gb_ref, ng_ref, *refs,
                         nb, nh, dk, dv, pages_per_step, pages_per_blk):
    page_refs = refs[:pages_per_step]
    h_ref, c_ref, n_ref, m_ref, ksum_ref = refs[pages_per_step:]
    lc = MLSTM_CHUNK
    chunks = q_ref.shape[1] // lc

    @pl.when(pl.program_id(0) == 0)
    def _():
        c_ref[...] = jnp.zeros_like(c_ref)
        n_ref[...] = jnp.zeros_like(n_ref)
        m_ref[...] = jnp.zeros_like(m_ref)

    blks_per_step = pages_per_step // pages_per_blk
    for i in range(blks_per_step):
        tot = jnp.sum(page_refs[i * pages_per_blk][...], axis=0)
        for u in range(1, pages_per_blk):
            tot = tot + jnp.sum(page_refs[i * pages_per_blk + u][...], axis=0)
        ksum_ref[pl.program_id(0) * blks_per_step + i] = tot

    tt = lax.broadcasted_iota(jnp.int32, (lc, lc), 0)
    ss = lax.broadcasted_iota(jnp.int32, (lc, lc), 1)
    causal = ss <= tt
    eye = ss == tt
    nt = (((1,), (1,)), ((), ()))
    tn = (((0,), (0,)), ((), ()))
    gb = gb_ref[...]
    for cc in range(chunks):
        rows = slice(cc * lc, (cc + 1) * lc)
        for b in range(nb):
            gates = g_ref[b, rows, :] + gb
            log_f = jnp.minimum(gates, 0.0) - jnp.log1p(jnp.exp(-jnp.abs(gates)))
            for h in range(nh):
                bh = b * nh + h
                i_col = gates[:, h:h + 1]
                f_col = log_f[:, nh + h:nh + h + 1]
                f_row = jnp.sum(jnp.where(eye, f_col, 0.0), axis=0, keepdims=True)
                i_row = jnp.sum(jnp.where(eye, i_col, 0.0), axis=0, keepdims=True)
                b_col = jnp.sum(jnp.where(causal, f_row, 0.0), axis=1, keepdims=True)
                b_row = jnp.sum(jnp.where(ss >= tt, f_col, 0.0), axis=0, keepdims=True)
                m_prev = m_ref[bh][:, 0:1]
                d = jnp.where(causal, b_col - b_row + i_row, NEG_BIG)
                inter = b_col + m_prev
                m_t = jnp.maximum(jnp.max(d, axis=1, keepdims=True), inter)
                w = jnp.exp(d - m_t)
                q = q_ref[b, rows, h * dk:(h + 1) * dk]
                k = k_ref[b, rows, h * dk:(h + 1) * dk] * (dk ** -0.5)
                vb = v_ref[b, rows, h * dv:(h + 1) * dv].astype(BF16)
                qb = q.astype(BF16)
                s = lax.dot_general(qb, k.astype(BF16), nt, preferred_element_type=F32) * w
                wi = jnp.exp(inter - m_t)
                cst = c_ref[bh]
                nrow = n_ref[bh]
                num = wi * jnp.dot(qb, cst.astype(BF16), preferred_element_type=F32) + jnp.dot(
                    s.astype(BF16), vb, preferred_element_type=F32)
                den = wi * jnp.sum(q * nrow, axis=1, keepdims=True) + jnp.sum(s, axis=1, keepdims=True)
                hh = num / jnp.maximum(jnp.abs(den), jnp.exp(-m_t))
                b_last = b_col[lc - 1:lc, :]
                g_col = b_last - b_col + i_col
                m_new = jnp.maximum(b_last + m_prev, jnp.max(g_col, axis=0, keepdims=True))
                wc = jnp.exp(b_last + m_prev - m_new)
                kw = k * jnp.exp(g_col - m_new)
                c_ref[bh] = wc * cst + lax.dot_general(kw.astype(BF16), vb, tn, preferred_element_type=F32)
                n_ref[bh] = wc * nrow + jnp.sum(kw, axis=0, keepdims=True)
                m_ref[bh] = jnp.broadcast_to(m_new, (1, LANES))
                hn = hh * lax.rsqrt(jnp.mean(hh * hh, axis=1, keepdims=True) + LN_EPS)
                hn = hn * ng_ref[:, h * dv:(h + 1) * dv] * _sigmoid(og_ref[b, rows, h * dv:(h + 1) * dv])
                h_ref[b, rows, h * dv:(h + 1) * dv] = hn.astype(h_ref.dtype)


def _mlstm_prompt(zm, gates, gate_bias, norm_g, bsz, cache_k, layer, tbl_flat):
    rows, _ = zm.shape
    seq = rows // bsz
    nh = B_HEADS
    v_w = norm_g.shape[1]
    qk_w = v_w // 2
    dk, dv = qk_w // nh, v_w // nh
    lc = MLSTM_CHUNK * MLSTM_CHUNKS_PER_STEP
    assert seq % lc == 0
    nsteps = seq // lc
    page, kv_heads, hd = cache_k.shape[2:]
    ppb = MOBA_BLOCK // page
    total_pages = tbl_flat.shape[0]
    assert total_pages % nsteps == 0 and (total_pages // nsteps) % ppb == 0
    pps = total_pages // nsteps
    z3 = zm.reshape(bsz, seq, zm.shape[1])
    gw = gates.shape[1]
    g3 = gates.reshape(bsz, seq, gw)
    nbh = bsz * nh
    full = lambda shape: pl.BlockSpec(shape, lambda c, tbl: (0,) * len(shape))

    def page_spec(u):
        return pl.BlockSpec((None, None, page, kv_heads, hd),
                            lambda c, tbl: (layer, tbl[c * pps + u], 0, 0, 0))

    grid_spec = pltpu.PrefetchScalarGridSpec(
        num_scalar_prefetch=1,
        grid=(nsteps,),
        in_specs=[
            pl.BlockSpec((bsz, lc, qk_w), lambda c, tbl: (0, c, 0)),
            pl.BlockSpec((bsz, lc, qk_w), lambda c, tbl: (0, c, 1)),
            pl.BlockSpec((bsz, lc, v_w), lambda c, tbl: (0, c, 1)),
            pl.BlockSpec((bsz, lc, v_w), lambda c, tbl: (0, c, 2)),
            pl.BlockSpec((bsz, lc, gw), lambda c, tbl: (0, c, 0)),
            full((1, gw)), full((1, v_w)),
        ] + [page_spec(u) for u in range(pps)],
        out_specs=[
            pl.BlockSpec((bsz, lc, v_w), lambda c, tbl: (0, c, 0)),
            full((nbh, dk, dv)), full((nbh, 1, dk)), full((nbh, 1, LANES)),
            full((total_pages // ppb, kv_heads, hd)),
        ],
    )
    return pl.pallas_call(
        functools.partial(_mlstm_prompt_kernel, nb=bsz, nh=nh, dk=dk, dv=dv,
                          pages_per_step=pps, pages_per_blk=ppb),
        grid_spec=grid_spec,
        out_shape=[
            jax.ShapeDtypeStruct((bsz, seq, v_w), BF16),
            jax.ShapeDtypeStruct((nbh, dk, dv), F32),
            jax.ShapeDtypeStruct((nbh, 1, dk), F32),
            jax.ShapeDtypeStruct((nbh, 1, LANES), F32),
            jax.ShapeDtypeStruct((total_pages // ppb, kv_heads, hd), F32),
        ],
        compiler_params=_cparams(1, 48),
        name="mlstm_prompt",
    )(tbl_flat, z3, z3, z3, z3, g3, gate_bias, norm_g, *([cache_k] * pps))


def _rglru_gates(xconv, wa_ref, ba, wx_ref, bx, lam):
    bd = wa_ref.shape[1]
    r_parts, i_parts = [], []
    for n in range(wa_ref.shape[0]):
        xb = xconv[:, n * bd:(n + 1) * bd].astype(BF16)
        r_parts.append(jnp.dot(xb, wa_ref[n].astype(BF16), preferred_element_type=F32))
        i_parts.append(jnp.dot(xb, wx_ref[n].astype(BF16), preferred_element_type=F32))
    r = _sigmoid(jnp.concatenate(r_parts, axis=1) + ba)
    ig = _sigmoid(jnp.concatenate(i_parts, axis=1) + bx)
    log_a = (-RG_C) * r * _softplus(-lam)
    a = jnp.exp(log_a)
    mult = jnp.sqrt(1.0 - a * a)
    return a, mult * ig * xconv


def _rglru_prompt_kernel(gate_ref, xr_ref, cw_ref, cb_ref, wa_ref, ba_ref, wx_ref, bx_ref, lam_ref,
                         y_ref, hl_ref, buf_ref, xext_ref, a_ref, b_ref, hs_ref, hc_ref):
    t = pl.program_id(1)
    tt = xr_ref.shape[0]
    pad = 8
    nbuf = CONV_WIDTH - 1

    @pl.when(t == 0)
    def _():
        xext_ref[0:pad, :] = jnp.zeros((pad, xext_ref.shape[1]), F32)
        hc_ref[...] = jnp.zeros_like(hc_ref)

    @pl.when(t > 0)
    def _():
        xext_ref[0:pad, :] = xext_ref[tt:tt + pad, :]

    xr = xr_ref[...]
    xext_ref[pad:pad + tt, :] = xr
    xconv = cb_ref[...] + xext_ref[pad - nbuf:pad - nbuf + tt, :] * cw_ref[0:1, :]
    for j in range(1, CONV_WIDTH):
        xconv = xconv + xext_ref[pad - nbuf + j:pad - nbuf + j + tt, :] * cw_ref[j:j + 1, :]
    a, bterm = _rglru_gates(xconv, wa_ref, ba_ref[...], wx_ref, bx_ref[...], lam_ref[...])
    a_ref[...] = a
    b_ref[...] = bterm

    def step(i, h):
        h = a_ref[pl.ds(i, 1), :] * h + b_ref[pl.ds(i, 1), :]
        hs_ref[pl.ds(i, 1), :] = h
        return h

    h_last = lax.fori_loop(0, tt, step, hc_ref[...], unroll=8)
    hc_ref[...] = h_last
    y_ref[...] = (_gelu_tanh(gate_ref[...]) * hs_ref[...]).astype(y_ref.dtype)
    hl_ref[...] = h_last
    buf_ref[...] = xr[tt - nbuf:tt, :]


def _rglru_prompt(zr, layer, conv_w, conv_b, w_a, b_a, w_x, b_x, lam, bsz, tt):
    rows, w2 = zr.shape
    width = w2 // 2
    seq = rows // bsz
    nt = seq // tt
    nblk, bd = w_a.shape[1], w_a.shape[2]
    vec = lambda a: a.reshape(a.shape[0], 1, width)
    vspec = pl.BlockSpec((None, 1, width), lambda b, t: (layer, 0, 0))
    wspec = pl.BlockSpec((None, nblk, bd, bd), lambda b, t: (layer, 0, 0, 0))
    nbuf = CONV_WIDTH - 1
    return pl.pallas_call(
        _rglru_prompt_kernel,
        grid=(bsz, nt),
        in_specs=[
            pl.BlockSpec((tt, width), lambda b, t: (b * nt + t, 0)),
            pl.BlockSpec((tt, width), lambda b, t: (b * nt + t, 1)),
            pl.BlockSpec((None, CONV_WIDTH, width), lambda b, t: (layer, 0, 0)),
            vspec, wspec, vspec, wspec, vspec, vspec,
        ],
        out_specs=[
            pl.BlockSpec((tt, width), lambda b, t: (b * nt + t, 0)),
            pl.BlockSpec((None, 1, width), lambda b, t: (b, 0, 0)),
            pl.BlockSpec((None, nbuf, width), lambda b, t: (b, 0, 0)),
        ],
        out_shape=[
            jax.ShapeDtypeStruct((rows, width), BF16),
            jax.ShapeDtypeStruct((bsz, 1, width), F32),
            jax.ShapeDtypeStruct((bsz, nbuf, width), F32),
        ],
        scratch_shapes=[
            pltpu.VMEM((tt + 8, width), F32), pltpu.VMEM((tt, width), F32),
            pltpu.VMEM((tt, width), F32), pltpu.VMEM((tt, width), F32), pltpu.VMEM((1, width), F32),
        ],
        compiler_params=_cparams(2, 48),
        name="rglru_prompt",
    )(zr, zr, conv_w, vec(conv_b), w_a, vec(b_a), w_x, vec(b_x), vec(lam))


def _moba_gate_kernel(q_ref, ksum_ref, sel_ref, *, nheads, keys_per_blk):
    b = pl.program_id(0)
    hd = A_HEAD_DIM
    nblk = ksum_ref.shape[0]
    inv = 1.0 / keys_per_blk
    rowi = lax.broadcasted_iota(jnp.int32, (nblk, 1), 0).astype(F32)
    out_r = lax.broadcasted_iota(jnp.int32, sel_ref.shape, 0)
    out_c = lax.broadcasted_iota(jnp.int32, sel_ref.shape, 1)
    out = jnp.zeros(sel_ref.shape, jnp.int32)
    qrow = q_ref[pl.ds(b, 1), :]
    for h in range(nheads):
        km = ksum_ref[:, h, :] * inv
        g = jnp.sum(km * qrow[:, h * hd:(h + 1) * hd], axis=1, keepdims=True)
        for i in range(MOBA_TOPK):
            mx = jnp.max(g, axis=0, keepdims=True)
            idx = jnp.min(jnp.where(g == mx, rowi, 1e9), axis=0, keepdims=True)
            out = jnp.where((out_r == h) & (out_c == i), idx.astype(jnp.int32), out)
            g = jnp.where(rowi == idx, -jnp.inf, g)
    sel_ref[...] = out


def _moba_sample_gate(q, ksum, nreq):
    nheads, hd = ksum.shape[1:]
    nblk = ksum.shape[0] // nreq
    return pl.pallas_call(
        functools.partial(_moba_gate_kernel, nheads=nheads, keys_per_blk=MOBA_BLOCK),
        grid=(nreq,),
        in_specs=[pl.BlockSpec(q.shape, lambda b: (0, 0)),
                  pl.BlockSpec((nblk, nheads, hd), lambda b: (b, 0, 0))],
        out_specs=pl.BlockSpec((None, nheads, LANES), lambda b: (b, 0, 0)),
        out_shape=jax.ShapeDtypeStruct((nreq, nheads, LANES), jnp.int32),
        compiler_params=_cparams(1, 32),
        name="moba_sample_gate",
    )(q, ksum)


def _moba_sample_attn_kernel(tbl_ref, sel_ref, q_ref, kn_ref, vn_ref, *refs, npg, hps, scale):
    o_ref = refs[2 * npg * hps]
    b = pl.program_id(0)
    nheads = q_ref.shape[1]
    hd = q_ref.shape[2]
    for hh in range(hps):
        k_refs = refs[hh * npg:(hh + 1) * npg]
        v_refs = refs[(hps + hh) * npg:(hps + hh + 1) * npg]
        h = pl.program_id(1) * hps + hh
        is_h = lax.broadcasted_iota(jnp.int32, (1, nheads, 1), 1) == h
        q_m = jnp.where(is_h, q_ref[b][None], 0.0)

        def score(kp):
            part = jnp.sum(kp * q_m, axis=2, keepdims=True)
            return jnp.sum(part, axis=1, keepdims=True) * scale

        s_own = score(kn_ref[b][None])
        ss = [score(k_ref[...]) for k_ref in k_refs]
        m = s_own
        for s in ss:
            m = jnp.maximum(m, jnp.max(s, axis=0, keepdims=True))
        l = jnp.exp(s_own - m)
        acc = l * vn_ref[b][None]
        for s, v_ref in zip(ss, v_refs):
            p = jnp.exp(s - m)
            l = l + jnp.sum(p, axis=0, keepdims=True)
            acc = acc + jnp.sum(p * v_ref[...], axis=0, keepdims=True)
        out = jnp.sum(jnp.where(is_h, acc / l, 0.0), axis=1)
        o_ref[:, hh * hd:(hh + 1) * hd] = out.astype(o_ref.dtype)


SAMPLE_ATTN_HEADS_PER_STEP = 2


def _moba_sample_attn(q, k_new, v_new, cache_k, cache_v, layer, tbl_flat, sel_flat, nreq, npages):
    page, nheads, hd = cache_k.shape[2:]
    ppb = MOBA_BLOCK // page
    npg = MOBA_TOPK * ppb
    hps = SAMPLE_ATTN_HEADS_PER_STEP
    assert nheads % hps == 0

    def page_spec(hh, j):
        def page_map(b, g, tbl, sel):
            blk = sel[(b * nheads + g * hps + hh) * LANES + j // ppb]
            return (layer, tbl[b * npages + blk * ppb + j % ppb], 0, 0, 0)
        return pl.BlockSpec((None, None, page, nheads, hd), page_map)

    head = pl.BlockSpec(q.shape, lambda b, g, tbl, sel: (0, 0, 0))
    pages = [page_spec(hh, j) for hh in range(hps) for j in range(npg)]
    grid_spec = pltpu.PrefetchScalarGridSpec(
        num_scalar_prefetch=2,
        grid=(nreq, nheads // hps),
        in_specs=[head, head, head] + pages + pages,
        out_specs=pl.BlockSpec((None, 1, hps * hd), lambda b, g, tbl, sel: (b, 0, g)),
    )
    n_in = npg * hps
    return pl.pallas_call(
        functools.partial(_moba_sample_attn_kernel, npg=npg, hps=hps, scale=hd ** -0.5),
        grid_spec=grid_spec,
        out_shape=jax.ShapeDtypeStruct((nreq, 1, nheads * hd), F32),
        compiler_params=_cparams(2, 48),
        name="moba_sample_attn",
    )(tbl_flat, sel_flat, q, k_new, v_new, *([cache_k] * n_in), *([cache_v] * n_in))


def _mlstm_sample_kernel(q_ref, k_ref, v_ref, og_ref, g_ref, gb_ref, ng_ref, c0_ref, n0_ref, m0_ref,
                         h_ref, c_ref, n_ref, m_ref, *, nh, dk, dv):
    b = pl.program_id(0)
    rr = lax.broadcasted_iota(jnp.int32, (dk, dk), 0)
    cc = lax.broadcasted_iota(jnp.int32, (dk, dk), 1)
    eye = rr == cc
    gates = g_ref[pl.ds(b, 1), :] + gb_ref[...]
    q_all = q_ref[pl.ds(b, 1), :]
    k_all = k_ref[pl.ds(b, 1), :]
    v_all = v_ref[pl.ds(b, 1), :]
    og_all = og_ref[pl.ds(b, 1), :]
    for h in range(nh):
        ii = gates[:, h:h + 1]
        fpre = gates[:, nh + h:nh + h + 1]
        ff = jnp.minimum(fpre, 0.0) - jnp.log1p(jnp.exp(-jnp.abs(fpre)))
        m_prev = m0_ref[h][:, 0:1]
        q = q_all[:, h * dk:(h + 1) * dk]
        k = k_all[:, h * dk:(h + 1) * dk] * (dk ** -0.5)
        v = v_all[:, h * dv:(h + 1) * dv]
        inter = ff + m_prev
        m_t = jnp.maximum(ii, inter)
        s = jnp.sum(q * k, axis=1, keepdims=True) * jnp.exp(ii - m_t)
        wi = jnp.exp(inter - m_t)
        q_col = jnp.sum(jnp.where(eye, q, 0.0), axis=1, keepdims=True)
        k_col = jnp.sum(jnp.where(eye, k, 0.0), axis=1, keepdims=True)
        cst = c0_ref[h]
        nrow = n0_ref[h]
        num = wi * jnp.sum(q_col * cst, axis=0, keepdims=True) + s * v
        den = wi * jnp.sum(q * nrow, axis=1, keepdims=True) + s
        hh = num / jnp.maximum(jnp.abs(den), jnp.exp(-m_t))
        m_new = m_t
        wc = jnp.exp(inter - m_new)
        wg = jnp.exp(ii - m_new)
        c_ref[h] = wc * cst + (wg * k_col) * v
        n_ref[h] = wc * nrow + wg * k
        m_ref[h] = jnp.broadcast_to(m_new, (1, LANES))
        hn = hh * lax.rsqrt(jnp.mean(hh * hh, axis=1, keepdims=True) + LN_EPS)
        hn = hn * ng_ref[:, h * dv:(h + 1) * dv] * _sigmoid(og_all[:, h * dv:(h + 1) * dv])
        h_ref[:, h * dv:(h + 1) * dv] = hn.astype(h_ref.dtype)


def _mlstm_sample(zm, gates, gate_bias, norm_g, c0, n0, m0, layer, nreq):
    nh = B_HEADS
    v_w = norm_g.shape[1]
    qk_w = v_w // 2
    dk, dv = qk_w // nh, v_w // nh
    rows = zm.shape[0]
    full = lambda shape: pl.BlockSpec(shape, lambda b: (0,) * len(shape))
    return pl.pallas_call(
        functools.partial(_mlstm_sample_kernel, nh=nh, dk=dk, dv=dv),
        grid=(nreq,),
        in_specs=[
            pl.BlockSpec((rows, qk_w), lambda b: (0, 0)),
            pl.BlockSpec((rows, qk_w), lambda b: (0, 1)),
            pl.BlockSpec((rows, v_w), lambda b: (0, 1)),
            pl.BlockSpec((rows, v_w), lambda b: (0, 2)),
            full(gates.shape), full(gate_bias.shape), full((1, v_w)),
            pl.BlockSpec((None, None, nh, dk, dv), lambda b: (layer, b, 0, 0, 0)),
            pl.BlockSpec((None, None, nh, 1, dk), lambda b: (layer, b, 0, 0, 0)),
            pl.BlockSpec((None, None, nh, 1, LANES), lambda b: (layer, b, 0, 0, 0)),
        ],
        out_specs=[
            pl.BlockSpec((None, 1, v_w), lambda b: (b, 0, 0)),
            pl.BlockSpec((None, nh, dk, dv), lambda b: (b, 0, 0, 0)),
            pl.BlockSpec((None, nh, 1, dk), lambda b: (b, 0, 0, 0)),
            pl.BlockSpec((None, nh, 1, LANES), lambda b: (b, 0, 0, 0)),
        ],
        out_shape=[
            jax.ShapeDtypeStruct((nreq, 1, v_w), F32),
            jax.ShapeDtypeStruct((nreq, nh, dk, dv), F32),
            jax.ShapeDtypeStruct((nreq, nh, 1, dk), F32),
            jax.ShapeDtypeStruct((nreq, nh, 1, LANES), F32),
        ],
        compiler_params=_cparams(1, 32),
        name="mlstm_sample",
    )(zm, zm, zm, zm, gates, gate_bias, norm_g, c0, n0, m0)


def _rglru_sample_kernel(gate_ref, xr_ref, buf_ref, h0_ref, cw_ref, cb_ref, wa_ref, ba_ref, wx_ref,
                         bx_ref, lam_ref, y_ref, h_ref, nbuf_ref):
    nbuf = CONV_WIDTH - 1
    xr = xr_ref[...]
    xconv = cb_ref[...] + buf_ref[0] * cw_ref[0:1, :]
    for j in range(1, nbuf):
        xconv = xconv + buf_ref[j] * cw_ref[j:j + 1, :]
    xconv = xconv + xr * cw_ref[nbuf:nbuf + 1, :]
    a, bterm = _rglru_gates(xconv, wa_ref, ba_ref[...], wx_ref, bx_ref[...], lam_ref[...])
    h = a * h0_ref[...] + bterm
    h_ref[...] = h
    y = _gelu_tanh(gate_ref[...]) * h
    pad_rows = y_ref.shape[0] - y.shape[0]
    y_ref[...] = jnp.concatenate([y, jnp.zeros((pad_rows, y.shape[1]), F32)], axis=0).astype(y_ref.dtype)
    for j in range(nbuf - 1):
        nbuf_ref[j] = buf_ref[j + 1]
    nbuf_ref[nbuf - 1] = xr


def _rglru_sample(zr, layer, buf_t, h0, conv_w, conv_b, w_a, b_a, w_x, b_x, lam, nreq):
    rows, w2 = zr.shape
    width = w2 // 2
    nblk, bd = w_a.shape[1], w_a.shape[2]
    nbuf = CONV_WIDTH - 1
    vec = lambda a: a.reshape(a.shape[0], 1, width)
    vspec = pl.BlockSpec((None, 1, width), lambda i: (layer, 0, 0))
    wspec = pl.BlockSpec((None, nblk, bd, bd), lambda i: (layer, 0, 0, 0))
    return pl.pallas_call(
        _rglru_sample_kernel,
        grid=(1,),
        in_specs=[
            pl.BlockSpec((nreq, width), lambda i: (0, 0)),
            pl.BlockSpec((nreq, width), lambda i: (0, 1)),
            pl.BlockSpec((None, nbuf, nreq, width), lambda i: (layer, 0, 0, 0)),
            pl.BlockSpec((None, nreq, width), lambda i: (layer, 0, 0)),
            pl.BlockSpec((None, CONV_WIDTH, width), lambda i: (layer, 0, 0)),
            vspec, wspec, vspec, wspec, vspec, vspec,
        ],
        out_specs=[
            pl.BlockSpec((rows, width), lambda i: (0, 0)),
            pl.BlockSpec((nreq, width), lambda i: (0, 0)),
            pl.BlockSpec((nbuf, nreq, width), lambda i: (0, 0, 0)),
        ],
        out_shape=[
            jax.ShapeDtypeStruct((rows, width), BF16),
            jax.ShapeDtypeStruct((nreq, width), F32),
            jax.ShapeDtypeStruct((nbuf, nreq, width), F32),
        ],
        compiler_params=_cparams(1, 32),
        name="rglru_sample",
    )(zr, zr, buf_t, h0, conv_w, vec(conv_b), w_a, vec(b_a), w_x, vec(b_x), vec(lam))


def _rope_tables(pos):
    half = A_HEAD_DIM // 2
    inv = ROPE_THETA ** (-jnp.arange(half, dtype=F32) / half)
    ang = pos.astype(F32)[:, None] * inv[None, :]
    cos, sin = jnp.cos(ang), jnp.sin(ang)
    return jnp.concatenate([cos, cos], axis=1), jnp.concatenate([-sin, sin], axis=1)


def kernel(x_prompt, x_sample, cache_k, cache_v, state_mlstm_c, state_mlstm_n, state_mlstm_m,
           state_rglru_h, state_conv, page_table, c_prompt, c_sample, w_ada, b_ada, ln_g, ln_b,
           w_ffn1_in, w_ffn1_out, w_ffn2_in, w_ffn2_out, w_in_ab, b_igate, b_fgate, mlstm_norm_g,
           w_out_ab, w_in_rg, conv_w, conv_b, w_rg_a, b_rg_a, w_rg_x, b_rg_x, lru_lambda, w_out_rg):
    bsz, seq, d = x_prompt.shape
    nreq = x_sample.shape[0]
    depth = w_ada.shape[0]
    n_even = w_in_ab.shape[0]
    a_w = cache_k.shape[3] * cache_k.shape[4]
    v_w = mlstm_norm_g.shape[1]
    qk_w = v_w // 2
    m_w = 2 * qk_w + 2 * v_w
    nh_b = B_HEADS
    past_len = page_table.shape[1] * cache_k.shape[2]
    npages = page_table.shape[1]
    rows_p = bsz * seq
    rs = SAMPLE_ROWS

    tm = 1024
    tn = 512
    tm_p = 1024
    tn_p = 1024
    tm_out = 256
    tk_cast = 512
    tt_rg = 256

    c_all = jnp.concatenate([c_prompt, c_sample, jnp.zeros((rs - bsz - nreq, d), F32)], axis=0)
    mods = _mods(c_all, w_ada, b_ada).reshape(depth, rs, 9, d)
    mods_p = jnp.transpose(mods[:, :bsz], (0, 2, 1, 3)).reshape(depth * 9 * bsz, 1, d)
    mods_s = jnp.transpose(mods[:, bsz:bsz + nreq], (0, 2, 1, 3))
    mods_s = jnp.pad(mods_s, ((0, 0), (0, 0), (0, rs - nreq), (0, 0))).reshape(depth * 9, rs, d)

    def mod_spec_p(l, s, j):
        base = ((l * 3 + s) * 3 + j) * bsz

        def make(tile_rows):
            tiles_per_batch = seq // tile_rows
            return pl.BlockSpec((None, 1, d), lambda i: (base + i // tiles_per_batch, 0, 0))
        return make

    def mod_spec_s(l, s, j):
        row = (l * 3 + s) * 3 + j

        def make(tile_rows):
            return pl.BlockSpec((None, rs, d), lambda i: (row, 0, 0))
        return make

    ln_g3 = ln_g.reshape(depth * 3, 1, d)
    ln_b3 = ln_b.reshape(depth * 3, 1, d)

    cos_p, sin_p = _rope_tables(jnp.arange(seq, dtype=jnp.int32))
    cos_s, sin_s = _rope_tables(jnp.full((rs,), past_len, dtype=jnp.int32))

    tbl_flat = page_table.reshape(-1).astype(jnp.int32)
    n0_s = state_mlstm_n[:, :, :, None, :]
    m0_s = jnp.broadcast_to(state_mlstm_m[:, :, :, None, None], state_mlstm_m.shape + (1, LANES))
    conv_t = jnp.transpose(state_conv, (0, 2, 1, 3))
    gate_bias = jnp.concatenate([b_igate, b_fgate], axis=1)
    w_ab_t = jnp.swapaxes(w_in_ab, 1, 2)

    xp = x_prompt.reshape(rows_p, d)
    xs = jnp.pad(x_sample.reshape(nreq, d), ((0, rs - nreq), (0, 0)))
    up = _modulate(xp, mod_spec_p(0, 0, 1), mod_spec_p(0, 0, 0), mods_p, tm)
    us = _modulate(xs, mod_spec_s(0, 0, 1), mod_spec_s(0, 0, 0), mods_s, rs)

    op = dict(k=[], v=[], c=[], n=[], m=[], h=[], buf=[])
    os_ = dict(k=[], v=[], c=[], n=[], m=[], h=[], buf=[])
    rope = (cos_p, sin_p, cos_s, sin_s)
    k_st = v_st = ()
    pad_rows = lambda t, w: jnp.pad(t.reshape(nreq, w), ((0, rs - nreq), (0, 0))).astype(BF16)
    heads3 = lambda t: t.reshape(rs, -1, A_HEAD_DIM)

    for l in range(depth):
        wb_ffn1 = _cast_bf16(w_ffn1_out, l, tk_cast)
        wb_ffn2 = _cast_bf16(w_ffn2_out, l, tk_cast)
        wb_mix = _cast_bf16(w_out_ab if l % 2 == 0 else w_out_rg, l // 2, tk_cast)

        def post(parts_p, parts_s, wb, s, res_w, nxt):
            nl, ns = nxt
            pick = lambda spec: [spec(l, s, 2), spec(nl, ns, 1), spec(nl, ns, 0)]
            xo, uo, xso, uso = _out_ln(parts_p, parts_s, wb, xp, xs, mods_p, mods_s, pick(mod_spec_p),
                                       pick(mod_spec_s), ln_g3, ln_b3, l * 3 + s, res_w,
                                       tm_out if wb.shape[0] > d else 2 * tm_out)
            return (xo, uo), (xso, uso)

        act_p, act_s = _swiglu_in(up, us, w_ffn1_in, l, tm, tn)
        (xp, up), (xs, us) = post([act_p], [act_s], wb_ffn1, 0, FFN_RES, (l, 1))

        if l % 2 == 0:
            e = l // 2
            q_p, q_s = _proj(up, us, w_ab_t, e, 0, a_w, tm_p, tn_p, F32, rope=rope, w_t=True, name="proj_q")
            tm_kv = tm_p if e == 0 else tm_p // 2
            k_st = _proj(up, us, w_ab_t, e, a_w, a_w, tm_kv, tn_p, F32, rope=rope, w_t=True,
                         stack=k_st, name="proj_k")
            v_st = _proj(up, us, w_ab_t, e, 2 * a_w, a_w, tm_kv, tn_p, F32, w_t=True,
                         stack=v_st, name="proj_v")
            k_s, v_s = k_st[1][e], v_st[1][e]
            zm_p, zm_s = _proj(up, us, w_ab_t, e, 3 * a_w, m_w, tm_p, tn_p, F32, w_t=True, name="proj_mlstm")
            n_gates = 2 * nh_b
            g_p, g_s = _proj(up, us, w_ab_t, e, 3 * a_w + m_w, n_gates, tm_p, n_gates, F32, w_t=True,
                             name="proj_gates")
            gb = gate_bias[e:e + 1]
            ng = mlstm_norm_g[e:e + 1]

            oa = _moba_prompt(q_p, k_st[0], v_st[0], e, bsz)
            hm, c1, n1, m1, ksum = _mlstm_prompt(zm_p, g_p, gb, ng, bsz, cache_k, e, tbl_flat)
            mix_p = [oa, hm.reshape(rows_p, v_w)]
            op["c"].append(c1.reshape(bsz, nh_b, c1.shape[1], c1.shape[2]))
            op["n"].append(n1.reshape(bsz, nh_b, -1))
            op["m"].append(m1[:, 0, 0].reshape(bsz, nh_b))

            sel = _moba_sample_gate(q_s, ksum, nreq)
            oa = _moba_sample_attn(heads3(q_s), heads3(k_s), heads3(v_s), cache_k, cache_v, e,
                                   tbl_flat, sel.reshape(-1), nreq, npages)
            hm, c1, n1, m1 = _mlstm_sample(zm_s, g_s, gb, ng, state_mlstm_c, n0_s, m0_s, e, nreq)
            mix_s = [pad_rows(oa, a_w), pad_rows(hm, v_w)]
            os_["c"].append(c1)
            os_["n"].append(n1.reshape(nreq, nh_b, -1))
            os_["m"].append(m1[:, :, 0, 0])
        else:
            od = l // 2
            zr_p, zr_s = _proj(up, us, w_in_rg, od, 0, w_in_rg.shape[2], tm_p, tn_p, F32, name="proj_rg")
            y_p, h1, nb = _rglru_prompt(zr_p, od, conv_w, conv_b, w_rg_a, b_rg_a, w_rg_x, b_rg_x,
                                        lru_lambda, bsz, tt_rg)
            op["h"].append(h1.reshape(bsz, -1))
            op["buf"].append(nb)
            y_s, h1, nb = _rglru_sample(zr_s, od, conv_t, state_rglru_h, conv_w, conv_b, w_rg_a,
                                        b_rg_a, w_rg_x, b_rg_x, lru_lambda, nreq)
            os_["h"].append(h1)
            os_["buf"].append(jnp.transpose(nb, (1, 0, 2)))
            mix_p, mix_s = [y_p], [y_s]
        (xp, up), (xs, us) = post(mix_p, mix_s, wb_mix, 1, 1.0, (l, 2))

        act_p, act_s = _swiglu_in(up, us, w_ffn2_in, l, tm, tn)
        (xp, up), (xs, us) = post([act_p], [act_s], wb_ffn2, 2, FFN_RES, (min(l + 1, depth - 1), 0))

    st = jnp.stack
    kv_p = lambda t: t.reshape(n_even, bsz, seq, -1, A_HEAD_DIM)
    kv_s = lambda t: t[:, :nreq].reshape(n_even, nreq, 1, -1, A_HEAD_DIM)
    return (xp.reshape(bsz, seq, d), xs[:nreq].reshape(nreq, 1, d),
            kv_p(k_st[0]), kv_p(v_st[0]), kv_s(k_st[1]), kv_s(v_st[1]),
            st(op["c"]), st(op["n"]), st(op["m"]), st(os_["c"]), st(os_["n"]), st(os_["m"]),
            st(op["h"]), st(op["buf"]), st(os_["h"]), st(os_["buf"]))
```

```python
import functools

import numpy as np
import jax
import jax.numpy as jnp
from jax import lax
from jax.experimental import pallas as pl
from jax.experimental.pallas import tpu as pltpu

F32 = jnp.float32
BF16 = jnp.bfloat16

DEPTH = 4
MOBA_BLOCK = 256
MOBA_TOPK = 3
MOBA_GROUP = 4
MOBA_HEADS_PER_STEP = 4
MLSTM_CHUNKS_PER_STEP = 1
ROPE_THETA = 10000.0
A_HEAD_DIM = 128
B_HEADS = 4
MLSTM_CHUNK = 64
RG_BLOCKS = 16
CONV_WIDTH = 4
RG_C = 8.0
FFN_RES = 0.5
ALPHA = (2.0 * DEPTH) ** 0.25
LN_EPS = 1e-5

LANES = 128
SAMPLE_ROWS = 16
NEG_BIG = -1e30
MIB = 1024 * 1024


def _cparams(n_axes, vmem_mib):
    return pltpu.CompilerParams(
        dimension_semantics=("arbitrary",) * n_axes,
        vmem_limit_bytes=int(vmem_mib * MIB),
    )


def _sigmoid(x):
    return 0.5 * jnp.tanh(0.5 * x) + 0.5


def _softplus(x):
    return jnp.maximum(x, 0.0) + jnp.log1p(jnp.exp(-jnp.abs(x)))


def _gelu_tanh(x):
    return 0.5 * x * (1.0 + jnp.tanh(0.7978845608028654 * (x + 0.044715 * x * x * x)))


def _mods_kernel(c_ref, w_ref, b_ref, o_ref):
    c = c_ref[...]
    s = (c * _sigmoid(c)).astype(BF16)
    o_ref[...] = jnp.dot(s, w_ref[...].astype(BF16), preferred_element_type=F32) + b_ref[...]


def _mods(c_all, w_ada, b_ada):
    depth, k, n = w_ada.shape
    rows = c_all.shape[0]
    tn = 1024 if n % 1024 == 0 else n
    return pl.pallas_call(
        _mods_kernel,
        grid=(depth, n // tn),
        in_specs=[
            pl.BlockSpec((rows, k), lambda l, j: (0, 0)),
            pl.BlockSpec((None, k, tn), lambda l, j: (l, 0, j)),
            pl.BlockSpec((None, 1, tn), lambda l, j: (l, 0, j)),
        ],
        out_specs=pl.BlockSpec((None, rows, tn), lambda l, j: (l, 0, j)),
        out_shape=jax.ShapeDtypeStruct((depth, rows, n), F32),
        compiler_params=_cparams(2, 48),
        name="adaln_mods",
    )(c_all, w_ada, b_ada.reshape(depth, 1, n))


def _modulate_kernel(x_ref, sc_ref, sh_ref, u_ref):
    u_ref[...] = (x_ref[...] * (1.0 + sc_ref[...]) + sh_ref[...]).astype(u_ref.dtype)


def _modulate(x, sc_spec, sh_spec, mods, tm):
    m, d = x.shape
    return pl.pallas_call(
        _modulate_kernel,
        grid=(m // tm,),
        in_specs=[pl.BlockSpec((tm, d), lambda i: (i, 0)), sc_spec(tm), sh_spec(tm)],
        out_specs=pl.BlockSpec((tm, d), lambda i: (i, 0)),
        out_shape=jax.ShapeDtypeStruct((m, d), BF16),
        compiler_params=_cparams(1, 32),
        name="modulate0",
    )(x, mods, mods)


def _rope_store(z, cos, sin, o_ref):
    for j in range(z.shape[1] // A_HEAD_DIM):
        zj = z[:, j * A_HEAD_DIM:(j + 1) * A_HEAD_DIM]
        o_ref[:, j * A_HEAD_DIM:(j + 1) * A_HEAD_DIM] = (
            zj * cos + pltpu.roll(zj, A_HEAD_DIM // 2, 1) * sin).astype(o_ref.dtype)


def xs_ref_of(refs):
    return refs[1]


def _proj_kernel(*refs, rope, n_prev, w_mode, narrow=False):
    if narrow:
        wn_ref, on_ref, osn_ref, wnb_ref = refs[len(refs) - 7], refs[len(refs) - 4], refs[len(refs) - 3], refs[-1]
        refs = refs[:len(refs) - 7] + refs[len(refs) - 6:len(refs) - 4] + refs[len(refs) - 2:len(refs) - 1]
        last = (((1,), (1,)), ((), ()))

        @pl.when(pl.program_id(1) == 0)
        def _():
            wnb_ref[...] = wn_ref[...].astype(BF16)
            osn_ref[...] = lax.dot_general(xs_ref_of(refs)[...], wnb_ref[...], last, preferred_element_type=F32)

        on_ref[...] = lax.dot_general(refs[0][...], wnb_ref[...], last, preferred_element_type=F32)
    if n_prev:
        prev_ref, prevs_ref = refs[len(refs) - 5:len(refs) - 3]
        refs = refs[:len(refs) - 5] + refs[len(refs) - 3:]
    if rope:
        x_ref, xs_ref, w_ref, cos_ref, sin_ref, coss_ref, sins_ref, o_ref, os_ref, wb_ref = refs
    else:
        x_ref, xs_ref, w_ref, o_ref, os_ref, wb_ref = refs
    if n_prev:
        o_ref[0:n_prev] = prev_ref[...]
        o_ref = o_ref.at[n_prev]

        @pl.when(pl.program_id(1) == 0)
        def _():
            os_ref[0:n_prev] = prevs_ref[...]

        os_ref = os_ref.at[n_prev]

    def mm(x):
        if w_mode == "nk_contract_last":
            return lax.dot_general(x, wb_ref[...], (((1,), (1,)), ((), ())), preferred_element_type=F32)
        return jnp.dot(x, wb_ref[...], preferred_element_type=F32)

    @pl.when(pl.program_id(1) == 0)
    def _():
        w = w_ref[...]
        if w_mode == "nk_transpose":
            w = w.T
        wb_ref[...] = w.astype(BF16)
        zs = mm(xs_ref[...])
        if rope:
            _rope_store(zs, coss_ref[...], sins_ref[...], os_ref)
        else:
            os_ref[...] = zs.astype(os_ref.dtype)

    z = mm(x_ref[...])
    if rope:
        _rope_store(z, cos_ref[...], sin_ref[...], o_ref)
    else:
        o_ref[...] = z.astype(o_ref.dtype)


def _swiglu_in_kernel(x_ref, xs_ref, wg_ref, wv_ref, o_ref, os_ref, wgb_ref, wvb_ref):
    def act(x):
        g = jnp.dot(x, wgb_ref[...], preferred_element_type=F32)
        v = jnp.dot(x, wvb_ref[...], preferred_element_type=F32)
        return g * _sigmoid(g) * v

    @pl.when(pl.program_id(1) == 0)
    def _():
        wgb_ref[...] = wg_ref[...].astype(BF16)
        wvb_ref[...] = wv_ref[...].astype(BF16)
        os_ref[...] = act(xs_ref[...]).astype(os_ref.dtype)

    o_ref[...] = act(x_ref[...]).astype(o_ref.dtype)


def _proj(x, xs, w, layer, col0, ncols, tm, tn, out_dtype, rope=None, stack=None, w_t=False,
          narrow=None, name="proj"):
    m, k = x.shape
    rows_s = xs.shape[0]
    c0 = col0 // tn
    if w_t:
        w_spec = pl.BlockSpec((None, tn, k), lambda j, i: (layer, c0 + j, 0))
        w_mode = "nk_transpose" if tn % LANES == 0 else "nk_contract_last"
    else:
        w_spec = pl.BlockSpec((None, k, tn), lambda j, i: (layer, 0, c0 + j))
        w_mode = "kn"
    wb_shape = (tn, k) if w_mode == "nk_contract_last" else (k, tn)
    in_specs = [
        pl.BlockSpec((tm, k), lambda j, i: (i, 0)),
        pl.BlockSpec((rows_s, k), lambda j, i: (0, 0)),
        w_spec,
    ]
    args = [x, xs, w]
    if rope is not None:
        period = rope[0].shape[0] // tm
        in_specs += [pl.BlockSpec((tm, A_HEAD_DIM), lambda j, i: (i % period, 0))] * 2
        in_specs += [pl.BlockSpec((rows_s, A_HEAD_DIM), lambda j, i: (0, 0))] * 2
        args += list(rope)
    n_prev = 0
    if stack is None:
        out_specs = [pl.BlockSpec((tm, tn), lambda j, i: (i, j)),
                     pl.BlockSpec((rows_s, tn), lambda j, i: (0, j))]
        out_shape = [jax.ShapeDtypeStruct((m, ncols), out_dtype),
                     jax.ShapeDtypeStruct((rows_s, ncols), out_dtype)]
    elif len(stack) == 0:
        out_specs = [pl.BlockSpec((None, tm, tn), lambda j, i: (0, i, j)),
                     pl.BlockSpec((None, rows_s, tn), lambda j, i: (0, 0, j))]
        out_shape = [jax.ShapeDtypeStruct((1, m, ncols), out_dtype),
                     jax.ShapeDtypeStruct((1, rows_s, ncols), out_dtype)]
    else:
        n_prev = stack[0].shape[0]
        in_specs += [pl.BlockSpec((n_prev, tm, tn), lambda j, i: (0, i, j)),
                     pl.BlockSpec((n_prev, rows_s, tn), lambda j, i: (0, 0, j))]
        args += list(stack)
        out_specs = [pl.BlockSpec((n_prev + 1, tm, tn), lambda j, i: (0, i, j)),
                     pl.BlockSpec((n_prev + 1, rows_s, tn), lambda j, i: (0, 0, j))]
        out_shape = [jax.ShapeDtypeStruct((n_prev + 1, m, ncols), out_dtype),
                     jax.ShapeDtypeStruct((n_prev + 1, rows_s, ncols), out_dtype)]
    scratch = [pltpu.VMEM(wb_shape, BF16)]
    if narrow is not None:
        row0, nn = narrow
        in_specs += [pl.BlockSpec((None, nn, k), lambda j, i: (layer, row0 // nn, 0))]
        args += [w]
        nct = ncols // tn
        out_specs += [pl.BlockSpec((None, tm, nn), lambda j, i: (j, i, 0)),
                      pl.BlockSpec((None, rows_s, nn), lambda j, i: (j, 0, 0))]
        out_shape += [jax.ShapeDtypeStruct((nct, m, nn), F32), jax.ShapeDtypeStruct((nct, rows_s, nn), F32)]
        scratch += [pltpu.VMEM((nn, k), BF16)]
    return pl.pallas_call(
        functools.partial(_proj_kernel, rope=rope is not None, n_prev=n_prev, w_mode=w_mode,
                          narrow=narrow is not None),
        grid=(ncols // tn, m // tm),
        in_specs=in_specs,
        out_specs=out_specs,
        out_shape=out_shape,
        scratch_shapes=scratch,
        compiler_params=_cparams(2, 56),
        name=name,
    )(*args)


def _swiglu_in(x, xs, w, layer, tm, tn):
    m, k = x.shape
    rows_s = xs.shape[0]
    dff = w.shape[2] // 2
    nv = dff // tn
    return pl.pallas_call(
        _swiglu_in_kernel,
        grid=(dff // tn, m // tm),
        in_specs=[
            pl.BlockSpec((tm, k), lambda j, i: (i, 0)),
            pl.BlockSpec((rows_s, k), lambda j, i: (0, 0)),
            pl.BlockSpec((None, k, tn), lambda j, i: (layer, 0, j)),
            pl.BlockSpec((None, k, tn), lambda j, i: (layer, 0, nv + j)),
        ],
        out_specs=[pl.BlockSpec((tm, tn), lambda j, i: (i, j)),
                   pl.BlockSpec((rows_s, tn), lambda j, i: (0, j))],
        out_shape=[jax.ShapeDtypeStruct((m, dff), BF16), jax.ShapeDtypeStruct((rows_s, dff), BF16)],
        scratch_shapes=[pltpu.VMEM((k, tn), BF16), pltpu.VMEM((k, tn), BF16)],
        compiler_params=_cparams(2, 56),
        name="swiglu_in",
    )(x, xs, w, w)


def _cast_kernel(w_ref, o_ref):
    o_ref[...] = w_ref[...].astype(o_ref.dtype)


def _cast_bf16(w, layer, tk):
    _, k, n = w.shape
    return pl.pallas_call(
        _cast_kernel,
        grid=(k // tk,),
        in_specs=[pl.BlockSpec((None, tk, n), lambda i: (layer, i, 0))],
        out_specs=pl.BlockSpec((tk, n), lambda i: (i, 0)),
        out_shape=jax.ShapeDtypeStruct((k, n), BF16),
        compiler_params=_cparams(1, 32),
        name="cast_bf16",
    )(w)


def _out_ln_kernel(*refs, n_parts, res_w):
    ap_refs = refs[:n_parts]
    as_refs = refs[n_parts:2 * n_parts]
    (w_ref, x_ref, gate_ref, sc_ref, sh_ref, xs_ref, gates_ref, scs_ref, shs_ref, lng_ref, lnb_ref,
     xo_ref, uo_ref, xso_ref, uso_ref) = refs[2 * n_parts:]

    def run(a_refs, x_ref, gate_ref, sc_ref, sh_ref, xo_ref, uo_ref):
        acc = None
        off = 0
        for a_ref in a_refs:
            kk = a_ref.shape[1]
            part = jnp.dot(a_ref[...], w_ref[off:off + kk, :], preferred_element_type=F32)
            acc = part if acc is None else acc + part
            off += kk
        y = ALPHA * x_ref[...] + (res_w * (1.0 + gate_ref[...])) * acc
        mu = jnp.mean(y, axis=-1, keepdims=True)
        yc = y - mu
        var = jnp.mean(yc * yc, axis=-1, keepdims=True)
        xh = yc * lax.rsqrt(var + LN_EPS)
        g = lng_ref[...]
        b = lnb_ref[...]
        mod = 1.0 + sc_ref[...]
        xo_ref[...] = xh * g + b
        uo_ref[...] = (xh * (g * mod) + (b * mod + sh_ref[...])).astype(uo_ref.dtype)

    @pl.when(pl.program_id(0) == 0)
    def _():
        run(as_refs, xs_ref, gates_ref, scs_ref, shs_ref, xso_ref, uso_ref)

    run(ap_refs, x_ref, gate_ref, sc_ref, sh_ref, xo_ref, uo_ref)


def _out_ln(parts, parts_s, wb, x, xs, mods, mods_s, specs, specs_s, ln_g, ln_b, ln_row, res_w, tm):
    m = parts[0].shape[0]
    rows_s = xs.shape[0]
    k, d = wb.shape
    ln_spec = pl.BlockSpec((None, 1, d), lambda i: (ln_row, 0, 0))
    row = lambda width: pl.BlockSpec((tm, width), lambda i: (i, 0))
    whole = lambda width: pl.BlockSpec((rows_s, width), lambda i: (0, 0))
    return pl.pallas_call(
        functools.partial(_out_ln_kernel, n_parts=len(parts), res_w=res_w),
        grid=(m // tm,),
        in_specs=[row(p.shape[1]) for p in parts] + [whole(p.shape[1]) for p in parts_s] + [
            pl.BlockSpec((k, d), lambda i: (0, 0), pipeline_mode=pl.Buffered(1)),
            row(d)] + [s(tm) for s in specs] + [whole(d)] + [s(rows_s) for s in specs_s] + [
            ln_spec, ln_spec],
        out_specs=[row(d), row(d), whole(d), whole(d)],
        out_shape=[jax.ShapeDtypeStruct((m, d), F32), jax.ShapeDtypeStruct((m, d), BF16),
                   jax.ShapeDtypeStruct((rows_s, d), F32), jax.ShapeDtypeStruct((rows_s, d), BF16)],
        compiler_params=_cparams(1, 56),
        name="out_postnorm",
    )(*parts, *parts_s, wb, x, mods, mods, mods, xs, mods_s, mods_s, mods_s, ln_g, ln_b)


def _moba_prompt_kernel(q_ref, k_ref, v_ref, o_ref, kb_ref, vb_ref, kmean_ref,
                        m_ref, acc_ref, *, nblk, scale):
    qi = pl.program_id(2)
    blk = MOBA_BLOCK
    hd = A_HEAD_DIM
    nheads = q_ref.shape[1] // hd
    lanes = lambda hh: slice(hh * hd, (hh + 1) * hd)

    @pl.when(qi == 0)
    def _():
        seq = k_ref.shape[0]
        r = lax.broadcasted_iota(jnp.int32, (seq, hd), 0)
        c = lax.broadcasted_iota(jnp.int32, (seq, hd), 1)
        in_blk = (r >= c * blk) & (r < c * blk + blk)
        onehot = jnp.where(in_blk, 1.0, 0.0).astype(BF16)
        kmean_ref[...] = jnp.zeros_like(kmean_ref)
        for hh in range(nheads):
            kf = k_ref[:, lanes(hh)]
            kb_ref[hh, :, 0:hd] = kf.astype(BF16)
            kb_ref[hh, :, hd:2 * hd] = onehot
            vb_ref[hh, :, 0:hd] = v_ref[:, lanes(hh)].astype(BF16)
            vb_ref[hh, :, hd:2 * hd] = jnp.ones((seq, hd), BF16)
            kmean_ref[hh, 0:nblk, :] = jnp.mean(kf.reshape(nblk, blk, hd), axis=1)

    nt = (((1,), (1,)), ((), ()))
    start = pl.multiple_of(qi * blk, blk)
    q_augs = []
    for hh in range(nheads):
        q = q_ref[:, lanes(hh)]
        gate = lax.dot_general(kmean_ref[hh, 0:nblk, :], q, nt, precision=lax.Precision.HIGHEST,
                               preferred_element_type=F32)
        bi = lax.broadcasted_iota(jnp.int32, gate.shape, 0)
        bf = bi.astype(F32)
        g = jnp.where(bi < qi, gate, -jnp.inf)
        sel = jnp.zeros(gate.shape, F32)
        for _ in range(MOBA_TOPK):
            mx = jnp.max(g, axis=0, keepdims=True)
            cand = jnp.where(g == mx, bf, 1e9)
            cand = jnp.where(mx > -jnp.inf, cand, 1e9)
            pick = bf == jnp.min(cand, axis=0, keepdims=True)
            sel = jnp.where(pick, 1.0, sel)
            g = jnp.where(pick, -jnp.inf, g)
        qb = q.astype(BF16)
        neg = jnp.concatenate([(1.0 - sel) * NEG_BIG, jnp.zeros((hd - nblk, blk), F32)], axis=0).T
        q_augs.append(jnp.concatenate([qb, neg.astype(BF16)], axis=1))
        s = lax.dot_general(qb, kb_ref[hh, pl.ds(start, blk), 0:hd], nt, preferred_element_type=F32) * scale
        row = lax.broadcasted_iota(jnp.int32, s.shape, 0)
        col = lax.broadcasted_iota(jnp.int32, s.shape, 1)
        s = jnp.where(col <= row, s, NEG_BIG)
        m0 = jnp.max(s, axis=1, keepdims=True)
        p = jnp.exp(s - m0)
        m_ref[hh] = jnp.broadcast_to(m0, m_ref.shape[1:])
        acc_ref[hh] = jnp.dot(p.astype(BF16), vb_ref[hh, pl.ds(start, blk), :], preferred_element_type=F32)

    grp = MOBA_GROUP
    span = grp * blk

    def past_group(gi, carry):
        st = pl.multiple_of(gi * span, span)
        for hh in range(nheads):
            sn = lax.dot_general(q_augs[hh], kb_ref[hh, pl.ds(st, span), :], nt,
                                 preferred_element_type=F32) * scale
            m_prev = m_ref[hh]
            m_new = jnp.maximum(m_prev, jnp.max(sn, axis=1, keepdims=True))
            a = jnp.exp(m_prev - m_new)
            pn = jnp.exp(sn - jnp.concatenate([m_new] * (span // hd), axis=1))
            acc_ref[hh] = jnp.concatenate([a, a], axis=1) * acc_ref[hh] + jnp.dot(
                pn.astype(BF16), vb_ref[hh, pl.ds(st, span), :], preferred_element_type=F32)
            m_ref[hh] = m_new
        return carry

    lax.fori_loop(0, (qi + grp - 1) // grp, past_group, 0)
    for hh in range(nheads):
        acc = acc_ref[hh]
        o_ref[:, lanes(hh)] = (acc[:, 0:hd] / acc[:, hd:2 * hd]).astype(o_ref.dtype)


def _moba_prompt(q, k, v, slab, bsz):
    rows, width = q.shape
    seq = rows // bsz
    nh = width // A_HEAD_DIM
    nblk = seq // MOBA_BLOCK
    assert nblk % MOBA_GROUP == 0
    blk = MOBA_BLOCK
    hd = A_HEAD_DIM
    hps = MOBA_HEADS_PER_STEP
    assert nh % hps == 0
    return pl.pallas_call(
        functools.partial(_moba_prompt_kernel, nblk=nblk, scale=hd ** -0.5),
        grid=(bsz, nh // hps, nblk),
        in_specs=[
            pl.BlockSpec((blk, hps * hd), lambda b, h, i: (b * nblk + i, h)),
            pl.BlockSpec((None, seq, hps * hd), lambda b, h, i: (slab, b, h), pipeline_mode=pl.Buffered(1)),
            pl.BlockSpec((None, seq, hps * hd), lambda b, h, i: (slab, b, h), pipeline_mode=pl.Buffered(1)),
        ],
        out_specs=pl.BlockSpec((blk, hps * hd), lambda b, h, i: (b * nblk + i, h)),
        out_shape=jax.ShapeDtypeStruct((rows, width), BF16),
        scratch_shapes=[
            pltpu.VMEM((hps, seq, 2 * hd), BF16), pltpu.VMEM((hps, seq, 2 * hd), BF16),
            pltpu.VMEM((hps, LANES, hd), F32),
            pltpu.VMEM((hps, blk, hd), F32), pltpu.VMEM((hps, blk, 2 * hd), F32),
        ],
        compiler_params=_cparams(3, 52),
        name="moba_prompt",
    )(q, k, v)


def _mlstm_prompt_kernel(tbl_ref, q_ref, k_ref, v_ref, og_ref, g_ref, gb_ref, ng_ref, *refs,
                         nb, nh, dk, dv, pages_per_step, pages_per_blk):
    page_refs = refs[:pages_per_step]
    h_ref, c_ref, n_ref, m_ref, ksum_ref = refs[pages_per_step:]
    lc = MLSTM_CHUNK
    chunks = q_ref.shape[1] // lc

    @pl.when(pl.program_id(0) == 0)
    def _():
        c_ref[...] = jnp.zeros_like(c_ref)
        n_ref[...] = jnp.zeros_like(n_ref)
        m_ref[...] = jnp.zeros_like(m_ref)

    blks_per_step = pages_per_step // pages_per_blk
    for i in range(blks_per_step):
        tot = jnp.sum(page_refs[i * pages_per_blk][...], axis=0)
        for u in range(1, pages_per_blk):
            tot = tot + jnp.sum(page_refs[i * pages_per_blk + u][...], axis=0)
        ksum_ref[pl.program_id(0) * blks_per_step + i] = tot

    tt = lax.broadcasted_iota(jnp.int32, (lc, lc), 0)
    ss = lax.broadcasted_iota(jnp.int32, (lc, lc), 1)
    causal = ss <= tt
    eye = ss == tt
    nt = (((1,), (1,)), ((), ()))
    tn = (((0,), (0,)), ((), ()))
    gb = gb_ref[...]
    for cc in range(chunks):
        rows = slice(cc * lc, (cc + 1) * lc)
        for b in range(nb):
            gates = g_ref[b, rows, :] + gb
            log_f = jnp.minimum(gates, 0.0) - jnp.log1p(jnp.exp(-jnp.abs(gates)))
            for h in range(nh):
                bh = b * nh + h
                i_col = gates[:, h:h + 1]
                f_col = log_f[:, nh + h:nh + h + 1]
                f_row = jnp.sum(jnp.where(eye, f_col, 0.0), axis=0, keepdims=True)
                i_row = jnp.sum(jnp.where(eye, i_col, 0.0), axis=0, keepdims=True)
                b_col = jnp.sum(jnp.where(causal, f_row, 0.0), axis=1, keepdims=True)
                b_row = jnp.sum(jnp.where(ss >= tt, f_col, 0.0), axis=0, keepdims=True)
                m_prev = m_ref[bh][:, 0:1]
                d = jnp.where(causal, b_col - b_row + i_row, NEG_BIG)
                inter = b_col + m_prev
                m_t = jnp.maximum(jnp.max(d, axis=1, keepdims=True), inter)
                w = jnp.exp(d - m_t)
                q = q_ref[b, rows, h * dk:(h + 1) * dk]
                k = k_ref[b, rows, h * dk:(h + 1) * dk] * (dk ** -0.5)
                vb = v_ref[b, rows, h * dv:(h + 1) * dv].astype(BF16)
                qb = q.astype(BF16)
                s = lax.dot_general(qb, k.astype(BF16), nt, preferred_element_type=F32) * w
                wi = jnp.exp(inter - m_t)
                cst = c_ref[bh]
                nrow = n_ref[bh]
                num = wi * jnp.dot(qb, cst.astype(BF16), preferred_element_type=F32) + jnp.dot(
                    s.astype(BF16), vb, preferred_element_type=F32)
                den = wi * jnp.sum(q * nrow, axis=1, keepdims=True) + jnp.sum(s, axis=1, keepdims=True)
                hh = num / jnp.maximum(jnp.abs(den), jnp.exp(-m_t))
                b_last = b_col[lc - 1:lc, :]
                g_col = b_last - b_col + i_col
                m_new = jnp.maximum(b_last + m_prev, jnp.max(g_col, axis=0, keepdims=True))
                wc = jnp.exp(b_last + m_prev - m_new)
                kw = k * jnp.exp(g_col - m_new)
                c_ref[bh] = wc * cst + lax.dot_general(kw.astype(BF16), vb, tn, preferred_element_type=F32)
                n_ref[bh] = wc * nrow + jnp.sum(kw, axis=0, keepdims=True)
                m_ref[bh] = jnp.broadcast_to(m_new, (1, LANES))
                hn = hh * lax.rsqrt(jnp.mean(hh * hh, axis=1, keepdims=True) + LN_EPS)
                hn = hn * ng_ref[:, h * dv:(h + 1) * dv] * _sigmoid(og_ref[b, rows, h * dv:(h + 1) * dv])
                h_ref[b, rows, h * dv:(h + 1) * dv] = hn.astype(h_ref.dtype)


def _mlstm_prompt(zm, gates, gate_bias, norm_g, bsz, cache_k, layer, tbl_flat):
    rows, _ = zm.shape
    seq = rows // bsz
    nh = B_HEADS
    v_w = norm_g.shape[1]
    qk_w = v_w // 2
    dk, dv = qk_w // nh, v_w // nh
    lc = MLSTM_CHUNK * MLSTM_CHUNKS_PER_STEP
    assert seq % lc == 0
    nsteps = seq // lc
    page, kv_heads, hd = cache_k.shape[2:]
    ppb = MOBA_BLOCK // page
    total_pages = tbl_flat.shape[0]
    assert total_pages % nsteps == 0 and (total_pages // nsteps) % ppb == 0
    pps = total_pages // nsteps
    z3 = zm.reshape(bsz, seq, zm.shape[1])
    gw = gates.shape[1]
    g3 = gates.reshape(bsz, seq, gw)
    nbh = bsz * nh
    full = lambda shape: pl.BlockSpec(shape, lambda c, tbl: (0,) * len(shape))

    def page_spec(u):
        return pl.BlockSpec((None, None, page, kv_heads, hd),
                            lambda c, tbl: (layer, tbl[c * pps + u], 0, 0, 0))

    grid_spec = pltpu.PrefetchScalarGridSpec(
        num_scalar_prefetch=1,
        grid=(nsteps,),
        in_specs=[
            pl.BlockSpec((bsz, lc, qk_w), lambda c, tbl: (0, c, 0)),
            pl.BlockSpec((bsz, lc, qk_w), lambda c, tbl: (0, c, 1)),
            pl.BlockSpec((bsz, lc, v_w), lambda c, tbl: (0, c, 1)),
            pl.BlockSpec((bsz, lc, v_w), lambda c, tbl: (0, c, 2)),
            pl.BlockSpec((bsz, lc, gw), lambda c, tbl: (0, c, 0)),
            full((1, gw)), full((1, v_w)),
        ] + [page_spec(u) for u in range(pps)],
        out_specs=[
            pl.BlockSpec((bsz, lc, v_w), lambda c, tbl: (0, c, 0)),
            full((nbh, dk, dv)), full((nbh, 1, dk)), full((nbh, 1, LANES)),
            full((total_pages // ppb, kv_heads, hd)),
        ],
    )
    return pl.pallas_call(
        functools.partial(_mlstm_prompt_kernel, nb=bsz, nh=nh, dk=dk, dv=dv,
                          pages_per_step=pps, pages_per_blk=ppb),
        grid_spec=grid_spec,
        out_shape=[
            jax.ShapeDtypeStruct((bsz, seq, v_w), BF16),
            jax.ShapeDtypeStruct((nbh, dk, dv), F32),
            jax.ShapeDtypeStruct((nbh, 1, dk), F32),
            jax.ShapeDtypeStruct((nbh, 1, LANES), F32),
            jax.ShapeDtypeStruct((total_pages // ppb, kv_heads, hd), F32),
        ],
        compiler_params=_cparams(1, 48),
        name="mlstm_prompt",
    )(tbl_flat, z3, z3, z3, z3, g3, gate_bias, norm_g, *([cache_k] * pps))


def _rglru_gates(xconv, wa_ref, ba, wx_ref, bx, lam):
    bd = wa_ref.shape[1]
    r_parts, i_parts = [], []
    for n in range(wa_ref.shape[0]):
        xb = xconv[:, n * bd:(n + 1) * bd].astype(BF16)
        r_parts.append(jnp.dot(xb, wa_ref[n].astype(BF16), preferred_element_type=F32))
        i_parts.append(jnp.dot(xb, wx_ref[n].astype(BF16), preferred_element_type=F32))
    r = _sigmoid(jnp.concatenate(r_parts, axis=1) + ba)
    ig = _sigmoid(jnp.concatenate(i_parts, axis=1) + bx)
    log_a = (-RG_C) * r * _softplus(-lam)
    a = jnp.exp(log_a)
    mult = jnp.sqrt(1.0 - a * a)
    return a, mult * ig * xconv


def _rglru_prompt_kernel(gate_ref, xr_ref, cw_ref, cb_ref, wa_ref, ba_ref, wx_ref, bx_ref, lam_ref,
                         y_ref, hl_ref, buf_ref, xext_ref, a_ref, b_ref, hs_ref, hc_ref):
    t = pl.program_id(1)
    tt = xr_ref.shape[0]
    pad = 8
    nbuf = CONV_WIDTH - 1

    @pl.when(t == 0)
    def _():
        xext_ref[0:pad, :] = jnp.zeros((pad, xext_ref.shape[1]), F32)
        hc_ref[...] = jnp.zeros_like(hc_ref)

    @pl.when(t > 0)
    def _():
        xext_ref[0:pad, :] = xext_ref[tt:tt + pad, :]

    xr = xr_ref[...]
    xext_ref[pad:pad + tt, :] = xr
    xconv = cb_ref[...] + xext_ref[pad - nbuf:pad - nbuf + tt, :] * cw_ref[0:1, :]
    for j in range(1, CONV_WIDTH):
        xconv = xconv + xext_ref[pad - nbuf + j:pad - nbuf + j + tt, :] * cw_ref[j:j + 1, :]
    a, bterm = _rglru_gates(xconv, wa_ref, ba_ref[...], wx_ref, bx_ref[...], lam_ref[...])
    a_ref[...] = a
    b_ref[...] = bterm

    def step(i, h):
        h = a_ref[pl.ds(i, 1), :] * h + b_ref[pl.ds(i, 1), :]
        hs_ref[pl.ds(i, 1), :] = h
        return h

    h_last = lax.fori_loop(0, tt, step, hc_ref[...], unroll=8)
    hc_ref[...] = h_last
    y_ref[...] = (_gelu_tanh(gate_ref[...]) * hs_ref[...]).astype(y_ref.dtype)
    hl_ref[...] = h_last
    buf_ref[...] = xr[tt - nbuf:tt, :]


def _rglru_prompt(zr, layer, conv_w, conv_b, w_a, b_a, w_x, b_x, lam, bsz, tt):
    rows, w2 = zr.shape
    width = w2 // 2
    seq = rows // bsz
    nt = seq // tt
    nblk, bd = w_a.shape[1], w_a.shape[2]
    vec = lambda a: a.reshape(a.shape[0], 1, width)
    vspec = pl.BlockSpec((None, 1, width), lambda b, t: (layer, 0, 0))
    wspec = pl.BlockSpec((None, nblk, bd, bd), lambda b, t: (layer, 0, 0, 0))
    nbuf = CONV_WIDTH - 1
    return pl.pallas_call(
        _rglru_prompt_kernel,
        grid=(bsz, nt),
        in_specs=[
            pl.BlockSpec((tt, width), lambda b, t: (b * nt + t, 0)),
            pl.BlockSpec((tt, width), lambda b, t: (b * nt + t, 1)),
            pl.BlockSpec((None, CONV_WIDTH, width), lambda b, t: (layer, 0, 0)),
            vspec, wspec, vspec, wspec, vspec, vspec,
        ],
        out_specs=[
            pl.BlockSpec((tt, width), lambda b, t: (b * nt + t, 0)),
            pl.BlockSpec((None, 1, width), lambda b, t: (b, 0, 0)),
            pl.BlockSpec((None, nbuf, width), lambda b, t: (b, 0, 0)),
        ],
        out_shape=[
            jax.ShapeDtypeStruct((rows, width), BF16),
            jax.ShapeDtypeStruct((bsz, 1, width), F32),
            jax.ShapeDtypeStruct((bsz, nbuf, width), F32),
        ],
        scratch_shapes=[
            pltpu.VMEM((tt + 8, width), F32), pltpu.VMEM((tt, width), F32),
            pltpu.VMEM((tt, width), F32), pltpu.VMEM((tt, width), F32), pltpu.VMEM((1, width), F32),
        ],
        compiler_params=_cparams(2, 48),
        name="rglru_prompt",
    )(zr, zr, conv_w, vec(conv_b), w_a, vec(b_a), w_x, vec(b_x), vec(lam))


def _moba_gate_kernel(q_ref, ksum_ref, sel_ref, *, nheads, keys_per_blk):
    b = pl.program_id(0)
    hd = A_HEAD_DIM
    nblk = ksum_ref.shape[0]
    inv = 1.0 / keys_per_blk
    rowi = lax.broadcasted_iota(jnp.int32, (nblk, 1), 0).astype(F32)
    out_r = lax.broadcasted_iota(jnp.int32, sel_ref.shape, 0)
    out_c = lax.broadcasted_iota(jnp.int32, sel_ref.shape, 1)
    out = jnp.zeros(sel_ref.shape, jnp.int32)
    qrow = q_ref[pl.ds(b, 1), :]
    for h in range(nheads):
        km = ksum_ref[:, h, :] * inv
        g = jnp.sum(km * qrow[:, h * hd:(h + 1) * hd], axis=1, keepdims=True)
        for i in range(MOBA_TOPK):
            mx = jnp.max(g, axis=0, keepdims=True)
            idx = jnp.min(jnp.where(g == mx, rowi, 1e9), axis=0, keepdims=True)
            out = jnp.where((out_r == h) & (out_c == i), idx.astype(jnp.int32), out)
            g = jnp.where(rowi == idx, -jnp.inf, g)
    sel_ref[...] = out


def _moba_sample_gate(q, ksum, nreq):
    nheads, hd = ksum.shape[1:]
    nblk = ksum.shape[0] // nreq
    return pl.pallas_call(
        functools.partial(_moba_gate_kernel, nheads=nheads, keys_per_blk=MOBA_BLOCK),
        grid=(nreq,),
        in_specs=[pl.BlockSpec(q.shape, lambda b: (0, 0)),
                  pl.BlockSpec((nblk, nheads, hd), lambda b: (b, 0, 0))],
        out_specs=pl.BlockSpec((None, nheads, LANES), lambda b: (b, 0, 0)),
        out_shape=jax.ShapeDtypeStruct((nreq, nheads, LANES), jnp.int32),
        compiler_params=_cparams(1, 32),
        name="moba_sample_gate",
    )(q, ksum)


def _moba_sample_attn_kernel(tbl_ref, sel_ref, q_ref, kn_ref, vn_ref, *refs, npg, hps, scale):
    o_ref = refs[2 * npg * hps]
    b = pl.program_id(0)
    nheads = q_ref.shape[1]
    hd = q_ref.shape[2]
    for hh in range(hps):
        k_refs = refs[hh * npg:(hh + 1) * npg]
        v_refs = refs[(hps + hh) * npg:(hps + hh + 1) * npg]
        h = pl.program_id(1) * hps + hh
        is_h = lax.broadcasted_iota(jnp.int32, (1, nheads, 1), 1) == h
        q_m = jnp.where(is_h, q_ref[b][None], 0.0)

        def score(kp):
            part = jnp.sum(kp * q_m, axis=2, keepdims=True)
            return jnp.sum(part, axis=1, keepdims=True) * scale

        s_own = score(kn_ref[b][None])
        ss = [score(k_ref[...]) for k_ref in k_refs]
        m = s_own
        for s in ss:
            m = jnp.maximum(m, jnp.max(s, axis=0, keepdims=True))
        l = jnp.exp(s_own - m)
        acc = l * vn_ref[b][None]
        for s, v_ref in zip(ss, v_refs):
            p = jnp.exp(s - m)
            l = l + jnp.sum(p, axis=0, keepdims=True)
            acc = acc + jnp.sum(p * v_ref[...], axis=0, keepdims=True)
        out = jnp.sum(jnp.where(is_h, acc / l, 0.0), axis=1)
        o_ref[:, hh * hd:(hh + 1) * hd] = out.astype(o_ref.dtype)


SAMPLE_ATTN_HEADS_PER_STEP = 2


def _moba_sample_attn(q, k_new, v_new, cache_k, cache_v, layer, tbl_flat, sel_flat, nreq, npages):
    page, nheads, hd = cache_k.shape[2:]
    ppb = MOBA_BLOCK // page
    npg = MOBA_TOPK * ppb
    hps = SAMPLE_ATTN_HEADS_PER_STEP
    assert nheads % hps == 0

    def page_spec(hh, j):
        def page_map(b, g, tbl, sel):
            blk = sel[(b * nheads + g * hps + hh) * LANES + j // ppb]
            return (layer, tbl[b * npages + blk * ppb + j % ppb], 0, 0, 0)
        return pl.BlockSpec((None, None, page, nheads, hd), page_map)

    head = pl.BlockSpec(q.shape, lambda b, g, tbl, sel: (0, 0, 0))
    pages = [page_spec(hh, j) for hh in range(hps) for j in range(npg)]
    grid_spec = pltpu.PrefetchScalarGridSpec(
        num_scalar_prefetch=2,
        grid=(nreq, nheads // hps),
        in_specs=[head, head, head] + pages + pages,
        out_specs=pl.BlockSpec((None, 1, hps * hd), lambda b, g, tbl, sel: (b, 0, g)),
    )
    n_in = npg * hps
    return pl.pallas_call(
        functools.partial(_moba_sample_attn_kernel, npg=npg, hps=hps, scale=hd ** -0.5),
        grid_spec=grid_spec,
        out_shape=jax.ShapeDtypeStruct((nreq, 1, nheads * hd), F32),
        compiler_params=_cparams(2, 48),
        name="moba_sample_attn",
    )(tbl_flat, sel_flat, q, k_new, v_new, *([cache_k] * n_in), *([cache_v] * n_in))


def _mlstm_sample_kernel(q_ref, k_ref, v_ref, og_ref, g_ref, gb_ref, ng_ref, c0_ref, n0_ref, m0_ref,
                         h_ref, c_ref, n_ref, m_ref, *, nh, dk, dv):
    b = pl.program_id(0)
    rr = lax.broadcasted_iota(jnp.int32, (dk, dk), 0)
    cc = lax.broadcasted_iota(jnp.int32, (dk, dk), 1)
    eye = rr == cc
    gates = g_ref[pl.ds(b, 1), :] + gb_ref[...]
    q_all = q_ref[pl.ds(b, 1), :]
    k_all = k_ref[pl.ds(b, 1), :]
    v_all = v_ref[pl.ds(b, 1), :]
    og_all = og_ref[pl.ds(b, 1), :]
    for h in range(nh):
        ii = gates[:, h:h + 1]
        fpre = gates[:, nh + h:nh + h + 1]
        ff = jnp.minimum(fpre, 0.0) - jnp.log1p(jnp.exp(-jnp.abs(fpre)))
        m_prev = m0_ref[h][:, 0:1]
        q = q_all[:, h * dk:(h + 1) * dk]
        k = k_all[:, h * dk:(h + 1) * dk] * (dk ** -0.5)
        v = v_all[:, h * dv:(h + 1) * dv]
        inter = ff + m_prev
        m_t = jnp.maximum(ii, inter)
        s = jnp.sum(q * k, axis=1, keepdims=True) * jnp.exp(ii - m_t)
        wi = jnp.exp(inter - m_t)
        q_col = jnp.sum(jnp.where(eye, q, 0.0), axis=1, keepdims=True)
        k_col = jnp.sum(jnp.where(eye, k, 0.0), axis=1, keepdims=True)
        cst = c0_ref[h]
        nrow = n0_ref[h]
        num = wi * jnp.sum(q_col * cst, axis=0, keepdims=True) + s * v
        den = wi * jnp.sum(q * nrow, axis=1, keepdims=True) + s
        hh = num / jnp.maximum(jnp.abs(den), jnp.exp(-m_t))
        m_new = m_t
        wc = jnp.exp(inter - m_new)
        wg = jnp.exp(ii - m_new)
        c_ref[h] = wc * cst + (wg * k_col) * v
        n_ref[h] = wc * nrow + wg * k
        m_ref[h] = jnp.broadcast_to(m_new, (1, LANES))
        hn = hh * lax.rsqrt(jnp.mean(hh * hh, axis=1, keepdims=True) + LN_EPS)
        hn = hn * ng_ref[:, h * dv:(h + 1) * dv] * _sigmoid(og_all[:, h * dv:(h + 1) * dv])
        h_ref[:, h * dv:(h + 1) * dv] = hn.astype(h_ref.dtype)


def _mlstm_sample(zm, gates, gate_bias, norm_g, c0, n0, m0, layer, nreq):
    nh = B_HEADS
    v_w = norm_g.shape[1]
    qk_w = v_w // 2
    dk, dv = qk_w // nh, v_w // nh
    rows = zm.shape[0]
    full = lambda shape: pl.BlockSpec(shape, lambda b: (0,) * len(shape))
    return pl.pallas_call(
        functools.partial(_mlstm_sample_kernel, nh=nh, dk=dk, dv=dv),
        grid=(nreq,),
        in_specs=[
            pl.BlockSpec((rows, qk_w), lambda b: (0, 0)),
            pl.BlockSpec((rows, qk_w), lambda b: (0, 1)),
            pl.BlockSpec((rows, v_w), lambda b: (0, 1)),
            pl.BlockSpec((rows, v_w), lambda b: (0, 2)),
            full(gates.shape), full(gate_bias.shape), full((1, v_w)),
            pl.BlockSpec((None, None, nh, dk, dv), lambda b: (layer, b, 0, 0, 0)),
            pl.BlockSpec((None, None, nh, 1, dk), lambda b: (layer, b, 0, 0, 0)),
            pl.BlockSpec((None, None, nh, 1, LANES), lambda b: (layer, b, 0, 0, 0)),
        ],
        out_specs=[
            pl.BlockSpec((None, 1, v_w), lambda b: (b, 0, 0)),
            pl.BlockSpec((None, nh, dk, dv), lambda b: (b, 0, 0, 0)),
            pl.BlockSpec((None, nh, 1, dk), lambda b: (b, 0, 0, 0)),
            pl.BlockSpec((None, nh, 1, LANES), lambda b: (b, 0, 0, 0)),
        ],
        out_shape=[
            jax.ShapeDtypeStruct((nreq, 1, v_w), F32),
            jax.ShapeDtypeStruct((nreq, nh, dk, dv), F32),
            jax.ShapeDtypeStruct((nreq, nh, 1, dk), F32),
            jax.ShapeDtypeStruct((nreq, nh, 1, LANES), F32),
        ],
        compiler_params=_cparams(1, 32),
        name="mlstm_sample",
    )(zm, zm, zm, zm, gates, gate_bias, norm_g, c0, n0, m0)


def _rglru_sample_kernel(gate_ref, xr_ref, buf_ref, h0_ref, cw_ref, cb_ref, wa_ref, ba_ref, wx_ref,
                         bx_ref, lam_ref, y_ref, h_ref, nbuf_ref):
    nbuf = CONV_WIDTH - 1
    xr = xr_ref[...]
    xconv = cb_ref[...] + buf_ref[0] * cw_ref[0:1, :]
    for j in range(1, nbuf):
        xconv = xconv + buf_ref[j] * cw_ref[j:j + 1, :]
    xconv = xconv + xr * cw_ref[nbuf:nbuf + 1, :]
    a, bterm = _rglru_gates(xconv, wa_ref, ba_ref[...], wx_ref, bx_ref[...], lam_ref[...])
    h = a * h0_ref[...] + bterm
    h_ref[...] = h
    y = _gelu_tanh(gate_ref[...]) * h
    pad_rows = y_ref.shape[0] - y.shape[0]
    y_ref[...] = jnp.concatenate([y, jnp.zeros((pad_rows, y.shape[1]), F32)], axis=0).astype(y_ref.dtype)
    for j in range(nbuf - 1):
        nbuf_ref[j] = buf_ref[j + 1]
    nbuf_ref[nbuf - 1] = xr


def _rglru_sample(zr, layer, buf_t, h0, conv_w, conv_b, w_a, b_a, w_x, b_x, lam, nreq):
    rows, w2 = zr.shape
    width = w2 // 2
    nblk, bd = w_a.shape[1], w_a.shape[2]
    nbuf = CONV_WIDTH - 1
    vec = lambda a: a.reshape(a.shape[0], 1, width)
    vspec = pl.BlockSpec((None, 1, width), lambda i: (layer, 0, 0))
    wspec = pl.BlockSpec((None, nblk, bd, bd), lambda i: (layer, 0, 0, 0))
    return pl.pallas_call(
        _rglru_sample_kernel,
        grid=(1,),
        in_specs=[
            pl.BlockSpec((nreq, width), lambda i: (0, 0)),
            pl.BlockSpec((nreq, width), lambda i: (0, 1)),
            pl.BlockSpec((None, nbuf, nreq, width), lambda i: (layer, 0, 0, 0)),
            pl.BlockSpec((None, nreq, width), lambda i: (layer, 0, 0)),
            pl.BlockSpec((None, CONV_WIDTH, width), lambda i: (layer, 0, 0)),
            vspec, wspec, vspec, wspec, vspec, vspec,
        ],
        out_specs=[
            pl.BlockSpec((rows, width), lambda i: (0, 0)),
            pl.BlockSpec((nreq, width), lambda i: (0, 0)),
            pl.BlockSpec((nbuf, nreq, width), lambda i: (0, 0, 0)),
        ],
        out_shape=[
            jax.ShapeDtypeStruct((rows, width), BF16),
            jax.ShapeDtypeStruct((nreq, width), F32),
            jax.ShapeDtypeStruct((nbuf, nreq, width), F32),
        ],
        compiler_params=_cparams(1, 32),
        name="rglru_sample",
    )(zr, zr, buf_t, h0, conv_w, vec(conv_b), w_a, vec(b_a), w_x, vec(b_x), vec(lam))


def _rope_tables(pos):
    half = A_HEAD_DIM // 2
    inv = ROPE_THETA ** (-jnp.arange(half, dtype=F32) / half)
    ang = pos.astype(F32)[:, None] * inv[None, :]
    cos, sin = jnp.cos(ang), jnp.sin(ang)
    return jnp.concatenate([cos, cos], axis=1), jnp.concatenate([-sin, sin], axis=1)


def kernel(x_prompt, x_sample, cache_k, cache_v, state_mlstm_c, state_mlstm_n, state_mlstm_m,
           state_rglru_h, state_conv, page_table, c_prompt, c_sample, w_ada, b_ada, ln_g, ln_b,
           w_ffn1_in, w_ffn1_out, w_ffn2_in, w_ffn2_out, w_in_ab, b_igate, b_fgate, mlstm_norm_g,
           w_out_ab, w_in_rg, conv_w, conv_b, w_rg_a, b_rg_a, w_rg_x, b_rg_x, lru_lambda, w_out_rg):
    bsz, seq, d = x_prompt.shape
    nreq = x_sample.shape[0]
    depth = w_ada.shape[0]
    n_even = w_in_ab.shape[0]
    a_w = cache_k.shape[3] * cache_k.shape[4]
    v_w = mlstm_norm_g.shape[1]
    qk_w = v_w // 2
    m_w = 2 * qk_w + 2 * v_w
    nh_b = B_HEADS
    past_len = page_table.shape[1] * cache_k.shape[2]
    npages = page_table.shape[1]
    rows_p = bsz * seq
    rs = SAMPLE_ROWS

    tm = 1024
    tn = 512
    tm_p = 1024
    tn_p = 1024
    tm_out = 256
    tk_cast = 512
    tt_rg = 256

    c_all = jnp.concatenate([c_prompt, c_sample, jnp.zeros((rs - bsz - nreq, d), F32)], axis=0)
    mods = _mods(c_all, w_ada, b_ada).reshape(depth, rs, 9, d)
    mods_p = jnp.transpose(mods[:, :bsz], (0, 2, 1, 3)).reshape(depth * 9 * bsz, 1, d)
    mods_s = jnp.transpose(mods[:, bsz:bsz + nreq], (0, 2, 1, 3))
    mods_s = jnp.pad(mods_s, ((0, 0), (0, 0), (0, rs - nreq), (0, 0))).reshape(depth * 9, rs, d)

    def mod_spec_p(l, s, j):
        base = ((l * 3 + s) * 3 + j) * bsz

        def make(tile_rows):
            tiles_per_batch = seq // tile_rows
            return pl.BlockSpec((None, 1, d), lambda i: (base + i // tiles_per_batch, 0, 0))
        return make

    def mod_spec_s(l, s, j):
        row = (l * 3 + s) * 3 + j

        def make(tile_rows):
            return pl.BlockSpec((None, rs, d), lambda i: (row, 0, 0))
        return make

    ln_g3 = ln_g.reshape(depth * 3, 1, d)
    ln_b3 = ln_b.reshape(depth * 3, 1, d)

    cos_p, sin_p = _rope_tables(jnp.arange(seq, dtype=jnp.int32))
    cos_s, sin_s = _rope_tables(jnp.full((rs,), past_len, dtype=jnp.int32))

    tbl_flat = page_table.reshape(-1).astype(jnp.int32)
    n0_s = state_mlstm_n[:, :, :, None, :]
    m0_s = jnp.broadcast_to(state_mlstm_m[:, :, :, None, None], state_mlstm_m.shape + (1, LANES))
    conv_t = jnp.transpose(state_conv, (0, 2, 1, 3))
    gate_bias = jnp.concatenate([b_igate, b_fgate], axis=1)
    w_ab_t = jnp.swapaxes(w_in_ab, 1, 2)

    xp = x_prompt.reshape(rows_p, d)
    xs = jnp.pad(x_sample.reshape(nreq, d), ((0, rs - nreq), (0, 0)))
    up = _modulate(xp, mod_spec_p(0, 0, 1), mod_spec_p(0, 0, 0), mods_p, tm)
    us = _modulate(xs, mod_spec_s(0, 0, 1), mod_spec_s(0, 0, 0), mods_s, rs)

    op = dict(k=[], v=[], c=[], n=[], m=[], h=[], buf=[])
    os_ = dict(k=[], v=[], c=[], n=[], m=[], h=[], buf=[])
    rope = (cos_p, sin_p, cos_s, sin_s)
    k_st = v_st = ()
    pad_rows = lambda t, w: jnp.pad(t.reshape(nreq, w), ((0, rs - nreq), (0, 0))).astype(BF16)
    heads3 = lambda t: t.reshape(rs, -1, A_HEAD_DIM)

    for l in range(depth):
        wb_ffn1 = _cast_bf16(w_ffn1_out, l, tk_cast)
        wb_ffn2 = _cast_bf16(w_ffn2_out, l, tk_cast)
        wb_mix = _cast_bf16(w_out_ab if l % 2 == 0 else w_out_rg, l // 2, tk_cast)

        def post(parts_p, parts_s, wb, s, res_w, nxt):
            nl, ns = nxt
            pick = lambda spec: [spec(l, s, 2), spec(nl, ns, 1), spec(nl, ns, 0)]
            xo, uo, xso, uso = _out_ln(parts_p, parts_s, wb, xp, xs, mods_p, mods_s, pick(mod_spec_p),
                                       pick(mod_spec_s), ln_g3, ln_b3, l * 3 + s, res_w,
                                       tm_out if wb.shape[0] > d else 2 * tm_out)
            return (xo, uo), (xso, uso)

        act_p, act_s = _swiglu_in(up, us, w_ffn1_in, l, tm, tn)
        (xp, up), (xs, us) = post([act_p], [act_s], wb_ffn1, 0, FFN_RES, (l, 1))

        if l % 2 == 0:
            e = l // 2
            q_p, q_s = _proj(up, us, w_ab_t, e, 0, a_w, tm_p, tn_p, F32, rope=rope, w_t=True, name="proj_q")
            tm_kv = tm_p if e == 0 else tm_p // 2
            k_st = _proj(up, us, w_ab_t, e, a_w, a_w, tm_kv, tn_p, F32, rope=rope, w_t=True,
                         stack=k_st, name="proj_k")
            v_st = _proj(up, us, w_ab_t, e, 2 * a_w, a_w, tm_kv, tn_p, F32, w_t=True,
                         stack=v_st, name="proj_v")
            k_s, v_s = k_st[1][e], v_st[1][e]
            zm_p, zm_s, g_p, g_s = _proj(up, us, w_ab_t, e, 3 * a_w, m_w, tm_p, tn_p, F32, w_t=True,
                                         narrow=(3 * a_w + m_w, 2 * nh_b), name="proj_mlstm")
            g_p, g_s = g_p[0], g_s[0]
            gb = gate_bias[e:e + 1]
            ng = mlstm_norm_g[e:e + 1]

            oa = _moba_prompt(q_p, k_st[0], v_st[0], e, bsz)
            hm, c1, n1, m1, ksum = _mlstm_prompt(zm_p, g_p, gb, ng, bsz, cache_k, e, tbl_flat)
            mix_p = [oa, hm.reshape(rows_p, v_w)]
            op["c"].append(c1.reshape(bsz, nh_b, c1.shape[1], c1.shape[2]))
            op["n"].append(n1.reshape(bsz, nh_b, -1))
            op["m"].append(m1[:, 0, 0].reshape(bsz, nh_b))

            sel = _moba_sample_gate(q_s, ksum, nreq)
            oa = _moba_sample_attn(heads3(q_s), heads3(k_s), heads3(v_s), cache_k, cache_v, e,
                                   tbl_flat, sel.reshape(-1), nreq, npages)
            hm, c1, n1, m1 = _mlstm_sample(zm_s, g_s, gb, ng, state_mlstm_c, n0_s, m0_s, e, nreq)
            mix_s = [pad_rows(oa, a_w), pad_rows(hm, v_w)]
            os_["c"].append(c1)
            os_["n"].append(n1.reshape(nreq, nh_b, -1))
            os_["m"].append(m1[:, :, 0, 0])
        else:
            od = l // 2
            zr_p, zr_s = _proj(up, us, w_in_rg, od, 0, w_in_rg.shape[2], tm_p, tn_p, F32, name="proj_rg")
            y_p, h1, nb = _rglru_prompt(zr_p, od, conv_w, conv_b, w_rg_a, b_rg_a, w_rg_x, b_rg_x,
                                        lru_lambda, bsz, tt_rg)
            op["h"].append(h1.reshape(bsz, -1))
            op["buf"].append(nb)
            y_s, h1, nb = _rglru_sample(zr_s, od, conv_t, state_rglru_h, conv_w, conv_b, w_rg_a,
                                        b_rg_a, w_rg_x, b_rg_x, lru_lambda, nreq)
            os_["h"].append(h1)
            os_["buf"].append(jnp.transpose(nb, (1, 0, 2)))
            mix_p, mix_s = [y_p], [y_s]
        (xp, up), (xs, us) = post(mix_p, mix_s, wb_mix, 1, 1.0, (l, 2))

        act_p, act_s = _swiglu_in(up, us, w_ffn2_in, l, tm, tn)
        (xp, up), (xs, us) = post([act_p], [act_s], wb_ffn2, 2, FFN_RES, (min(l + 1, depth - 1), 0))

    st = jnp.stack
    kv_p = lambda t: t.reshape(n_even, bsz, seq, -1, A_HEAD_DIM)
    kv_s = lambda t: t[:, :nreq].reshape(n_even, nreq, 1, -1, A_HEAD_DIM)
    return (xp.reshape(bsz, seq, d), xs[:nreq].reshape(nreq, 1, d),
            kv_p(k_st[0]), kv_p(v_st[0]), kv_s(k_st[1]), kv_s(v_st[1]),
            st(op["c"]), st(op["n"]), st(op["m"]), st(os_["c"]), st(os_["n"]), st(os_["m"]),
            st(op["h"]), st(op["buf"]), st(os_["h"]), st(os_["buf"]))
```
